```python
import jax, jax.numpy as jnp
from jax import lax
import numpy as np

D_MODEL = 2048
BATCH = 4
SEQ = 4096
DEPTH = 4
DEC_BATCH = 8
DEC_SEQ = 4096
PAST_LEN = 128

N_META = 16
ML_HEADS = 4
ML_HEAD_DIM = 256
ML_WIDTH = ML_HEADS * ML_HEAD_DIM
ML_CHUNK = 64
META_PAD = ML_CHUNK - N_META
N_GATES = 4 * ML_HEADS
I_GATE_BIAS = 0.0
F_GATE_BIAS = 3.0
MLA_HEADS = 8
Q_LORA = 512
KV_LORA = 256
QK_NOPE = 128
QK_ROPE = 64
V_HEAD = 128
MLA_WIDTH = MLA_HEADS * V_HEAD
ROPE_THETA = 10000.0
ATTN_BLOCK = 128
MIX_WIDTH = ML_WIDTH + MLA_WIDTH
IN_COLS = 4 * ML_WIDTH + N_GATES + Q_LORA + KV_LORA + QK_ROPE
IN_SPLITS = (ML_WIDTH, 2 * ML_WIDTH, 3 * ML_WIDTH, 4 * ML_WIDTH,
             4 * ML_WIDTH + N_GATES,
             4 * ML_WIDTH + N_GATES + Q_LORA,
             4 * ML_WIDTH + N_GATES + Q_LORA + KV_LORA)
D_FF = 5632
CONV_W = 3
ALPHA = (2 * DEPTH) ** 0.25
BETA = (8 * DEPTH) ** -0.25
EPS = 1e-5
NEG = -1e30

kernel_name = 'hybrid_mlstm_mla_convffn_encoder'


def _layer_norm(x, g, b):
    xf = x.astype(jnp.float32)
    mu = jnp.mean(xf, axis=-1, keepdims=True)
    var = jnp.mean(jnp.square(xf - mu), axis=-1, keepdims=True)
    y = (xf - mu) * lax.rsqrt(var + EPS) * g.astype(jnp.float32) + b.astype(jnp.float32)
    return y.astype(x.dtype)


def _rms_norm(x, g):
    xf = x.astype(jnp.float32)
    y = xf * lax.rsqrt(jnp.mean(jnp.square(xf), axis=-1, keepdims=True) + EPS) * g.astype(jnp.float32)
    return y.astype(x.dtype)


def _rope_tables(seqlen):
    inv_freq = ROPE_THETA ** (-jnp.arange(0, QK_ROPE, 2, dtype=jnp.float32) / QK_ROPE)
    ang = jnp.arange(seqlen, dtype=jnp.float32)[:, None] * inv_freq[None, :]
    return jnp.cos(ang), jnp.sin(ang)


def _rope(x, cos, sin):
    half = QK_ROPE // 2
    xf = x.astype(jnp.float32)
    x1, x2 = xf[..., :half], xf[..., half:]
    return jnp.concatenate([x1 * cos - x2 * sin, x1 * sin + x2 * cos], axis=-1).astype(x.dtype)


def _mlstm_scan(q, k, v, log_i, log_f):
    bsz, nh, plen, dh = q.shape
    n_chunks = plen // ML_CHUNK

    def to_chunks(a):
        a = a.reshape(a.shape[:2] + (n_chunks, ML_CHUNK) + a.shape[3:])
        return jnp.moveaxis(a, 2, 0)

    lower = jnp.tril(jnp.ones((ML_CHUNK, ML_CHUNK), dtype=bool))

    def step(carry, inp):
        c_state, n_state, m_state = carry
        qc, kc, vc, lic, lfc = inp
        b = jnp.cumsum(lfc, axis=-1)
        d_mat = b[..., :, None] - b[..., None, :] + lic[..., None, :]
        d_mat = jnp.where(lower, d_mat, -jnp.inf)
        m_inter = b + m_state[..., None]
        m_t = jnp.maximum(m_inter, jnp.max(d_mat, axis=-1))
        w_inter = jnp.exp(m_inter - m_t)
        s = jnp.einsum('bhtd,bhsd->bhts', qc, kc) * jnp.exp(d_mat - m_t[..., None])
        num = (w_inter[..., None] * jnp.einsum('bhtd,bhde->bhte', qc, c_state)
               + jnp.einsum('bhts,bhse->bhte', s, vc))
        den = w_inter * jnp.einsum('bhtd,bhd->bht', qc, n_state) + jnp.sum(s, axis=-1)
        h = num / jnp.maximum(jnp.abs(den), jnp.exp(-m_t))[..., None]
        b_last = b[..., -1]
        g = b_last[..., None] - b + lic
        m_new = jnp.maximum(b_last + m_state, jnp.max(g, axis=-1))
        a = jnp.exp(b_last + m_state - m_new)
        w = jnp.exp(g - m_new[..., None])
        c_new = a[..., None, None] * c_state + jnp.einsum('bhs,bhsd,bhse->bhde', w, kc, vc)
        n_new = a[..., None] * n_state + jnp.einsum('bhs,bhsd->bhd', w, kc)
        return (c_new, n_new, m_new), h

    init = (jnp.zeros((bsz, nh, dh, dh), jnp.float32),
            jnp.zeros((bsz, nh, dh), jnp.float32),
            jnp.zeros((bsz, nh), jnp.float32))
    _, h = lax.scan(step, init, (to_chunks(q), to_chunks(k), to_chunks(v),
                                 to_chunks(log_i), to_chunks(log_f)))
    return jnp.moveaxis(h, 0, 2).reshape(bsz, nh, plen, dh)


def _mlstm_group(u_q, u_k, u_v, u_o, u_gate, b_gate, norm_g):
    bsz, seqlen, _ = u_q.shape
    f32 = jnp.float32

    def heads(a):
        return a.astype(f32).reshape(bsz, seqlen, ML_HEADS, ML_HEAD_DIM).transpose(0, 2, 1, 3)

    def pad(a, value=0.0):
        widths = ((0, 0), (0, 0), (META_PAD, 0)) + ((0, 0),) * (a.ndim - 3)
        return jnp.pad(a, widths, constant_values=value)

    def flip(a):
        return jnp.flip(a, axis=2)

    q = pad(heads(u_q))
    k = pad(heads(u_k) * ML_HEAD_DIM ** -0.5)
    v = pad(heads(u_v))
    gates = (u_gate.astype(f32) + b_gate.astype(f32)).transpose(0, 2, 1)
    i_fw, f_fw, i_bw, f_bw = jnp.split(gates, 4, axis=1)
    h_fw = _mlstm_scan(q, k, v, pad(i_fw, NEG), pad(jax.nn.log_sigmoid(f_fw)))
    h_bw = flip(_mlstm_scan(flip(q), flip(k), flip(v),
                            flip(pad(i_bw, NEG)), flip(pad(jax.nn.log_sigmoid(f_bw)))))
    h = (h_fw + h_bw)[:, :, META_PAD:]
    mu = jnp.mean(h, axis=-1, keepdims=True)
    var = jnp.mean(jnp.square(h - mu), axis=-1, keepdims=True)
    h = (h - mu) * lax.rsqrt(var + EPS) * norm_g.astype(f32).reshape(ML_HEADS, 1, ML_HEAD_DIM)
    h = h.transpose(0, 2, 1, 3).reshape(bsz, seqlen, ML_WIDTH)
    return (jax.nn.sigmoid(u_o.astype(f32)) * h).astype(u_o.dtype)


def _mla_group(u_dq, u_dkv, u_kr, q_norm_g, kv_norm_g, w_uq, w_ukv, cos, sin):
    bsz, seqlen, _ = u_dq.shape
    q = (_rms_norm(u_dq, q_norm_g) @ w_uq).reshape(bsz, seqlen, MLA_HEADS, QK_NOPE + QK_ROPE)
    q_nope = q[..., :QK_NOPE]
    q_rope = _rope(q[..., QK_NOPE:], cos[:, None, :], sin[:, None, :])
    kv = (_rms_norm(u_dkv, kv_norm_g) @ w_ukv).reshape(bsz, seqlen, MLA_HEADS, QK_NOPE + V_HEAD)
    k_nope, v = kv[..., :QK_NOPE], kv[..., QK_NOPE:]
    k_rope = _rope(u_kr, cos, sin)
    scale = (QK_NOPE + QK_ROPE) ** -0.5

    def attend(blk):
        qn, qr = blk
        s = (jnp.einsum('bthd,bshd->bhts', qn, k_nope)
             + jnp.einsum('bthd,bsd->bhts', qr, k_rope))
        p = jax.nn.softmax(s.astype(jnp.float32) * scale, axis=-1).astype(v.dtype)
        return jnp.einsum('bhts,bshd->bthd', p, v)

    o_meta = attend((q_nope[:, :N_META], q_rope[:, :N_META]))
    n_real = seqlen - N_META
    n_blocks = n_real // ATTN_BLOCK

    def to_blocks(a):
        a = a[:, N_META:].reshape((bsz, n_blocks, ATTN_BLOCK) + a.shape[2:])
        return jnp.moveaxis(a, 1, 0)

    o_real = lax.map(attend, (to_blocks(q_nope), to_blocks(q_rope)))
    o_real = jnp.moveaxis(o_real, 0, 1).reshape(bsz, n_real, MLA_HEADS, V_HEAD)
    return jnp.concatenate([o_meta, o_real], axis=1).reshape(bsz, seqlen, MLA_WIDTH)


def _dwconv(g, w, b):
    seqlen = g.shape[1]
    half = CONV_W // 2
    gp = jnp.pad(g, ((0, 0), (half, half), (0, 0)))
    out = gp[:, 0:seqlen] * w[0]
    for j in range(1, CONV_W):
        out = out + gp[:, j:j + seqlen] * w[j]
    return out + b


def _layer(h, cos, sin, w_in, b_gate, ml_norm_g, q_norm_g, kv_norm_g, w_uq, w_ukv, w_out,
           ln1_g, ln1_b, w_up, conv_w, conv_b, w_down, ln2_g, ln2_b):
    u = h @ w_in
    u_q, u_k, u_v, u_o, u_gate, u_dq, u_dkv, u_kr = jnp.split(u, IN_SPLITS, axis=-1)
    mix = jnp.concatenate([
        _mlstm_group(u_q, u_k, u_v, u_o, u_gate, b_gate, ml_norm_g),
        _mla_group(u_dq, u_dkv, u_kr, q_norm_g, kv_norm_g, w_uq, w_ukv, cos, sin),
    ], axis=-1) @ w_out
    h = _layer_norm(ALPHA * h + mix, ln1_g, ln1_b)
    gate, val = jnp.split(h @ w_up, 2, axis=-1)
    ffn = (jax.nn.silu(_dwconv(gate, conv_w, conv_b)) * val) @ w_down
    return _layer_norm(ALPHA * h + ffn, ln2_g, ln2_b)


def _trunk(x, meta_tokens, params):
    bsz = x.shape[0]
    meta = jnp.broadcast_to(meta_tokens.astype(x.dtype)[None], (bsz, N_META, D_MODEL))
    h = jnp.concatenate([meta, x], axis=1)
    cos, sin = _rope_tables(h.shape[1])
    for l in range(DEPTH):
        h = _layer(h, cos, sin, *[p[l] for p in params])
    return h[:, N_META:]


def setup_inputs(seed: int = 0) -> dict:
    key = jax.random.key(seed)
    ks = jax.random.split(key, 24)

    def nrm(k, shape, scale):
        return jax.random.normal(k, shape, jnp.float32) * scale

    gate_offset = jnp.repeat(jnp.array([I_GATE_BIAS, F_GATE_BIAS, I_GATE_BIAS, F_GATE_BIAS],
                                       jnp.float32), ML_HEADS)
    return {
        'x_prompt': nrm(ks[0], (BATCH, SEQ, D_MODEL), 1.0),
        'x_sample': nrm(ks[1], (DEC_BATCH, DEC_SEQ, D_MODEL), 1.0),
        'meta_tokens': nrm(ks[2], (N_META, D_MODEL), 1.0),
        'w_in': nrm(ks[3], (DEPTH, D_MODEL, IN_COLS), D_MODEL ** -0.5),
        'b_gates': gate_offset + nrm(ks[4], (DEPTH, N_GATES), 0.1),
        'ml_norm_g': 1.0 + nrm(ks[5], (DEPTH, ML_WIDTH), 0.02),
        'q_norm_g': 1.0 + nrm(ks[6], (DEPTH, Q_LORA), 0.02),
        'kv_norm_g': 1.0 + nrm(ks[7], (DEPTH, KV_LORA), 0.02),
        'w_uq': nrm(ks[8], (DEPTH, Q_LORA, MLA_HEADS * (QK_NOPE + QK_ROPE)), Q_LORA ** -0.5),
        'w_ukv': nrm(ks[9], (DEPTH, KV_LORA, MLA_HEADS * (QK_NOPE + V_HEAD)), KV_LORA ** -0.5),
        'w_out': nrm(ks[10], (DEPTH, MIX_WIDTH, D_MODEL), MIX_WIDTH ** -0.5 * BETA),
        'ln1_g': 1.0 + nrm(ks[11], (DEPTH, D_MODEL), 0.02),
        'ln1_b': nrm(ks[12], (DEPTH, D_MODEL), 0.02),
        'w_up': nrm(ks[13], (DEPTH, D_MODEL, 2 * D_FF), D_MODEL ** -0.5),
        'conv_w': nrm(ks[14], (DEPTH, CONV_W, D_FF), CONV_W ** -0.5),
        'conv_b': nrm(ks[15], (DEPTH, D_FF), 0.02),
        'w_down': nrm(ks[16], (DEPTH, D_FF, D_MODEL), D_FF ** -0.5 * BETA),
        'ln2_g': 1.0 + nrm(ks[17], (DEPTH, D_MODEL), 0.02),
        'ln2_b': nrm(ks[18], (DEPTH, D_MODEL), 0.02),
    }


def reference(x_prompt, x_sample, meta_tokens, w_in, b_gates, ml_norm_g, q_norm_g, kv_norm_g,
              w_uq, w_ukv, w_out, ln1_g, ln1_b, w_up, conv_w, conv_b, w_down, ln2_g, ln2_b):
    params = (w_in, b_gates, ml_norm_g, q_norm_g, kv_norm_g, w_uq, w_ukv, w_out,
              ln1_g, ln1_b, w_up, conv_w, conv_b, w_down, ln2_g, ln2_b)
    y_prompt = _trunk(x_prompt, meta_tokens, params)
    y_sample = _trunk(x_sample, meta_tokens, params)
    return (y_prompt, y_sample)
```

```python
import functools
import math

import jax
import jax.numpy as jnp
from jax import lax
from jax.experimental import pallas as pl
from jax.experimental.pallas import tpu as pltpu

N_META = 16
ML_HEADS = 4
ML_HEAD_DIM = 256
ML_WIDTH = ML_HEADS * ML_HEAD_DIM
N_GATES = 4 * ML_HEADS
MLA_HEADS = 8
Q_LORA = 512
KV_LORA = 256
QK_NOPE = 128
QK_ROPE = 64
V_HEAD = 128
MLA_WIDTH = MLA_HEADS * V_HEAD
ROPE_THETA = 10000.0
EPS = 1e-5
NEG = -1e30

LANE = 128
FRONT = LANE - N_META
CHUNK = 128
ROPE_HALF = QK_ROPE // 2
MIB = 1024 * 1024

COL_Q, COL_K, COL_V, COL_O = 0, ML_WIDTH, 2 * ML_WIDTH, 3 * ML_WIDTH
COL_DQ = 4 * ML_WIDTH
COL_DKV = COL_DQ + Q_LORA
COL_KR = COL_DKV + KV_LORA
N_MAIN = COL_KR + 2 * QK_ROPE + LANE

f32 = jnp.float32
bf16 = jnp.bfloat16


def _pick(n, cands):
    for c in cands:
        if n % c == 0:
            return c
    raise ValueError(f"no tile for {n} in {cands}")


def _params(sem, vmem_mib):
    return pltpu.CompilerParams(dimension_semantics=sem, vmem_limit_bytes=vmem_mib * MIB)


def _dot(a, b):
    return jnp.dot(a, b, preferred_element_type=f32)


def _dot_nt(a, b):
    return lax.dot_general(a, b, (((1,), (1,)), ((), ())), preferred_element_type=f32)


def _inproj_kernel(a_ref, w_ref, o_ref):
    o_ref[...] = _dot(a_ref[...], w_ref[...]).astype(o_ref.dtype)


def _inproj(hb, w_main, layer):
    r, d = hb.shape
    n = w_main.shape[-1]
    tm = _pick(r, (768, 384, 128))
    tn = _pick(n, (1280, 1024, 512, 256, 128))
    return pl.pallas_call(
        _inproj_kernel,
        grid=(r // tm, n // tn),
        in_specs=[pl.BlockSpec((tm, d), lambda i, j: (i, 0)),
                  pl.BlockSpec((None, d, tn), lambda i, j: (layer, 0, j))],
        out_specs=pl.BlockSpec((tm, tn), lambda i, j: (i, j)),
        out_shape=jax.ShapeDtypeStruct((r, n), bf16),
        compiler_params=_params(("parallel", "arbitrary"), 48),
        name="inproj",
    )(hb, w_main)


def _gates_kernel(h_ref, wg_ref, bg_ref, o_ref):
    tm = h_ref.shape[0]
    half = N_GATES // 2
    g = _dot_nt(wg_ref[...], h_ref[...]) + bg_ref[...]
    pos = pl.program_id(1) * tm + lax.broadcasted_iota(jnp.int32, (half, tm), 1)
    unused = pos < FRONT
    gi, gf = g[:half], g[half:]
    li = jnp.where(unused, NEG, gi)
    lf = jnp.where(unused, 0.0, jnp.minimum(gf, 0.0) - jnp.log1p(jnp.exp(-jnp.abs(gf))))
    lane = lax.broadcasted_iota(jnp.int32, (half, CHUNK), 1)
    is_fwd = lax.broadcasted_iota(jnp.int32, (half, CHUNK), 0) < ML_HEADS
    for c in range(tm // CHUNK):
        sl = slice(c * CHUNK, (c + 1) * CHUNK)
        pre = suf = lf[:, sl]
        k = 1
        while k < CHUNK:
            pre = pre + jnp.where(lane >= k, pltpu.roll(pre, k, 1), 0.0)
            suf = suf + jnp.where(lane < CHUNK - k, pltpu.roll(suf, CHUNK - k, 1), 0.0)
            k *= 2
        b = jnp.where(is_fwd, pre, suf)
        o_ref[:half, sl] = li[:, sl] - b
        o_ref[half:, sl] = b


def _gates(hb, wg_t, bg, layer, bsz, lp):
    r, d = hb.shape
    tm = _pick(lp, (1408, 384, 128))
    nt = lp // tm
    return pl.pallas_call(
        _gates_kernel,
        grid=(bsz, nt),
        in_specs=[pl.BlockSpec((tm, d), lambda b, i: (b * nt + i, 0)),
                  pl.BlockSpec((None, N_GATES, d), lambda b, i: (layer, 0, 0)),
                  pl.BlockSpec((None, N_GATES, 1), lambda b, i: (layer, 0, 0))],
        out_specs=pl.BlockSpec((None, N_GATES, tm), lambda b, i: (b, 0, i)),
        out_shape=jax.ShapeDtypeStruct((bsz, N_GATES, lp), f32),
        compiler_params=_params(("parallel", "arbitrary"), 32),
        name="gates",
    )(hb, wg_t, bg)


def _mlstm_kernel(q_ref, k_ref, v_ref, o_ref, gq_ref, ng_ref, out_ref,
                  hf_scr, hb_scr, c_scr, n_scr, m_scr):
    head = pl.program_id(1)
    lp, dh = q_ref.shape
    t = CHUNK
    nc = lp // t
    c_scr[...] = jnp.zeros_like(c_scr)
    n_scr[...] = jnp.zeros_like(n_scr)
    m_scr[...] = jnp.zeros_like(m_scr)
    ti = lax.broadcasted_iota(jnp.int32, (t, t), 0)
    si = lax.broadcasted_iota(jnp.int32, (t, t), 1)
    masks = (si <= ti, si >= ti)
    eye = si == ti
    half = N_GATES // 2
    gate_row = lax.broadcasted_iota(jnp.int32, (half, t), 0)

    def chunk(c, dirn, h_scr):
        base = pl.multiple_of(c * t, t)
        q = q_ref[pl.ds(base, t), :]
        k = k_ref[pl.ds(base, t), :] * (ML_HEAD_DIM ** -0.5)
        v = v_ref[pl.ds(base, t), :]
        mine = gate_row == head + ML_HEADS * dirn
        r_row = jnp.sum(jnp.where(mine, gq_ref[:half, pl.ds(base, t)], 0.0), axis=0, keepdims=True)
        b_row = jnp.sum(jnp.where(mine, gq_ref[half:, pl.ds(base, t)], 0.0), axis=0, keepdims=True)
        m_state = m_scr[dirn, 0:1, 0:1]
        mask = masks[dirn]
        m_col = jnp.maximum(jnp.max(jnp.where(mask, r_row, -jnp.inf), axis=1, keepdims=True), m_state)
        b_col = jnp.sum(jnp.where(eye, b_row, 0.0), axis=1, keepdims=True)
        m_all = jnp.maximum(jnp.max(r_row, axis=1, keepdims=True), m_state)
        b_last = b_row[:, 0:1] if dirn else b_row[:, t - 1:t]
        e = jnp.exp(jnp.where(mask, r_row - m_col, -jnp.inf))
        s = _dot_nt(q, k) * e
        w_inter = jnp.exp(m_state - m_col)
        inter = _dot(q, c_scr[dirn].astype(bf16))
        intra = _dot(s.astype(bf16), v)
        n_row = n_scr[dirn, 0:1, :]
        qn = jnp.sum(q.astype(f32) * n_row, axis=1, keepdims=True)
        den = w_inter * qn + jnp.sum(s, axis=1, keepdims=True)
        num = w_inter * inter + intra
        h_scr[pl.ds(base, t), :] = num / jnp.maximum(jnp.abs(den), jnp.exp(-(b_col + m_col)))
        a = jnp.exp(m_state - m_all)
        w_row = jnp.exp(r_row - m_all)
        ktw = (k.astype(f32).T * w_row).astype(bf16)
        c_scr[dirn] = a * c_scr[dirn] + _dot(ktw, v)
        w8 = jnp.broadcast_to(w_row, (8, t)).astype(bf16)
        n_scr[dirn] = a * n_scr[dirn] + _dot(w8, k)
        m_scr[dirn] = jnp.broadcast_to(b_last + m_all, m_scr.shape[1:])

    def body(c, carry):
        chunk(c, 0, hf_scr)
        chunk(nc - 1 - c, 1, hb_scr)
        return carry

    lax.fori_loop(0, nc, body, 0)

    def finish(c, carry):
        base = pl.multiple_of(c * t, t)
        hs = hf_scr[pl.ds(base, t), :] + hb_scr[pl.ds(base, t), :]
        mu = jnp.mean(hs, axis=1, keepdims=True)
        var = jnp.mean(jnp.square(hs - mu), axis=1, keepdims=True)
        hn = (hs - mu) * lax.rsqrt(var + EPS) * ng_ref[...]
        gate = jax.nn.sigmoid(o_ref[pl.ds(base, t), :].astype(f32))
        out_ref[pl.ds(base, t), :] = (gate * hn).astype(out_ref.dtype)
        return carry

    lax.fori_loop(0, nc, finish, 0)


def _mlstm(u, gq, ml_norm_g, layer, bsz, lp):
    r = u.shape[0]
    dh = ML_HEAD_DIM

    def col(first):
        return pl.BlockSpec((lp, dh), lambda b, h: (b, first // dh + h))

    return pl.pallas_call(
        _mlstm_kernel,
        grid=(bsz, ML_HEADS),
        in_specs=[col(COL_Q), col(COL_K), col(COL_V), col(COL_O),
                  pl.BlockSpec((None, N_GATES, lp), lambda b, h: (b, 0, 0)),
                  pl.BlockSpec((None, 1, dh), lambda b, h: (layer, 0, h))],
        out_specs=pl.BlockSpec((lp, dh), lambda b, h: (b, h)),
        out_shape=jax.ShapeDtypeStruct((r, ML_WIDTH), bf16),
        scratch_shapes=[pltpu.VMEM((lp, dh), f32), pltpu.VMEM((lp, dh), f32),
                        pltpu.VMEM((2, dh, dh), f32), pltpu.VMEM((2, 8, dh), f32),
                        pltpu.VMEM((2, 8, LANE), f32)],
        compiler_params=_params(("parallel", "arbitrary"), 48),
        name="mlstm",
    )(u, u, u, u, gq, ml_norm_g)


def _rms(x, g):
    return x * lax.rsqrt(jnp.mean(jnp.square(x), axis=-1, keepdims=True) + EPS) * g


def _rope(x, cos, sin_a, sin_b):
    n = x.shape[1]
    reps = n // LANE
    if reps > 1:
        cos, sin_a, sin_b = (jnp.concatenate([a] * reps, axis=1) for a in (cos, sin_a, sin_b))
    return x * cos + pltpu.roll(x, n - ROPE_HALF, 1) * sin_a + pltpu.roll(x, ROPE_HALF, 1) * sin_b


def _mla_prep_kernel(dq_ref, dkv_ref, kr_ref, wuq_ref, wukv_ref, qg_ref, kvg_ref,
                     cos_ref, sa_ref, sb_ref, qc_ref, kc_ref, v_ref):
    cos, sa, sb = cos_ref[...], sa_ref[...], sb_ref[...]
    q = _dot(_rms(dq_ref[...].astype(f32), qg_ref[...]).astype(bf16), wuq_ref[...])
    kv = _dot(_rms(dkv_ref[...].astype(f32), kvg_ref[...]).astype(bf16), wukv_ref[...])
    q_rope = _rope(q[:, MLA_HEADS * QK_NOPE:], cos, sa, sb)
    k_rope = _rope(kr_ref[...].astype(f32), cos, sa, sb).astype(bf16)
    tm = q.shape[0]
    low = lax.broadcasted_iota(jnp.int32, (tm, LANE), 1) < QK_ROPE
    for h in range(MLA_HEADS):
        pair = q_rope[:, (h // 2) * LANE:(h // 2 + 1) * LANE]
        mine = jnp.where(low if h % 2 == 0 else jnp.logical_not(low), pair, 0.0)
        qc_ref[h, :, :QK_NOPE] = q[:, h * QK_NOPE:(h + 1) * QK_NOPE].astype(bf16)
        qc_ref[h, :, QK_NOPE:] = mine.astype(bf16)
        kc_ref[h, :, :QK_NOPE] = kv[:, h * QK_NOPE:(h + 1) * QK_NOPE].astype(bf16)
        kc_ref[h, :, QK_NOPE:] = k_rope
    v_ref[...] = kv[:, MLA_HEADS * QK_NOPE:].astype(bf16)


def _mla_prep(u, w_uq_p, w_ukv_p, q_norm_g, kv_norm_g, tables, layer, bsz, lp):
    r = u.shape[0]
    tm = _pick(lp, (384, 128))
    nt = lp // tm
    kd = QK_NOPE + LANE
    tab = pl.BlockSpec((tm, LANE), lambda b, i: (i, 0))
    row = lambda b, i: b * nt + i
    return pl.pallas_call(
        _mla_prep_kernel,
        grid=(bsz, nt),
        in_specs=[pl.BlockSpec((tm, Q_LORA), lambda b, i: (row(b, i), COL_DQ // Q_LORA)),
                  pl.BlockSpec((tm, KV_LORA), lambda b, i: (row(b, i), COL_DKV // KV_LORA)),
                  pl.BlockSpec((tm, LANE), lambda b, i: (row(b, i), COL_KR // LANE)),
                  pl.BlockSpec((None,) + w_uq_p.shape[1:], lambda b, i: (layer, 0, 0)),
                  pl.BlockSpec((None,) + w_ukv_p.shape[1:], lambda b, i: (layer, 0, 0)),
                  pl.BlockSpec((None, 1, Q_LORA), lambda b, i: (layer, 0, 0)),
                  pl.BlockSpec((None, 1, KV_LORA), lambda b, i: (layer, 0, 0)),
                  tab, tab, tab],
        out_specs=[pl.BlockSpec((MLA_HEADS, tm, kd), lambda b, i: (0, row(b, i), 0)),
                   pl.BlockSpec((MLA_HEADS, tm, kd), lambda b, i: (0, row(b, i), 0)),
                   pl.BlockSpec((tm, MLA_WIDTH), lambda b, i: (row(b, i), 0))],
        out_shape=[jax.ShapeDtypeStruct((MLA_HEADS, r, kd), bf16),
                   jax.ShapeDtypeStruct((MLA_HEADS, r, kd), bf16),
                   jax.ShapeDtypeStruct((r, MLA_WIDTH), bf16)],
        compiler_params=_params(("parallel", "arbitrary"), 48),
        name="mla_prep",
    )(u, u, u, w_uq_p, w_ukv_p, q_norm_g, kv_norm_g, *tables)


def _attn_kernel(q_ref, k_ref, v_ref, o_ref, *, tk):
    tq = q_ref.shape[0]
    lp = k_ref.shape[0]
    nk = lp // tk
    c = (QK_NOPE + QK_ROPE) ** -0.5 * math.log2(math.e)
    q = q_ref[...]

    def step(j, carry, first):
        m, l, acc = carry
        base = j * tk if first else pl.multiple_of(j * tk, tk)
        s = _dot_nt(q, k_ref[pl.ds(base, tk), :])
        if first:
            s = jnp.where(lax.broadcasted_iota(jnp.int32, (1, tk), 1) >= FRONT, s, -jnp.inf)
        m_new = jnp.maximum(m, jnp.max(s, axis=1, keepdims=True))
        alpha = jnp.exp2((m - m_new) * c)
        p = jnp.exp2((s - m_new) * c)
        l = alpha * l + jnp.sum(p, axis=1, keepdims=True)
        acc = alpha * acc + _dot(p.astype(bf16), v_ref[pl.ds(base, tk), :])
        return m_new, l, acc

    init = (jnp.full((tq, 1), -jnp.inf, f32), jnp.zeros((tq, 1), f32), jnp.zeros((tq, V_HEAD), f32))
    carry = step(0, init, True)
    if nk > 1:
        carry = lax.fori_loop(1, nk, lambda j, cr: step(j, cr, False), carry)
    _, l, acc = carry
    o_ref[...] = (acc / l).astype(o_ref.dtype)


def _attention(qc, kc, v, bsz, lp):
    r = v.shape[0]
    tq = _pick(lp, (384, 128))
    tk = _pick(lp, (1408, 384, 128))
    nq = lp // tq
    kd = qc.shape[-1]
    return pl.pallas_call(
        functools.partial(_attn_kernel, tk=tk),
        grid=(bsz, MLA_HEADS, nq),
        in_specs=[pl.BlockSpec((None, tq, kd), lambda b, h, i: (h, b * nq + i, 0)),
                  pl.BlockSpec((None, lp, kd), lambda b, h, i: (h, b, 0)),
                  pl.BlockSpec((lp, V_HEAD), lambda b, h, i: (b, h))],
        out_specs=pl.BlockSpec((tq, V_HEAD), lambda b, h, i: (b * nq + i, h)),
        out_shape=jax.ShapeDtypeStruct((r, MLA_WIDTH), bf16),
        compiler_params=_params(("parallel", "parallel", "arbitrary"), 48),
        name="attention",
    )(qc, kc, v)


def _layer_norm(y, g, b):
    mu = jnp.mean(y, axis=-1, keepdims=True)
    var = jnp.mean(jnp.square(y - mu), axis=-1, keepdims=True)
    return (y - mu) * lax.rsqrt(var + EPS) * g + b


def _outproj_kernel(ml_ref, at_ref, w_ref, h_ref, g_ref, b_ref, o_ref, ob_ref, *, alpha):
    mix = _dot(ml_ref[...], w_ref[:ML_WIDTH, :]) + _dot(at_ref[...], w_ref[ML_WIDTH:, :])
    y = _layer_norm(alpha * h_ref[...] + mix, g_ref[...], b_ref[...])
    o_ref[...] = y
    ob_ref[...] = y.astype(bf16)


def _outproj(ml, at, w_out, h, ln_g, ln_b, layer, alpha):
    r, d = h.shape
    tm = _pick(r, (384, 128))
    vec = pl.BlockSpec((None, 1, d), lambda i: (layer, 0, 0))
    return pl.pallas_call(
        functools.partial(_outproj_kernel, alpha=alpha),
        grid=(r // tm,),
        in_specs=[pl.BlockSpec((tm, ML_WIDTH), lambda i: (i, 0)),
                  pl.BlockSpec((tm, MLA_WIDTH), lambda i: (i, 0)),
                  pl.BlockSpec((None,) + w_out.shape[1:], lambda i: (layer, 0, 0)),
                  pl.BlockSpec((tm, d), lambda i: (i, 0)), vec, vec],
        out_specs=[pl.BlockSpec((tm, d), lambda i: (i, 0)), pl.BlockSpec((tm, d), lambda i: (i, 0))],
        out_shape=[jax.ShapeDtypeStruct((r, d), f32), jax.ShapeDtypeStruct((r, d), bf16)],
        compiler_params=_params(("parallel",), 56),
        name="outproj_ln",
    )(ml, at, w_out, h, ln_g, ln_b)


HALO = 16


def _ffn_up_kernel(prev_ref, main_ref, next_ref, wg_ref, wv_ref, cw_ref, cb_ref, o_ref, ext_scr, *, lp):
    tm = main_ref.shape[0]

    @pl.when(pl.program_id(2) == 0)
    def _():
        ext_scr[:HALO, :] = prev_ref[...]
        ext_scr[HALO:HALO + tm, :] = main_ref[...]
        ext_scr[HALO + tm:, :] = next_ref[...]

    gate = _dot(ext_scr[...], wg_ref[...])
    pos = pl.program_id(1) * tm - HALO + lax.broadcasted_iota(jnp.int32, (tm + 2 * HALO, 1), 0)
    gate = jnp.where((pos >= FRONT) & (pos < lp), gate, 0.0)
    rows = tm + 2 * HALO
    left = pltpu.roll(gate, 1, 0)[HALO:HALO + tm]
    right = pltpu.roll(gate, rows - 1, 0)[HALO:HALO + tm]
    conv = left * cw_ref[0:1, :] + gate[HALO:HALO + tm] * cw_ref[1:2, :] + right * cw_ref[2:3, :] + cb_ref[...]
    val = _dot(main_ref[...], wv_ref[...])
    o_ref[...] = (conv * jax.nn.sigmoid(conv) * val).astype(o_ref.dtype)


def _ffn_up(hb, w_up, conv_w, conv_b, layer, bsz, lp):
    r, d = hb.shape
    dff = conv_b.shape[-1]
    tm = _pick(lp, (1408, 384, 128))
    tn = _pick(dff, (512, 256, 128))
    nt, nj = lp // tm, dff // tn
    hblk = tm // HALO
    last = r // HALO - 1
    return pl.pallas_call(
        functools.partial(_ffn_up_kernel, lp=lp),
        grid=(bsz, nt, nj),
        in_specs=[pl.BlockSpec((HALO, d), lambda b, i, j: (jnp.maximum((b * nt + i) * hblk - 1, 0), 0)),
                  pl.BlockSpec((tm, d), lambda b, i, j: (b * nt + i, 0)),
                  pl.BlockSpec((HALO, d), lambda b, i, j: (jnp.minimum((b * nt + i + 1) * hblk, last), 0)),
                  pl.BlockSpec((None, d, tn), lambda b, i, j: (layer, 0, j)),
                  pl.BlockSpec((None, d, tn), lambda b, i, j: (layer, 0, nj + j)),
                  pl.BlockSpec((None, 3, tn), lambda b, i, j: (layer, 0, j)),
                  pl.BlockSpec((None, 1, tn), lambda b, i, j: (layer, 0, j))],
        out_specs=pl.BlockSpec((tm, tn), lambda b, i, j: (b * nt + i, j)),
        out_shape=jax.ShapeDtypeStruct((r, dff), bf16),
        scratch_shapes=[pltpu.VMEM((tm + 2 * HALO, d), bf16)],
        compiler_params=_params(("parallel", "parallel", "arbitrary"), 56),
        name="ffn_up_conv",
    )(hb, hb, hb, w_up, w_up, conv_w, conv_b)


def _ffn_down_kernel(a_ref, w_ref, h_ref, g_ref, b_ref, o_ref, ob_ref, acc_scr, *, alpha):
    k = pl.program_id(1)

    @pl.when(k == 0)
    def _():
        acc_scr[...] = alpha * h_ref[...]

    acc_scr[...] += _dot(a_ref[...], w_ref[...])

    @pl.when(k == pl.num_programs(1) - 1)
    def _():
        y = _layer_norm(acc_scr[...], g_ref[...], b_ref[...])
        o_ref[...] = y
        ob_ref[...] = y.astype(bf16)


def _ffn_down(act, w_down, h, ln_g, ln_b, layer, alpha):
    r, d = h.shape
    dff = act.shape[1]
    tm = _pick(r, (768, 384, 128))
    tk = _pick(dff, (512, 256, 128))
    vec = pl.BlockSpec((None, 1, d), lambda i, k: (layer, 0, 0))
    return pl.pallas_call(
        functools.partial(_ffn_down_kernel, alpha=alpha),
        grid=(r // tm, dff // tk),
        in_specs=[pl.BlockSpec((tm, tk), lambda i, k: (i, k)),
                  pl.BlockSpec((None, tk, d), lambda i, k: (layer, k, 0)),
                  pl.BlockSpec((tm, d), lambda i, k: (i, 0)), vec, vec],
        out_specs=[pl.BlockSpec((tm, d), lambda i, k: (i, 0)), pl.BlockSpec((tm, d), lambda i, k: (i, 0))],
        out_shape=[jax.ShapeDtypeStruct((r, d), f32), jax.ShapeDtypeStruct((r, d), bf16)],
        scratch_shapes=[pltpu.VMEM((tm, d), f32)],
        compiler_params=_params(("parallel", "arbitrary"), 56),
        name="ffn_down_ln",
    )(act, w_down, h, ln_g, ln_b)


def _rope_tables(lp):
    inv_freq = ROPE_THETA ** (-jnp.arange(0, QK_ROPE, 2, dtype=f32) / QK_ROPE)
    pos = jnp.maximum(jnp.arange(lp, dtype=f32) - FRONT, 0.0)
    ang = pos[:, None] * inv_freq[None, :]
    cos, sin = jnp.cos(ang), jnp.sin(ang)
    zero = jnp.zeros_like(sin)
    reps = LANE // QK_ROPE
    cos_t = jnp.tile(jnp.concatenate([cos, cos], axis=1), (1, reps))
    sin_a = jnp.tile(jnp.concatenate([-sin, zero], axis=1), (1, reps))
    sin_b = jnp.tile(jnp.concatenate([zero, sin], axis=1), (1, reps))
    return cos_t, sin_a, sin_b


def kernel(x_prompt, x_sample, meta_tokens, w_in, b_gates, ml_norm_g, q_norm_g, kv_norm_g, w_uq, w_ukv,
           w_out, ln1_g, ln1_b, w_up, conv_w, conv_b, w_down, ln2_g, ln2_b):
    depth, d, _ = w_in.shape
    assert x_prompt.shape[1:] == x_sample.shape[1:]
    n_prompt = x_prompt.shape[0]
    x = jnp.concatenate([x_prompt, x_sample], axis=0)
    bsz, seq, _ = x.shape
    assert seq % LANE == 0
    lp = LANE + seq
    alpha = (2 * depth) ** 0.25

    gate_lo = 4 * ML_WIDTH
    w_main = jnp.concatenate(
        [w_in[..., :gate_lo], w_in[..., gate_lo + N_GATES:], w_in[..., -QK_ROPE:],
         jnp.zeros((depth, d, N_MAIN - COL_KR - 2 * QK_ROPE), w_in.dtype)], axis=-1).astype(bf16)
    h4 = ML_HEADS
    perm = jnp.array(list(range(0, h4)) + list(range(2 * h4, 3 * h4))
                     + list(range(h4, 2 * h4)) + list(range(3 * h4, 4 * h4)))
    wg_t = jnp.swapaxes(w_in[..., gate_lo:gate_lo + N_GATES], 1, 2)[:, perm, :].astype(bf16)
    bg = b_gates[:, perm, None].astype(f32)
    uq = w_uq.reshape(depth, Q_LORA, MLA_HEADS, QK_NOPE + QK_ROPE)
    w_uq_p = jnp.concatenate([uq[..., :QK_NOPE].reshape(depth, Q_LORA, -1),
                              uq[..., QK_NOPE:].reshape(depth, Q_LORA, -1)], axis=-1).astype(bf16)
    ukv = w_ukv.reshape(depth, KV_LORA, MLA_HEADS, QK_NOPE + V_HEAD)
    w_ukv_p = jnp.concatenate([ukv[..., :QK_NOPE].reshape(depth, KV_LORA, -1),
                               ukv[..., QK_NOPE:].reshape(depth, KV_LORA, -1)], axis=-1).astype(bf16)
    w_out_b, w_up_b, w_down_b = w_out.astype(bf16), w_up.astype(bf16), w_down.astype(bf16)
    vec3 = lambda a: a[:, None, :].astype(f32)
    ml_g, q_g, kv_g = vec3(ml_norm_g), vec3(q_norm_g), vec3(kv_norm_g)
    l1g, l1b, l2g, l2b, cb = vec3(ln1_g), vec3(ln1_b), vec3(ln2_g), vec3(ln2_b), vec3(conv_b)
    cw = conv_w.astype(f32)
    tables = _rope_tables(lp)

    meta = jnp.broadcast_to(meta_tokens.astype(x.dtype)[None], (bsz, N_META, d))
    h = jnp.concatenate([jnp.zeros((bsz, FRONT, d), x.dtype), meta, x], axis=1).reshape(bsz * lp, d)
    hb = h.astype(bf16)
    for layer in range(depth):
        u = _inproj(hb, w_main, layer)
        gq = _gates(hb, wg_t, bg, layer, bsz, lp)
        ml = _mlstm(u, gq, ml_g, layer, bsz, lp)
        qc, kc, v = _mla_prep(u, w_uq_p, w_ukv_p, q_g, kv_g, tables, layer, bsz, lp)
        at = _attention(qc, kc, v, bsz, lp)
        h, hb = _outproj(ml, at, w_out_b, h, l1g, l1b, layer, alpha)
        act = _ffn_up(hb, w_up_b, cw, cb, layer, bsz, lp)
        h, hb = _ffn_down(act, w_down_b, h, l2g, l2b, layer, alpha)
    y = h.reshape(bsz, lp, d)[:, LANE:]
    return (y[:n_prompt], y[n_prompt:])
```

```python
import functools
import math

import jax
import jax.numpy as jnp
from jax import lax
from jax.experimental import pallas as pl
from jax.experimental.pallas import tpu as pltpu

N_META = 16
ML_HEADS = 4
ML_HEAD_DIM = 256
ML_WIDTH = ML_HEADS * ML_HEAD_DIM
N_GATES = 4 * ML_HEADS
MLA_HEADS = 8
Q_LORA = 512
KV_LORA = 256
QK_NOPE = 128
QK_ROPE = 64
V_HEAD = 128
MLA_WIDTH = MLA_HEADS * V_HEAD
ROPE_THETA = 10000.0
EPS = 1e-5
NEG = -1e30

LANE = 128
FRONT = LANE - N_META
CHUNK = 128
ROPE_HALF = QK_ROPE // 2
MIB = 1024 * 1024

COL_Q, COL_K, COL_V, COL_O = 0, ML_WIDTH, 2 * ML_WIDTH, 3 * ML_WIDTH
COL_DQ = 4 * ML_WIDTH
COL_DKV = COL_DQ + Q_LORA
COL_KR = COL_DKV + KV_LORA
N_MAIN = COL_KR + 2 * QK_ROPE + LANE

f32 = jnp.float32
bf16 = jnp.bfloat16


def _pick(n, cands):
    for c in cands:
        if n % c == 0:
            return c
    raise ValueError(f"no tile for {n} in {cands}")


def _params(sem, vmem_mib):
    return pltpu.CompilerParams(dimension_semantics=sem, vmem_limit_bytes=vmem_mib * MIB)


def _dot(a, b):
    return jnp.dot(a, b, preferred_element_type=f32)


def _dot_nt(a, b):
    return lax.dot_general(a, b, (((1,), (1,)), ((), ())), preferred_element_type=f32)


def _inproj_kernel(a_ref, w_ref, o_ref):
    o_ref[...] = _dot(a_ref[...], w_ref[...]).astype(o_ref.dtype)


def _inproj(hb, w_main, layer):
    r, d = hb.shape
    n = w_main.shape[-1]
    tm = _pick(r, (768, 384, 128))
    tn = _pick(n, (1280, 1024, 512, 256, 128))
    return pl.pallas_call(
        _inproj_kernel,
        grid=(r // tm, n // tn),
        in_specs=[pl.BlockSpec((tm, d), lambda i, j: (i, 0)),
                  pl.BlockSpec((None, d, tn), lambda i, j: (layer, 0, j))],
        out_specs=pl.BlockSpec((tm, tn), lambda i, j: (i, j)),
        out_shape=jax.ShapeDtypeStruct((r, n), bf16),
        compiler_params=_params(("parallel", "arbitrary"), 48),
        name="inproj",
    )(hb, w_main)


def _gates_kernel(h_ref, wg_ref, bg_ref, o_ref):
    tm = h_ref.shape[0]
    half = N_GATES // 2
    g = _dot_nt(wg_ref[...], h_ref[...]) + bg_ref[...]
    pos = pl.program_id(1) * tm + lax.broadcasted_iota(jnp.int32, (half, tm), 1)
    unused = pos < FRONT
    gi, gf = g[:half], g[half:]
    li = jnp.where(unused, NEG, gi)
    lf = jnp.where(unused, 0.0, jnp.minimum(gf, 0.0) - jnp.log1p(jnp.exp(-jnp.abs(gf))))
    lane = lax.broadcasted_iota(jnp.int32, (half, CHUNK), 1)
    is_fwd = lax.broadcasted_iota(jnp.int32, (half, CHUNK), 0) < ML_HEADS
    for c in range(tm // CHUNK):
        sl = slice(c * CHUNK, (c + 1) * CHUNK)
        pre = suf = lf[:, sl]
        k = 1
        while k < CHUNK:
            pre = pre + jnp.where(lane >= k, pltpu.roll(pre, k, 1), 0.0)
            suf = suf + jnp.where(lane < CHUNK - k, pltpu.roll(suf, CHUNK - k, 1), 0.0)
            k *= 2
        b = jnp.where(is_fwd, pre, suf)
        o_ref[:half, sl] = li[:, sl] - b
        o_ref[half:, sl] = b


def _gates(hb, wg_t, bg, layer, bsz, lp):
    r, d = hb.shape
    tm = _pick(lp, (1408, 384, 128))
    nt = lp // tm
    return pl.pallas_call(
        _gates_kernel,
        grid=(bsz, nt),
        in_specs=[pl.BlockSpec((tm, d), lambda b, i: (b * nt + i, 0)),
                  pl.BlockSpec((None, N_GATES, d), lambda b, i: (layer, 0, 0)),
                  pl.BlockSpec((None, N_GATES, 1), lambda b, i: (layer, 0, 0))],
        out_specs=pl.BlockSpec((None, N_GATES, tm), lambda b, i: (b, 0, i)),
        out_shape=jax.ShapeDtypeStruct((bsz, N_GATES, lp), f32),
        compiler_params=_params(("parallel", "arbitrary"), 32),
        name="gates",
    )(hb, wg_t, bg)


def _mlstm_kernel(q_ref, k_ref, v_ref, o_ref, gq_ref, ng_ref, out_ref,
                  hf_scr, hb_scr, c_scr, n_scr, m_scr):
    head = pl.program_id(1)
    lp, dh = q_ref.shape
    t = CHUNK
    nc = lp // t
    c_scr[...] = jnp.zeros_like(c_scr)
    n_scr[...] = jnp.zeros_like(n_scr)
    m_scr[...] = jnp.zeros_like(m_scr)
    ti = lax.broadcasted_iota(jnp.int32, (t, t), 0)
    si = lax.broadcasted_iota(jnp.int32, (t, t), 1)
    masks = (si <= ti, si >= ti)
    eye = si == ti
    half = N_GATES // 2
    gate_row = lax.broadcasted_iota(jnp.int32, (half, t), 0)

    def chunk(c, dirn, h_scr):
        base = pl.multiple_of(c * t, t)
        q = q_ref[pl.ds(base, t), :]
        k = k_ref[pl.ds(base, t), :] * (ML_HEAD_DIM ** -0.5)
        v = v_ref[pl.ds(base, t), :]
        mine = gate_row == head + ML_HEADS * dirn
        r_row = jnp.sum(jnp.where(mine, gq_ref[:half, pl.ds(base, t)], 0.0), axis=0, keepdims=True)
        b_row = jnp.sum(jnp.where(mine, gq_ref[half:, pl.ds(base, t)], 0.0), axis=0, keepdims=True)
        m_state = m_scr[dirn, 0:1, 0:1]
        mask = masks[dirn]
        m_col = jnp.maximum(jnp.max(jnp.where(mask, r_row, -jnp.inf), axis=1, keepdims=True), m_state)
        b_col = jnp.sum(jnp.where(eye, b_row, 0.0), axis=1, keepdims=True)
        m_all = jnp.maximum(jnp.max(r_row, axis=1, keepdims=True), m_state)
        b_last = b_row[:, 0:1] if dirn else b_row[:, t - 1:t]
        e = jnp.exp(jnp.where(mask, r_row - m_col, -jnp.inf))
        s = _dot_nt(q, k) * e
        w_inter = jnp.exp(m_state - m_col)
        inter = _dot(q, c_scr[dirn].astype(bf16))
        intra = _dot(s.astype(bf16), v)
        n_row = n_scr[dirn, 0:1, :]
        qn = jnp.sum(q.astype(f32) * n_row, axis=1, keepdims=True)
        den = w_inter * qn + jnp.sum(s, axis=1, keepdims=True)
        num = w_inter * inter + intra
        h_scr[pl.ds(base, t), :] = num / jnp.maximum(jnp.abs(den), jnp.exp(-(b_col + m_col)))
        a = jnp.exp(m_state - m_all)
        w_row = jnp.exp(r_row - m_all)
        ktw = (k.astype(f32).T * w_row).astype(bf16)
        c_scr[dirn] = a * c_scr[dirn] + _dot(ktw, v)
        w8 = jnp.broadcast_to(w_row, (8, t)).astype(bf16)
        n_scr[dirn] = a * n_scr[dirn] + _dot(w8, k)
        m_scr[dirn] = jnp.broadcast_to(b_last + m_all, m_scr.shape[1:])

    def body(c, carry):
        chunk(c, 0, hf_scr)
        chunk(nc - 1 - c, 1, hb_scr)
        return carry

    lax.fori_loop(0, nc, body, 0)

    def finish(c, carry):
        base = pl.multiple_of(c * t, t)
        hs = hf_scr[pl.ds(base, t), :] + hb_scr[pl.ds(base, t), :]
        mu = jnp.mean(hs, axis=1, keepdims=True)
        var = jnp.mean(jnp.square(hs - mu), axis=1, keepdims=True)
        hn = (hs - mu) * lax.rsqrt(var + EPS) * ng_ref[...]
        gate = jax.nn.sigmoid(o_ref[pl.ds(base, t), :].astype(f32))
        out_ref[pl.ds(base, t), :] = (gate * hn).astype(out_ref.dtype)
        return carry

    lax.fori_loop(0, nc, finish, 0)


def _mlstm(u, gq, ml_norm_g, layer, bsz, lp):
    r = u.shape[0]
    dh = ML_HEAD_DIM

    def col(first):
        return pl.BlockSpec((lp, dh), lambda b, h: (b, first // dh + h))

    return pl.pallas_call(
        _mlstm_kernel,
        grid=(bsz, ML_HEADS),
        in_specs=[col(COL_Q), col(COL_K), col(COL_V), col(COL_O),
                  pl.BlockSpec((None, N_GATES, lp), lambda b, h: (b, 0, 0)),
                  pl.BlockSpec((None, 1, dh), lambda b, h: (layer, 0, h))],
        out_specs=pl.BlockSpec((lp, dh), lambda b, h: (b, h)),
        out_shape=jax.ShapeDtypeStruct((r, ML_WIDTH), bf16),
        scratch_shapes=[pltpu.VMEM((lp, dh), f32), pltpu.VMEM((lp, dh), f32),
                        pltpu.VMEM((2, dh, dh), f32), pltpu.VMEM((2, 8, dh), f32),
                        pltpu.VMEM((2, 8, LANE), f32)],
        compiler_params=_params(("parallel", "arbitrary"), 48),
        name="mlstm",
    )(u, u, u, u, gq, ml_norm_g)


def _rms(x, g):
    return x * lax.rsqrt(jnp.mean(jnp.square(x), axis=-1, keepdims=True) + EPS) * g


def _rope(x, cos, sin_a, sin_b):
    n = x.shape[1]
    reps = n // LANE
    if reps > 1:
        cos, sin_a, sin_b = (jnp.concatenate([a] * reps, axis=1) for a in (cos, sin_a, sin_b))
    return x * cos + pltpu.roll(x, n - ROPE_HALF, 1) * sin_a + pltpu.roll(x, ROPE_HALF, 1) * sin_b


def _mla_prep_kernel(dq_ref, dkv_ref, kr_ref, wuq_ref, wukv_ref, qg_ref, kvg_ref,
                     cos_ref, sa_ref, sb_ref, qc_ref, kc_ref, v_ref):
    cos, sa, sb = cos_ref[...], sa_ref[...], sb_ref[...]
    q = _dot(_rms(dq_ref[...].astype(f32), qg_ref[...]).astype(bf16), wuq_ref[...])
    kv = _dot(_rms(dkv_ref[...].astype(f32), kvg_ref[...]).astype(bf16), wukv_ref[...])
    q_rope = _rope(q[:, MLA_HEADS * QK_NOPE:], cos, sa, sb)
    k_rope = _rope(kr_ref[...].astype(f32), cos, sa, sb)
    tm = q.shape[0]
    lane = lax.broadcasted_iota(jnp.int32, (tm, LANE), 1)
    pos = pl.program_id(1) * tm + lax.broadcasted_iota(jnp.int32, (tm, LANE), 0)
    low = lane < QK_ROPE
    marker = lane == QK_ROPE
    one_hot = jnp.where(marker, 1.0, 0.0)
    k_tail = jnp.where(low, k_rope, jnp.where(marker & (pos < FRONT), -jnp.inf, 0.0)).astype(bf16)
    ones_col = jnp.where(lane == 0, 1.0, 0.0).astype(bf16)
    for h in range(MLA_HEADS):
        pair = q_rope[:, (h // 2) * LANE:(h // 2 + 1) * LANE]
        if h % 2:
            pair = pltpu.roll(pair, QK_ROPE, 1)
        qc_ref[h, :, :QK_NOPE] = q[:, h * QK_NOPE:(h + 1) * QK_NOPE].astype(bf16)
        qc_ref[h, :, QK_NOPE:] = jnp.where(low, pair, one_hot).astype(bf16)
        kc_ref[h, :, :QK_NOPE] = kv[:, h * QK_NOPE:(h + 1) * QK_NOPE].astype(bf16)
        kc_ref[h, :, QK_NOPE:] = k_tail
        v_ref[:, 2 * h * V_HEAD:(2 * h + 1) * V_HEAD] = kv[:, (MLA_HEADS + h) * V_HEAD:
                                                           (MLA_HEADS + h + 1) * V_HEAD].astype(bf16)
        v_ref[:, (2 * h + 1) * V_HEAD:(2 * h + 2) * V_HEAD] = ones_col


def _mla_prep(u, w_uq_p, w_ukv_p, q_norm_g, kv_norm_g, tables, layer, bsz, lp):
    r = u.shape[0]
    tm = _pick(lp, (384, 128))
    nt = lp // tm
    kd = QK_NOPE + LANE
    tab = pl.BlockSpec((tm, LANE), lambda b, i: (i, 0))
    row = lambda b, i: b * nt + i
    return pl.pallas_call(
        _mla_prep_kernel,
        grid=(bsz, nt),
        in_specs=[pl.BlockSpec((tm, Q_LORA), lambda b, i: (row(b, i), COL_DQ // Q_LORA)),
                  pl.BlockSpec((tm, KV_LORA), lambda b, i: (row(b, i), COL_DKV // KV_LORA)),
                  pl.BlockSpec((tm, LANE), lambda b, i: (row(b, i), COL_KR // LANE)),
                  pl.BlockSpec((None,) + w_uq_p.shape[1:], lambda b, i: (layer, 0, 0)),
                  pl.BlockSpec((None,) + w_ukv_p.shape[1:], lambda b, i: (layer, 0, 0)),
                  pl.BlockSpec((None, 1, Q_LORA), lambda b, i: (layer, 0, 0)),
                  pl.BlockSpec((None, 1, KV_LORA), lambda b, i: (layer, 0, 0)),
                  tab, tab, tab],
        out_specs=[pl.BlockSpec((MLA_HEADS, tm, kd), lambda b, i: (0, row(b, i), 0)),
                   pl.BlockSpec((MLA_HEADS, tm, kd), lambda b, i: (0, row(b, i), 0)),
                   pl.BlockSpec((tm, 2 * MLA_WIDTH), lambda b, i: (row(b, i), 0))],
        out_shape=[jax.ShapeDtypeStruct((MLA_HEADS, r, kd), bf16),
                   jax.ShapeDtypeStruct((MLA_HEADS, r, kd), bf16),
                   jax.ShapeDtypeStruct((r, 2 * MLA_WIDTH), bf16)],
        compiler_params=_params(("parallel", "arbitrary"), 48),
        name="mla_prep",
    )(u, u, u, w_uq_p, w_ukv_p, q_norm_g, kv_norm_g, *tables)


def _attn_kernel(q_ref, k_ref, v_ref, o_ref, s0_ref, s1_ref, p0_ref, p1_ref, *, tq, tk):
    lp = k_ref.shape[0]
    nq, nk = lp // tq, lp // tk
    n_tiles = nq * nk
    odd = n_tiles % 2
    c = (QK_NOPE + QK_ROPE) ** -0.5 * math.log2(math.e)
    s_refs, p_refs = (s0_ref, s1_ref), (p0_ref, p1_ref)
    for ref in s_refs + p_refs:
        ref[...] = jnp.zeros_like(ref)

    def tile(x):
        x = jnp.clip(x, 0, n_tiles - 1)
        i = x // nk
        return pl.multiple_of(i * tq, tq), pl.multiple_of((x - i * nk) * tk, tk)

    def scores(x, slot):
        q0, k0 = tile(x)
        s_refs[slot][...] = _dot_nt(q_ref[pl.ds(q0, tq), :], k_ref[pl.ds(k0, tk), :])

    def softmax(x, slot, m):
        _, k0 = tile(x)
        m = jnp.where(k0 == 0, -jnp.inf, m)
        s = s_refs[slot][...]
        m_new = jnp.maximum(m, jnp.max(s, axis=1, keepdims=True))
        p_refs[slot][...] = jnp.exp2((s - m_new) * c).astype(bf16)
        return m_new, jnp.exp2((m - m_new) * c)

    def weighted(x, slot, alpha, acc):
        q0, k0 = tile(x)
        acc = alpha * acc + _dot(p_refs[slot][...], v_ref[pl.ds(k0, tk), :])
        out = acc[:, :V_HEAD] * (1.0 / acc[:, V_HEAD:V_HEAD + 1])
        o_ref[pl.ds(q0, tq), :] = out.astype(o_ref.dtype)
        return acc

    def body(u, carry):
        m, alpha, acc = carry
        x = 2 * u + odd
        a, b = odd, 1 - odd
        acc = weighted(x - 2, a, alpha, acc)
        m, alpha = softmax(x - 1, b, m)
        scores(x, a)
        acc = weighted(x - 1, b, alpha, acc)
        m, alpha = softmax(x, a, m)
        scores(x + 1, b)
        return m, alpha, acc

    if odd:
        scores(0, 0)
    init = (jnp.zeros((tq, 1), f32), jnp.zeros((tq, 1), f32), jnp.zeros((tq, 2 * V_HEAD), f32))
    lax.fori_loop(0, (n_tiles - odd) // 2 + 1, body, init)


def _attention(qc, kc, v, bsz, lp):
    r = v.shape[0]
    tq = _pick(lp, (384, 128))
    tk = _pick(lp, (1408, 384, 128))
    kd = qc.shape[-1]
    return pl.pallas_call(
        functools.partial(_attn_kernel, tq=tq, tk=tk),
        grid=(bsz, MLA_HEADS),
        in_specs=[pl.BlockSpec((None, lp, kd), lambda b, h: (h, b, 0)),
                  pl.BlockSpec((None, lp, kd), lambda b, h: (h, b, 0)),
                  pl.BlockSpec((lp, 2 * V_HEAD), lambda b, h: (b, h))],
        out_specs=pl.BlockSpec((lp, V_HEAD), lambda b, h: (b, h)),
        out_shape=jax.ShapeDtypeStruct((r, MLA_WIDTH), bf16),
        scratch_shapes=[pltpu.VMEM((tq, tk), f32), pltpu.VMEM((tq, tk), f32),
                        pltpu.VMEM((tq, tk), bf16), pltpu.VMEM((tq, tk), bf16)],
        compiler_params=_params(("parallel", "arbitrary"), 48),
        name="attention",
    )(qc, kc, v)


def _layer_norm(y, g, b):
    mu = jnp.mean(y, axis=-1, keepdims=True)
    var = jnp.mean(jnp.square(y - mu), axis=-1, keepdims=True)
    return (y - mu) * lax.rsqrt(var + EPS) * g + b


def _outproj_kernel(ml_ref, at_ref, w_ref, h_ref, g_ref, b_ref, o_ref, ob_ref, *, alpha):
    mix = _dot(ml_ref[...], w_ref[:ML_WIDTH, :]) + _dot(at_ref[...], w_ref[ML_WIDTH:, :])
    y = _layer_norm(alpha * h_ref[...] + mix, g_ref[...], b_ref[...])
    o_ref[...] = y
    ob_ref[...] = y.astype(bf16)


def _outproj(ml, at, w_out, h, ln_g, ln_b, layer, alpha):
    r, d = h.shape
    tm = _pick(r, (384, 128))
    vec = pl.BlockSpec((None, 1, d), lambda i: (layer, 0, 0))
    return pl.pallas_call(
        functools.partial(_outproj_kernel, alpha=alpha),
        grid=(r // tm,),
        in_specs=[pl.BlockSpec((tm, ML_WIDTH), lambda i: (i, 0)),
                  pl.BlockSpec((tm, MLA_WIDTH), lambda i: (i, 0)),
                  pl.BlockSpec((None,) + w_out.shape[1:], lambda i: (layer, 0, 0)),
                  pl.BlockSpec((tm, d), lambda i: (i, 0)), vec, vec],
        out_specs=[pl.BlockSpec((tm, d), lambda i: (i, 0)), pl.BlockSpec((tm, d), lambda i: (i, 0))],
        out_shape=[jax.ShapeDtypeStruct((r, d), f32), jax.ShapeDtypeStruct((r, d), bf16)],
        compiler_params=_params(("parallel",), 56),
        name="outproj_ln",
    )(ml, at, w_out, h, ln_g, ln_b)


HALO = 16


def _ffn_up_kernel(prev_ref, main_ref, next_ref, wg_ref, wv_ref, cw_ref, cb_ref, o_ref, ext_scr, *, lp):
    tm = main_ref.shape[0]

    @pl.when(pl.program_id(2) == 0)
    def _():
        ext_scr[:HALO, :] = prev_ref[...]
        ext_scr[HALO:HALO + tm, :] = main_ref[...]
        ext_scr[HALO + tm:, :] = next_ref[...]

    gate = _dot(ext_scr[...], wg_ref[...])
    pos = pl.program_id(1) * tm - HALO + lax.broadcasted_iota(jnp.int32, (tm + 2 * HALO, 1), 0)
    gate = jnp.where((pos >= FRONT) & (pos < lp), gate, 0.0)
    rows = tm + 2 * HALO
    left = pltpu.roll(gate, 1, 0)[HALO:HALO + tm]
    right = pltpu.roll(gate, rows - 1, 0)[HALO:HALO + tm]
    conv = left * cw_ref[0:1, :] + gate[HALO:HALO + tm] * cw_ref[1:2, :] + right * cw_ref[2:3, :] + cb_ref[...]
    val = _dot(main_ref[...], wv_ref[...])
    o_ref[...] = (conv * jax.nn.sigmoid(conv) * val).astype(o_ref.dtype)


def _ffn_up(hb, w_up, conv_w, conv_b, layer, bsz, lp):
    r, d = hb.shape
    dff = conv_b.shape[-1]
    tm = _pick(lp, (1408, 384, 128))
    tn = _pick(dff, (512, 256, 128))
    nt, nj = lp // tm, dff // tn
    hblk = tm // HALO
    last = r // HALO - 1
    return pl.pallas_call(
        functools.partial(_ffn_up_kernel, lp=lp),
        grid=(bsz, nt, nj),
        in_specs=[pl.BlockSpec((HALO, d), lambda b, i, j: (jnp.maximum((b * nt + i) * hblk - 1, 0), 0)),
                  pl.BlockSpec((tm, d), lambda b, i, j: (b * nt + i, 0)),
                  pl.BlockSpec((HALO, d), lambda b, i, j: (jnp.minimum((b * nt + i + 1) * hblk, last), 0)),
                  pl.BlockSpec((None, d, tn), lambda b, i, j: (layer, 0, j)),
                  pl.BlockSpec((None, d, tn), lambda b, i, j: (layer, 0, nj + j)),
                  pl.BlockSpec((None, 3, tn), lambda b, i, j: (layer, 0, j)),
                  pl.BlockSpec((None, 1, tn), lambda b, i, j: (layer, 0, j))],
        out_specs=pl.BlockSpec((tm, tn), lambda b, i, j: (b * nt + i, j)),
        out_shape=jax.ShapeDtypeStruct((r, dff), bf16),
        scratch_shapes=[pltpu.VMEM((tm + 2 * HALO, d), bf16)],
        compiler_params=_params(("parallel", "parallel", "arbitrary"), 56),
        name="ffn_up_conv",
    )(hb, hb, hb, w_up, w_up, conv_w, conv_b)


def _ffn_down_kernel(a_ref, w_ref, h_ref, g_ref, b_ref, o_ref, ob_ref, acc_scr, *, alpha):
    k = pl.program_id(1)

    @pl.when(k == 0)
    def _():
        acc_scr[...] = alpha * h_ref[...]

    acc_scr[...] += _dot(a_ref[...], w_ref[...])

    @pl.when(k == pl.num_programs(1) - 1)
    def _():
        y = _layer_norm(acc_scr[...], g_ref[...], b_ref[...])
        o_ref[...] = y
        ob_ref[...] = y.astype(bf16)


def _ffn_down(act, w_down, h, ln_g, ln_b, layer, alpha):
    r, d = h.shape
    dff = act.shape[1]
    tm = _pick(r, (768, 384, 128))
    tk = _pick(dff, (512, 256, 128))
    vec = pl.BlockSpec((None, 1, d), lambda i, k: (layer, 0, 0))
    return pl.pallas_call(
        functools.partial(_ffn_down_kernel, alpha=alpha),
        grid=(r // tm, dff // tk),
        in_specs=[pl.BlockSpec((tm, tk), lambda i, k: (i, k)),
                  pl.BlockSpec((None, tk, d), lambda i, k: (layer, k, 0)),
                  pl.BlockSpec((tm, d), lambda i, k: (i, 0)), vec, vec],
        out_specs=[pl.BlockSpec((tm, d), lambda i, k: (i, 0)), pl.BlockSpec((tm, d), lambda i, k: (i, 0))],
        out_shape=[jax.ShapeDtypeStruct((r, d), f32), jax.ShapeDtypeStruct((r, d), bf16)],
        scratch_shapes=[pltpu.VMEM((tm, d), f32)],
        compiler_params=_params(("parallel", "arbitrary"), 56),
        name="ffn_down_ln",
    )(act, w_down, h, ln_g, ln_b)


def _rope_tables(lp):
    inv_freq = ROPE_THETA ** (-jnp.arange(0, QK_ROPE, 2, dtype=f32) / QK_ROPE)
    pos = jnp.maximum(jnp.arange(lp, dtype=f32) - FRONT, 0.0)
    ang = pos[:, None] * inv_freq[None, :]
    cos, sin = jnp.cos(ang), jnp.sin(ang)
    zero = jnp.zeros_like(sin)
    reps = LANE // QK_ROPE
    cos_t = jnp.tile(jnp.concatenate([cos, cos], axis=1), (1, reps))
    sin_a = jnp.tile(jnp.concatenate([-sin, zero], axis=1), (1, reps))
    sin_b = jnp.tile(jnp.concatenate([zero, sin], axis=1), (1, reps))
    return cos_t, sin_a, sin_b


def kernel(x_prompt, x_sample, meta_tokens, w_in, b_gates, ml_norm_g, q_norm_g, kv_norm_g, w_uq, w_ukv,
           w_out, ln1_g, ln1_b, w_up, conv_w, conv_b, w_down, ln2_g, ln2_b):
    depth, d, _ = w_in.shape
    assert x_prompt.shape[1:] == x_sample.shape[1:]
    n_prompt = x_prompt.shape[0]
    x = jnp.concatenate([x_prompt, x_sample], axis=0)
    bsz, seq, _ = x.shape
    assert seq % LANE == 0
    lp = LANE + seq
    alpha = (2 * depth) ** 0.25

    gate_lo = 4 * ML_WIDTH
    w_main = jnp.concatenate(
        [w_in[..., :gate_lo], w_in[..., gate_lo + N_GATES:], w_in[..., -QK_ROPE:],
         jnp.zeros((depth, d, N_MAIN - COL_KR - 2 * QK_ROPE), w_in.dtype)], axis=-1).astype(bf16)
    h4 = ML_HEADS
    perm = jnp.array(list(range(0, h4)) + list(range(2 * h4, 3 * h4))
                     + list(range(h4, 2 * h4)) + list(range(3 * h4, 4 * h4)))
    wg_t = jnp.swapaxes(w_in[..., gate_lo:gate_lo + N_GATES], 1, 2)[:, perm, :].astype(bf16)
    bg = b_gates[:, perm, None].astype(f32)
    uq = w_uq.reshape(depth, Q_LORA, MLA_HEADS, QK_NOPE + QK_ROPE)
    w_uq_p = jnp.concatenate([uq[..., :QK_NOPE].reshape(depth, Q_LORA, -1),
                              uq[..., QK_NOPE:].reshape(depth, Q_LORA, -1)], axis=-1).astype(bf16)
    ukv = w_ukv.reshape(depth, KV_LORA, MLA_HEADS, QK_NOPE + V_HEAD)
    w_ukv_p = jnp.concatenate([ukv[..., :QK_NOPE].reshape(depth, KV_LORA, -1),
                               ukv[..., QK_NOPE:].reshape(depth, KV_LORA, -1)], axis=-1).astype(bf16)
    w_out_b, w_up_b, w_down_b = w_out.astype(bf16), w_up.astype(bf16), w_down.astype(bf16)
    vec3 = lambda a: a[:, None, :].astype(f32)
    ml_g, q_g, kv_g = vec3(ml_norm_g), vec3(q_norm_g), vec3(kv_norm_g)
    l1g, l1b, l2g, l2b, cb = vec3(ln1_g), vec3(ln1_b), vec3(ln2_g), vec3(ln2_b), vec3(conv_b)
    cw = conv_w.astype(f32)
    tables = _rope_tables(lp)

    meta = jnp.broadcast_to(meta_tokens.astype(x.dtype)[None], (bsz, N_META, d))
    h = jnp.concatenate([jnp.zeros((bsz, FRONT, d), x.dtype), meta, x], axis=1).reshape(bsz * lp, d)
    hb = h.astype(bf16)
    for layer in range(depth):
        u = _inproj(hb, w_main, layer)
        gq = _gates(hb, wg_t, bg, layer, bsz, lp)
        ml = _mlstm(u, gq, ml_g, layer, bsz, lp)
        qc, kc, v = _mla_prep(u, w_uq_p, w_ukv_p, q_g, kv_g, tables, layer, bsz, lp)
        at = _attention(qc, kc, v, bsz, lp)
        h, hb = _outproj(ml, at, w_out_b, h, l1g, l1b, layer, alpha)
        act = _ffn_up(hb, w_up_b, cw, cb, layer, bsz, lp)
        h, hb = _ffn_down(act, w_down_b, h, l2g, l2b, layer, alpha)
    y = h.reshape(bsz, lp, d)[:, LANE:]
    return (y[:n_prompt], y[n_prompt:])
```

```python
import functools
import math

import jax
import jax.numpy as jnp
from jax import lax
from jax.experimental import pallas as pl
from jax.experimental.pallas import tpu as pltpu

N_META = 16
ML_HEADS = 4
ML_HEAD_DIM = 256
ML_WIDTH = ML_HEADS * ML_HEAD_DIM
N_GATES = 4 * ML_HEADS
MLA_HEADS = 8
Q_LORA = 512
KV_LORA = 256
QK_NOPE = 128
QK_ROPE = 64
V_HEAD = 128
MLA_WIDTH = MLA_HEADS * V_HEAD
ROPE_THETA = 10000.0
EPS = 1e-5
NEG = -1e30

LANE = 128
FRONT = LANE - N_META
CHUNK = 128
ROPE_HALF = QK_ROPE // 2
SOFTMAX_SCALE_LOG2 = (QK_NOPE + QK_ROPE) ** -0.5 * math.log2(math.e)
MIB = 1024 * 1024

COL_Q, COL_V, COL_O = 0, ML_WIDTH, 2 * ML_WIDTH
COL_DQ = 3 * ML_WIDTH
COL_DKV = COL_DQ + Q_LORA
COL_KR = COL_DKV + KV_LORA
N_MAIN = COL_KR + 2 * QK_ROPE + LANE

f32 = jnp.float32
bf16 = jnp.bfloat16


def _pick(n, cands):
    for c in cands:
        if n % c == 0:
            return c
    raise ValueError(f"no tile for {n} in {cands}")


def _params(sem, vmem_mib):
    return pltpu.CompilerParams(dimension_semantics=sem, vmem_limit_bytes=vmem_mib * MIB)


def _dot(a, b):
    return jnp.dot(a, b, preferred_element_type=f32)


def _dot_nt(a, b):
    return lax.dot_general(a, b, (((1,), (1,)), ((), ())), preferred_element_type=f32)


def _inproj_kernel(a_ref, w_ref, o_ref):
    o_ref[...] = _dot(a_ref[...], w_ref[...]).astype(o_ref.dtype)


def _keys_t_kernel(w_ref, h_ref, o_ref):
    o_ref[...] = (_dot_nt(w_ref[...], h_ref[...]) * ML_HEAD_DIM ** -0.5).astype(o_ref.dtype)


def _keys_t(hb, wk_t, layer):
    r, d = hb.shape
    tm = _pick(r, (1408, 384, 128))
    return pl.pallas_call(
        _keys_t_kernel,
        grid=(r // tm,),
        in_specs=[pl.BlockSpec((None, ML_WIDTH, d), lambda i: (layer, 0, 0)),
                  pl.BlockSpec((tm, d), lambda i: (i, 0))],
        out_specs=pl.BlockSpec((ML_WIDTH, tm), lambda i: (0, i)),
        out_shape=jax.ShapeDtypeStruct((ML_WIDTH, r), bf16),
        compiler_params=_params(("parallel",), 48),
        name="keys_t",
    )(wk_t, hb)


def _inproj(hb, w_main, layer):
    r, d = hb.shape
    n = w_main.shape[-1]
    tm = _pick(r, (1408, 384, 128))
    tn = _pick(n, (1024, 512, 256, 128))
    return pl.pallas_call(
        _inproj_kernel,
        grid=(r // tm, n // tn),
        in_specs=[pl.BlockSpec((tm, d), lambda i, j: (i, 0)),
                  pl.BlockSpec((None, d, tn), lambda i, j: (layer, 0, j))],
        out_specs=pl.BlockSpec((tm, tn), lambda i, j: (i, j)),
        out_shape=jax.ShapeDtypeStruct((r, n), bf16),
        compiler_params=_params(("parallel", "arbitrary"), 48),
        name="inproj",
    )(hb, w_main)


def _gates_kernel(h_ref, wg_ref, bg_ref, o_ref):
    tm = h_ref.shape[0]
    half = N_GATES // 2
    g = _dot_nt(wg_ref[...], h_ref[...]) + bg_ref[...]
    pos = pl.program_id(1) * tm + lax.broadcasted_iota(jnp.int32, (half, tm), 1)
    unused = pos < FRONT
    gi, gf = g[:half], g[half:]
    li = jnp.where(unused, NEG, gi)
    lf = jnp.where(unused, 0.0, jnp.minimum(gf, 0.0) - jnp.log1p(jnp.exp(-jnp.abs(gf))))
    lane = lax.broadcasted_iota(jnp.int32, (half, CHUNK), 1)
    is_fwd = lax.broadcasted_iota(jnp.int32, (half, CHUNK), 0) < ML_HEADS
    for c in range(tm // CHUNK):
        sl = slice(c * CHUNK, (c + 1) * CHUNK)
        pre = suf = lf[:, sl]
        k = 1
        while k < CHUNK:
            pre = pre + jnp.where(lane >= k, pltpu.roll(pre, k, 1), 0.0)
            suf = suf + jnp.where(lane < CHUNK - k, pltpu.roll(suf, CHUNK - k, 1), 0.0)
            k *= 2
        b = jnp.where(is_fwd, pre, suf)
        o_ref[:half, sl] = li[:, sl] - b
        o_ref[half:, sl] = b


def _gates(hb, wg_t, bg, layer, bsz, lp):
    r, d = hb.shape
    tm = _pick(lp, (1408, 384, 128))
    nt = lp // tm
    return pl.pallas_call(
        _gates_kernel,
        grid=(bsz, nt),
        in_specs=[pl.BlockSpec((tm, d), lambda b, i: (b * nt + i, 0)),
                  pl.BlockSpec((None, N_GATES, d), lambda b, i: (layer, 0, 0)),
                  pl.BlockSpec((None, N_GATES, 1), lambda b, i: (layer, 0, 0))],
        out_specs=pl.BlockSpec((None, N_GATES, tm), lambda b, i: (b, 0, i)),
        out_shape=jax.ShapeDtypeStruct((bsz, N_GATES, lp), f32),
        compiler_params=_params(("parallel", "arbitrary"), 32),
        name="gates",
    )(hb, wg_t, bg)


def _mlstm_kernel(q_ref, kt_ref, v_ref, o_ref, gq_ref, ng_ref, out_ref, hf_scr, hb_scr, c_scr, m_scr, s_scr):
    head = pl.program_id(1)
    lp, dh = q_ref.shape
    t = CHUNK
    nc = lp // t
    c_scr[...] = jnp.zeros_like(c_scr)
    m_scr[...] = jnp.zeros_like(m_scr)
    ti = lax.broadcasted_iota(jnp.int32, (t, t), 0)
    si = lax.broadcasted_iota(jnp.int32, (t, t), 1)
    masks = (si <= ti, si >= ti)
    eye = si == ti
    half = N_GATES // 2
    gate_row = lax.broadcasted_iota(jnp.int32, (half, t), 0)
    ones_col = jnp.where(lax.broadcasted_iota(jnp.int32, (t, LANE), 1) == 0, 1.0, 0.0).astype(bf16)

    def qk_ahead(c, dirn):
        base = pl.multiple_of(jnp.clip(c, 0, nc - 1) * t, t)
        s_scr[dirn] = _dot(q_ref[pl.ds(base, t), :], kt_ref[:, pl.ds(base, t)])

    def chunk(c, dirn, h_scr):
        base = pl.multiple_of(c * t, t)
        q = q_ref[pl.ds(base, t), :]
        kt = kt_ref[:, pl.ds(base, t)]
        v = jnp.concatenate([v_ref[pl.ds(base, t), :], ones_col], axis=1)
        mine = gate_row == head + ML_HEADS * dirn
        r_row = jnp.sum(jnp.where(mine, gq_ref[:half, pl.ds(base, t)], 0.0), axis=0, keepdims=True)
        b_row = jnp.sum(jnp.where(mine, gq_ref[half:, pl.ds(base, t)], 0.0), axis=0, keepdims=True)
        m_state = m_scr[dirn, 0:1, 0:1]
        mask = masks[dirn]
        m_col = jnp.maximum(jnp.max(jnp.where(mask, r_row, -jnp.inf), axis=1, keepdims=True), m_state)
        b_col = jnp.sum(jnp.where(eye, b_row, 0.0), axis=1, keepdims=True)
        m_all = jnp.maximum(jnp.max(r_row, axis=1, keepdims=True), m_state)
        b_last = b_row[:, 0:1] if dirn else b_row[:, t - 1:t]
        s = s_scr[dirn] * jnp.exp(jnp.where(mask, r_row - m_col, -jnp.inf))
        both = jnp.exp(m_state - m_col) * _dot(q, c_scr[dirn].astype(bf16)) + _dot(s.astype(bf16), v)
        floor = jnp.exp(-(b_col + m_col))
        h_scr[pl.ds(base, t), :] = both[:, :dh] / jnp.maximum(jnp.abs(both[:, dh:dh + 1]), floor)
        ktw = (kt.astype(f32) * jnp.exp(r_row - m_all)).astype(bf16)
        c_scr[dirn] = jnp.exp(m_state - m_all) * c_scr[dirn] + _dot(ktw, v)
        m_scr[dirn] = jnp.broadcast_to(b_last + m_all, m_scr.shape[1:])

    def body(c, carry):
        chunk(c, 0, hf_scr)
        chunk(nc - 1 - c, 1, hb_scr)
        qk_ahead(c + 1, 0)
        qk_ahead(nc - 2 - c, 1)
        return carry

    qk_ahead(0, 0)
    qk_ahead(nc - 1, 1)
    lax.fori_loop(0, nc, body, 0)

    def finish(c, carry):
        base = pl.multiple_of(c * t, t)
        hs = hf_scr[pl.ds(base, t), :] + hb_scr[pl.ds(base, t), :]
        mu = jnp.mean(hs, axis=1, keepdims=True)
        var = jnp.mean(jnp.square(hs - mu), axis=1, keepdims=True)
        hn = (hs - mu) * lax.rsqrt(var + EPS) * ng_ref[...]
        gate = jax.nn.sigmoid(o_ref[pl.ds(base, t), :].astype(f32))
        out_ref[pl.ds(base, t), :] = (gate * hn).astype(out_ref.dtype)
        return carry

    lax.fori_loop(0, nc, finish, 0)


def _mlstm(u, kt, gq, ml_norm_g, layer, bsz, lp):
    r = u.shape[0]
    dh = ML_HEAD_DIM

    def col(first):
        return pl.BlockSpec((lp, dh), lambda b, h: (b, first // dh + h))

    return pl.pallas_call(
        _mlstm_kernel,
        grid=(bsz, ML_HEADS),
        in_specs=[col(COL_Q), pl.BlockSpec((dh, lp), lambda b, h: (h, b)), col(COL_V), col(COL_O),
                  pl.BlockSpec((None, N_GATES, lp), lambda b, h: (b, 0, 0)),
                  pl.BlockSpec((None, 1, dh), lambda b, h: (layer, 0, h))],
        out_specs=pl.BlockSpec((lp, dh), lambda b, h: (b, h)),
        out_shape=jax.ShapeDtypeStruct((r, ML_WIDTH), bf16),
        scratch_shapes=[pltpu.VMEM((lp, dh), f32), pltpu.VMEM((lp, dh), f32),
                        pltpu.VMEM((2, dh, dh + LANE), f32), pltpu.VMEM((2, 8, LANE), f32),
                        pltpu.VMEM((2, CHUNK, CHUNK), f32)],
        compiler_params=_params(("parallel", "arbitrary"), 48),
        name="mlstm",
    )(u, kt, u, u, gq, ml_norm_g)


def _rms(x, g):
    return x * lax.rsqrt(jnp.mean(jnp.square(x), axis=-1, keepdims=True) + EPS) * g


def _rope(x, cos, sin_a, sin_b):
    n = x.shape[1]
    reps = n // LANE
    if reps > 1:
        cos, sin_a, sin_b = (jnp.concatenate([a] * reps, axis=1) for a in (cos, sin_a, sin_b))
    return x * cos + pltpu.roll(x, n - ROPE_HALF, 1) * sin_a + pltpu.roll(x, ROPE_HALF, 1) * sin_b


def _mla_prep_kernel(dq_ref, dkv_ref, kr_ref, wuq_ref, wukv_ref, qg_ref, kvg_ref,
                     cos_ref, sa_ref, sb_ref, qc_ref, kc_ref, v_ref):
    cos, sa, sb = cos_ref[...], sa_ref[...], sb_ref[...]
    q = _dot(_rms(dq_ref[...].astype(f32), qg_ref[...]).astype(bf16), wuq_ref[...]) * SOFTMAX_SCALE_LOG2
    kv =_dot(_rms(dkv_ref[...].astype(f32), kvg_ref[...]).astype(bf16), wukv_ref[...])
    q_rope = _rope(q[:, MLA_HEADS * QK_NOPE:], cos, sa, sb)
    k_rope = _rope(kr_ref[...].astype(f32), cos, sa, sb)
    tm = q.shape[0]
    lane = lax.broadcasted_iota(jnp.int32, (tm, LANE), 1)
    pos = pl.program_id(1) * tm + lax.broadcasted_iota(jnp.int32, (tm, LANE), 0)
    low = lane < QK_ROPE
    marker = lane == QK_ROPE
    one_hot = jnp.where(marker, 1.0, 0.0)
    k_tail = jnp.where(low, k_rope, jnp.where(marker & (pos < FRONT), -jnp.inf, 0.0)).astype(bf16)
    ones_col = jnp.where(lane == 0, 1.0, 0.0).astype(bf16)
    for h in range(MLA_HEADS):
        pair = q_rope[:, (h // 2) * LANE:(h // 2 + 1) * LANE]
        if h % 2:
            pair = pltpu.roll(pair, QK_ROPE, 1)
        qc_ref[h, :, :QK_NOPE] = q[:, h * QK_NOPE:(h + 1) * QK_NOPE].astype(bf16)
        qc_ref[h, :, QK_NOPE:] = jnp.where(low, pair, one_hot).astype(bf16)
        kc_ref[h, :, :QK_NOPE] = kv[:, h * QK_NOPE:(h + 1) * QK_NOPE].astype(bf16)
        kc_ref[h, :, QK_NOPE:] = k_tail
        v_ref[h, :, :V_HEAD] = kv[:, (MLA_HEADS + h) * V_HEAD:(MLA_HEADS + h + 1) * V_HEAD].astype(bf16)
        v_ref[h, :, V_HEAD:] = ones_col


def _mla_prep(u, w_uq_p, w_ukv_p, q_norm_g, kv_norm_g, tables, layer, bsz, lp):
    r = u.shape[0]
    tm = _pick(lp, (384, 128))
    nt = lp // tm
    kd = QK_NOPE + LANE
    tab = pl.BlockSpec((tm, LANE), lambda b, i: (i, 0))
    row = lambda b, i: b * nt + i
    return pl.pallas_call(
        _mla_prep_kernel,
        grid=(bsz, nt),
        in_specs=[pl.BlockSpec((tm, Q_LORA), lambda b, i: (row(b, i), COL_DQ // Q_LORA)),
                  pl.BlockSpec((tm, KV_LORA), lambda b, i: (row(b, i), COL_DKV // KV_LORA)),
                  pl.BlockSpec((tm, LANE), lambda b, i: (row(b, i), COL_KR // LANE)),
                  pl.BlockSpec((None,) + w_uq_p.shape[1:], lambda b, i: (layer, 0, 0)),
                  pl.BlockSpec((None,) + w_ukv_p.shape[1:], lambda b, i: (layer, 0, 0)),
                  pl.BlockSpec((None, 1, Q_LORA), lambda b, i: (layer, 0, 0)),
                  pl.BlockSpec((None, 1, KV_LORA), lambda b, i: (layer, 0, 0)),
                  tab, tab, tab],
        out_specs=[pl.BlockSpec((MLA_HEADS, tm, kd), lambda b, i: (0, row(b, i), 0)),
                   pl.BlockSpec((MLA_HEADS, tm, kd), lambda b, i: (0, row(b, i), 0)),
                   pl.BlockSpec((MLA_HEADS, tm, 2 * V_HEAD), lambda b, i: (0, row(b, i), 0))],
        out_shape=[jax.ShapeDtypeStruct((MLA_HEADS, r, kd), bf16),
                   jax.ShapeDtypeStruct((MLA_HEADS, r, kd), bf16),
                   jax.ShapeDtypeStruct((MLA_HEADS, r, 2 * V_HEAD), bf16)],
        compiler_params=_params(("parallel", "arbitrary"), 48),
        name="mla_prep",
    )(u, u, u, w_uq_p, w_ukv_p, q_norm_g, kv_norm_g, *tables)


ATTN_TILES_PER_BODY = 12


def _attn_kernel(q_ref, k_ref, v_ref, o_ref, s0_ref, s1_ref, p0_ref, p1_ref, *, tq, tk):
    lp = k_ref.shape[0]
    nq, nk = lp // tq, lp // tk
    n_tiles = nq * nk
    s_refs, p_refs = (s0_ref, s1_ref), (p0_ref, p1_ref)
    for ref in s_refs + p_refs:
        ref[...] = jnp.zeros_like(ref)

    def tile(x):
        x = jnp.clip(x, 0, n_tiles - 1)
        i = x // nk
        return pl.multiple_of(i * tq, tq), pl.multiple_of((x - i * nk) * tk, tk)

    def scores(x, slot):
        q0, k0 = tile(x)
        s_refs[slot][...] = _dot_nt(q_ref[pl.ds(q0, tq), :], k_ref[pl.ds(k0, tk), :])

    def softmax(x, slot, m):
        _, k0 = tile(x)
        m = jnp.where(k0 == 0, -jnp.inf, m)
        s = s_refs[slot][...]
        m_new = jnp.maximum(m, jnp.max(s, axis=1, keepdims=True))
        p_refs[slot][...] = jnp.exp2(s - m_new).astype(bf16)
        return m_new, jnp.exp2(m - m_new)

    def weighted(x, slot, alpha, acc):
        q0, k0 = tile(x)
        acc = alpha * acc + _dot(p_refs[slot][...], v_ref[pl.ds(k0, tk), :])
        out = acc[:, :V_HEAD] * (1.0 / acc[:, V_HEAD:V_HEAD + 1])
        o_ref[pl.ds(q0, tq), :] = out.astype(o_ref.dtype)
        return acc

    n_bodies = -(-(n_tiles + 2) // ATTN_TILES_PER_BODY)
    start = n_tiles + 2 - ATTN_TILES_PER_BODY * n_bodies

    def body(u, carry):
        m, alpha, acc = carry
        for e in range(ATTN_TILES_PER_BODY):
            x = ATTN_TILES_PER_BODY * u + e + start
            acc = weighted(x - 2, e % 2, alpha, acc)
            m, alpha = softmax(x - 1, (e + 1) % 2, m)
            scores(x, e % 2)
        return m, alpha, acc

    zero = jnp.zeros((tq, 1), f32)
    lax.fori_loop(0, n_bodies, body, (zero, zero, jnp.zeros((tq, 2 * V_HEAD), f32)))


def _attention(qc, kc, v, bsz, lp):
    r = v.shape[1]
    tq = _pick(lp, (384, 128))
    tk = _pick(lp, (1408, 384, 128))
    kd = qc.shape[-1]
    blk = lambda width: pl.BlockSpec((None, lp, width), lambda b, h: (h, b, 0))
    return pl.pallas_call(
        functools.partial(_attn_kernel, tq=tq, tk=tk),
        grid=(bsz, MLA_HEADS),
        in_specs=[blk(kd), blk(kd), blk(2 * V_HEAD)],
        out_specs=blk(V_HEAD),
        out_shape=jax.ShapeDtypeStruct((MLA_HEADS, r, V_HEAD), bf16),
        scratch_shapes=[pltpu.VMEM((tq, tk), f32), pltpu.VMEM((tq, tk), f32),
                        pltpu.VMEM((tq, tk), bf16), pltpu.VMEM((tq, tk), bf16)],
        compiler_params=_params(("parallel", "arbitrary"), 48),
        name="attention",
    )(qc, kc, v)


def _layer_norm(y, g, b):
    mu = jnp.mean(y, axis=-1, keepdims=True)
    var = jnp.mean(jnp.square(y - mu), axis=-1, keepdims=True)
    return (y - mu) * lax.rsqrt(var + EPS) * g + b


def _outproj_kernel(ml_ref, at_ref, w_ref, h_ref, g_ref, b_ref, o_ref, ob_ref, *, alpha):
    at = jnp.concatenate([at_ref[h] for h in range(MLA_HEADS)], axis=1)
    mix = _dot(ml_ref[...], w_ref[:ML_WIDTH, :]) + _dot(at, w_ref[ML_WIDTH:, :])
    y = _layer_norm(alpha * h_ref[...] + mix, g_ref[...], b_ref[...])
    o_ref[...] = y
    ob_ref[...] = y.astype(bf16)


def _outproj(ml, at, w_out, h, ln_g, ln_b, layer, alpha):
    r, d = h.shape
    tm = _pick(r, (384, 128))
    vec = pl.BlockSpec((None, 1, d), lambda i: (layer, 0, 0))
    return pl.pallas_call(
        functools.partial(_outproj_kernel, alpha=alpha),
        grid=(r // tm,),
        in_specs=[pl.BlockSpec((tm, ML_WIDTH), lambda i: (i, 0)),
                  pl.BlockSpec((MLA_HEADS, tm, V_HEAD), lambda i: (0, i, 0)),
                  pl.BlockSpec((None,) + w_out.shape[1:], lambda i: (layer, 0, 0)),
                  pl.BlockSpec((tm, d), lambda i: (i, 0)), vec, vec],
        out_specs=[pl.BlockSpec((tm, d), lambda i: (i, 0)), pl.BlockSpec((tm, d), lambda i: (i, 0))],
        out_shape=[jax.ShapeDtypeStruct((r, d), f32), jax.ShapeDtypeStruct((r, d), bf16)],
        compiler_params=_params(("parallel",), 56),
        name="outproj_ln",
    )(ml, at, w_out, h, ln_g, ln_b)


HALO = 16


def _ffn_up_kernel(prev_ref, main_ref, next_ref, wg_ref, wv_ref, cw_ref, cb_ref, o_ref, ext_scr, *, lp):
    tm = main_ref.shape[0]

    @pl.when(pl.program_id(2) == 0)
    def _():
        ext_scr[:HALO, :] = prev_ref[...]
        ext_scr[HALO:HALO + tm, :] = main_ref[...]
        ext_scr[HALO + tm:, :] = next_ref[...]

    gate = _dot(ext_scr[...], wg_ref[...])
    pos = pl.program_id(1) * tm - HALO + lax.broadcasted_iota(jnp.int32, (tm + 2 * HALO, 1), 0)
    gate = jnp.where((pos >= FRONT) & (pos < lp), gate, 0.0)
    rows = tm + 2 * HALO
    left = pltpu.roll(gate, 1, 0)[HALO:HALO + tm]
    right = pltpu.roll(gate, rows - 1, 0)[HALO:HALO + tm]
    conv = left * cw_ref[0:1, :] + gate[HALO:HALO + tm] * cw_ref[1:2, :] + right * cw_ref[2:3, :] + cb_ref[...]
    val = _dot(main_ref[...], wv_ref[...])
    o_ref[...] = (conv * jax.nn.sigmoid(conv) * val).astype(o_ref.dtype)


def _ffn_up(hb, w_up, conv_w, conv_b, layer, bsz, lp):
    r, d = hb.shape
    dff = conv_b.shape[-1]
    tm = _pick(lp, (1408, 384, 128))
    tn = _pick(dff, (512, 256, 128))
    nt, nj = lp // tm, dff // tn
    hblk = tm // HALO
    last = r // HALO - 1
    return pl.pallas_call(
        functools.partial(_ffn_up_kernel, lp=lp),
        grid=(bsz, nt, nj),
        in_specs=[pl.BlockSpec((HALO, d), lambda b, i, j: (jnp.maximum((b * nt + i) * hblk - 1, 0), 0)),
                  pl.BlockSpec((tm, d), lambda b, i, j: (b * nt + i, 0)),
                  pl.BlockSpec((HALO, d), lambda b, i, j: (jnp.minimum((b * nt + i + 1) * hblk, last), 0)),
                  pl.BlockSpec((None, d, tn), lambda b, i, j: (layer, 0, j)),
                  pl.BlockSpec((None, d, tn), lambda b, i, j: (layer, 0, nj + j)),
                  pl.BlockSpec((None, 3, tn), lambda b, i, j: (layer, 0, j)),
                  pl.BlockSpec((None, 1, tn), lambda b, i, j: (layer, 0, j))],
        out_specs=pl.BlockSpec((tm, tn), lambda b, i, j: (b * nt + i, j)),
        out_shape=jax.ShapeDtypeStruct((r, dff), bf16),
        scratch_shapes=[pltpu.VMEM((tm + 2 * HALO, d), bf16)],
        compiler_params=_params(("parallel", "parallel", "arbitrary"), 56),
        name="ffn_up_conv",
    )(hb, hb, hb, w_up, w_up, conv_w, conv_b)


def _ffn_down_kernel(a_ref, w_ref, h_ref, g_ref, b_ref, o_ref, ob_ref, *, alpha):
    y = _layer_norm(alpha * h_ref[...] + _dot(a_ref[...], w_ref[...]), g_ref[...], b_ref[...])
    o_ref[...] = y
    ob_ref[...] = y.astype(bf16)


def _ffn_down(act, w_down, h, ln_g, ln_b, layer, alpha):
    r, d = h.shape
    dff = act.shape[1]
    tm = _pick(r, (384, 128))
    vec = pl.BlockSpec((None, 1, d), lambda i: (layer, 0, 0))
    return pl.pallas_call(
        functools.partial(_ffn_down_kernel, alpha=alpha),
        grid=(r // tm,),
        in_specs=[pl.BlockSpec((tm, dff), lambda i: (i, 0)),
                  pl.BlockSpec((None, dff, d), lambda i: (layer, 0, 0), pipeline_mode=pl.Buffered(1)),
                  pl.BlockSpec((tm, d), lambda i: (i, 0)), vec, vec],
        out_specs=[pl.BlockSpec((tm, d), lambda i: (i, 0)), pl.BlockSpec((tm, d), lambda i: (i, 0))],
        out_shape=[jax.ShapeDtypeStruct((r, d), f32), jax.ShapeDtypeStruct((r, d), bf16)],
        compiler_params=_params(("parallel",), 60),
        name="ffn_down_ln",
    )(act, w_down, h, ln_g, ln_b)


def _rope_tables(lp):
    inv_freq = ROPE_THETA ** (-jnp.arange(0, QK_ROPE, 2, dtype=f32) / QK_ROPE)
    pos = jnp.maximum(jnp.arange(lp, dtype=f32) - FRONT, 0.0)
    ang = pos[:, None] * inv_freq[None, :]
    cos, sin = jnp.cos(ang), jnp.sin(ang)
    zero = jnp.zeros_like(sin)
    reps = LANE // QK_ROPE
    cos_t = jnp.tile(jnp.concatenate([cos, cos], axis=1), (1, reps))
    sin_a = jnp.tile(jnp.concatenate([-sin, zero], axis=1), (1, reps))
    sin_b = jnp.tile(jnp.concatenate([zero, sin], axis=1), (1, reps))
    return cos_t, sin_a, sin_b


def kernel(x_prompt, x_sample, meta_tokens, w_in, b_gates, ml_norm_g, q_norm_g, kv_norm_g, w_uq, w_ukv,
           w_out, ln1_g, ln1_b, w_up, conv_w, conv_b, w_down, ln2_g, ln2_b):
    depth, d, _ = w_in.shape
    assert x_prompt.shape[1:] == x_sample.shape[1:]
    n_prompt = x_prompt.shape[0]
    x = jnp.concatenate([x_prompt, x_sample], axis=0)
    bsz, seq, _ = x.shape
    assert seq % LANE == 0
    lp = LANE + seq
    alpha = (2 * depth) ** 0.25

    gate_lo = 4 * ML_WIDTH
    w_main = jnp.concatenate(
        [w_in[..., :ML_WIDTH], w_in[..., 2 * ML_WIDTH:gate_lo], w_in[..., gate_lo + N_GATES:], w_in[..., -QK_ROPE:],
         jnp.zeros((depth, d, N_MAIN - COL_KR - 2 * QK_ROPE), w_in.dtype)], axis=-1).astype(bf16)
    wk_t = jnp.swapaxes(w_in[..., ML_WIDTH:2 * ML_WIDTH], 1, 2).astype(bf16)
    h4 = ML_HEADS
    perm = jnp.array(list(range(0, h4)) + list(range(2 * h4, 3 * h4))
                     + list(range(h4, 2 * h4)) + list(range(3 * h4, 4 * h4)))
    wg_t = jnp.swapaxes(w_in[..., gate_lo:gate_lo + N_GATES], 1, 2)[:, perm, :].astype(bf16)
    bg = b_gates[:, perm, None].astype(f32)
    uq = w_uq.reshape(depth, Q_LORA, MLA_HEADS, QK_NOPE + QK_ROPE)
    w_uq_p = jnp.concatenate([uq[..., :QK_NOPE].reshape(depth, Q_LORA, -1),
                              uq[..., QK_NOPE:].reshape(depth, Q_LORA, -1)], axis=-1).astype(bf16)
    ukv = w_ukv.reshape(depth, KV_LORA, MLA_HEADS, QK_NOPE + V_HEAD)
    w_ukv_p = jnp.concatenate([ukv[..., :QK_NOPE].reshape(depth, KV_LORA, -1),
                               ukv[..., QK_NOPE:].reshape(depth, KV_LORA, -1)], axis=-1).astype(bf16)
    w_out_b, w_up_b, w_down_b = w_out.astype(bf16), w_up.astype(bf16), w_down.astype(bf16)
    vec3 = lambda a: a[:, None, :].astype(f32)
    ml_g, q_g, kv_g = vec3(ml_norm_g), vec3(q_norm_g), vec3(kv_norm_g)
    l1g, l1b, l2g, l2b, cb = vec3(ln1_g), vec3(ln1_b), vec3(ln2_g), vec3(ln2_b), vec3(conv_b)
    cw = conv_w.astype(f32)
    tables = _rope_tables(lp)

    meta = jnp.broadcast_to(meta_tokens.astype(x.dtype)[None], (bsz, N_META, d))
    h = jnp.concatenate([jnp.zeros((bsz, FRONT, d), x.dtype), meta, x], axis=1).reshape(bsz * lp, d)
    hb = h.astype(bf16)
    for layer in range(depth):
        u = _inproj(hb, w_main, layer)
        gq = _gates(hb, wg_t, bg, layer, bsz, lp)
        kt = _keys_t(hb, wk_t, layer)
        ml = _mlstm(u, kt, gq, ml_g, layer, bsz, lp)
        qc, kc, v = _mla_prep(u, w_uq_p, w_ukv_p, q_g, kv_g, tables, layer, bsz, lp)
        at = _attention(qc, kc, v, bsz, lp)
        h, hb = _outproj(ml, at, w_out_b, h, l1g, l1b, layer, alpha)
        act = _ffn_up(hb, w_up_b, cw, cb, layer, bsz, lp)
        h, hb = _ffn_down(act, w_down_b, h, l2g, l2b, layer, alpha)
    y = h.reshape(bsz, lp, d)[:, LANE:]
    return (y[:n_prompt], y[n_prompt:])
```

```python
import functools
import math

import jax
import jax.numpy as jnp
from jax import lax
from jax.experimental import pallas as pl
from jax.experimental.pallas import tpu as pltpu

N_META = 16
ML_HEADS = 4
ML_HEAD_DIM = 256
ML_WIDTH = ML_HEADS * ML_HEAD_DIM
N_GATES = 4 * ML_HEADS
MLA_HEADS = 8
Q_LORA = 512
KV_LORA = 256
QK_NOPE = 128
QK_ROPE = 64
V_HEAD = 128
MLA_WIDTH = MLA_HEADS * V_HEAD
ROPE_THETA = 10000.0
EPS = 1e-5
NEG = -1e30

LANE = 128
FRONT = LANE - N_META
CHUNK = 128
ROPE_HALF = QK_ROPE // 2
SOFTMAX_SCALE_LOG2 = (QK_NOPE + QK_ROPE) ** -0.5 * math.log2(math.e)
MIB = 1024 * 1024

COL_Q, COL_V, COL_O = 0, ML_WIDTH, 2 * ML_WIDTH
COL_DQ = 3 * ML_WIDTH
COL_DKV = COL_DQ + Q_LORA
COL_KR = COL_DKV + KV_LORA
N_MAIN = COL_KR + 2 * QK_ROPE + LANE

f32 = jnp.float32
bf16 = jnp.bfloat16


def _pick(n, cands):
    for c in cands:
        if n % c == 0:
            return c
    raise ValueError(f"no tile for {n} in {cands}")


def _params(sem, vmem_mib):
    return pltpu.CompilerParams(dimension_semantics=sem, vmem_limit_bytes=vmem_mib * MIB)


def _dot(a, b):
    return jnp.dot(a, b, preferred_element_type=f32)


def _dot_nt(a, b):
    return lax.dot_general(a, b, (((1,), (1,)), ((), ())), preferred_element_type=f32)


def _inproj_kernel(a_ref, w_ref, o_ref):
    o_ref[...] = _dot(a_ref[...], w_ref[...]).astype(o_ref.dtype)


def _keys_t_kernel(w_ref, h_ref, o_ref):
    o_ref[...] = (_dot_nt(w_ref[...], h_ref[...]) * ML_HEAD_DIM ** -0.5).astype(o_ref.dtype)


def _keys_t(hb, wk_t, layer):
    r, d = hb.shape
    tm = _pick(r, (1408, 384, 128))
    return pl.pallas_call(
        _keys_t_kernel,
        grid=(r // tm,),
        in_specs=[pl.BlockSpec((None, ML_WIDTH, d), lambda i: (layer, 0, 0)),
                  pl.BlockSpec((tm, d), lambda i: (i, 0))],
        out_specs=pl.BlockSpec((ML_WIDTH, tm), lambda i: (0, i)),
        out_shape=jax.ShapeDtypeStruct((ML_WIDTH, r), bf16),
        compiler_params=_params(("parallel",), 48),
        name="keys_t",
    )(wk_t, hb)


def _inproj(hb, w_main, layer):
    r, d = hb.shape
    n = w_main.shape[-1]
    tm = _pick(r, (1408, 384, 128))
    tn = _pick(n, (1024, 512, 256, 128))
    return pl.pallas_call(
        _inproj_kernel,
        grid=(r // tm, n // tn),
        in_specs=[pl.BlockSpec((tm, d), lambda i, j: (i, 0)),
                  pl.BlockSpec((None, d, tn), lambda i, j: (layer, 0, j))],
        out_specs=pl.BlockSpec((tm, tn), lambda i, j: (i, j)),
        out_shape=jax.ShapeDtypeStruct((r, n), bf16),
        compiler_params=_params(("parallel", "arbitrary"), 48),
        name="inproj",
    )(hb, w_main)


def _gates_kernel(h_ref, wg_ref, bg_ref, o_ref):
    tm = h_ref.shape[0]
    half = N_GATES // 2
    g = _dot_nt(wg_ref[...], h_ref[...]) + bg_ref[...]
    pos = pl.program_id(1) * tm + lax.broadcasted_iota(jnp.int32, (half, tm), 1)
    unused = pos < FRONT
    gi, gf = g[:half], g[half:]
    li = jnp.where(unused, NEG, gi)
    lf = jnp.where(unused, 0.0, jnp.minimum(gf, 0.0) - jnp.log1p(jnp.exp(-jnp.abs(gf))))
    lane = lax.broadcasted_iota(jnp.int32, (half, CHUNK), 1)
    is_fwd = lax.broadcasted_iota(jnp.int32, (half, CHUNK), 0) < ML_HEADS
    for c in range(tm // CHUNK):
        sl = slice(c * CHUNK, (c + 1) * CHUNK)
        pre = suf = lf[:, sl]
        k = 1
        while k < CHUNK:
            pre = pre + jnp.where(lane >= k, pltpu.roll(pre, k, 1), 0.0)
            suf = suf + jnp.where(lane < CHUNK - k, pltpu.roll(suf, CHUNK - k, 1), 0.0)
            k *= 2
        b = jnp.where(is_fwd, pre, suf)
        o_ref[:half, sl] = li[:, sl] - b
        o_ref[half:, sl] = b


def _gates(hb, wg_t, bg, layer, bsz, lp):
    r, d = hb.shape
    tm = _pick(lp, (1408, 384, 128))
    nt = lp // tm
    return pl.pallas_call(
        _gates_kernel,
        grid=(bsz, nt),
        in_specs=[pl.BlockSpec((tm, d), lambda b, i: (b * nt + i, 0)),
                  pl.BlockSpec((None, N_GATES, d), lambda b, i: (layer, 0, 0)),
                  pl.BlockSpec((None, N_GATES, 1), lambda b, i: (layer, 0, 0))],
        out_specs=pl.BlockSpec((None, N_GATES, tm), lambda b, i: (b, 0, i)),
        out_shape=jax.ShapeDtypeStruct((bsz, N_GATES, lp), f32),
        compiler_params=_params(("parallel", "arbitrary"), 32),
        name="gates",
    )(hb, wg_t, bg)


def _mlstm_kernel(q_ref, kt_ref, v_ref, o_ref, gq_ref, ng_ref, out_ref, hf_scr, hb_scr, c_scr, m_scr, s_scr):
    head = pl.program_id(1)
    lp, dh = q_ref.shape
    t = CHUNK
    nc = lp // t
    c_scr[...] = jnp.zeros_like(c_scr)
    m_scr[...] = jnp.zeros_like(m_scr)
    ti = lax.broadcasted_iota(jnp.int32, (t, t), 0)
    si = lax.broadcasted_iota(jnp.int32, (t, t), 1)
    masks = (si <= ti, si >= ti)
    eye = si == ti
    half = N_GATES // 2
    gate_row = lax.broadcasted_iota(jnp.int32, (half, t), 0)
    ones_col = jnp.where(lax.broadcasted_iota(jnp.int32, (t, LANE), 1) == 0, 1.0, 0.0).astype(bf16)

    def qk_ahead(c, dirn):
        base = pl.multiple_of(jnp.clip(c, 0, nc - 1) * t, t)
        s_scr[dirn] = _dot(q_ref[pl.ds(base, t), :], kt_ref[:, pl.ds(base, t)])

    def chunk(c, dirn, h_scr):
        base = pl.multiple_of(c * t, t)
        q = q_ref[pl.ds(base, t), :]
        kt = kt_ref[:, pl.ds(base, t)]
        v = jnp.concatenate([v_ref[pl.ds(base, t), :], ones_col], axis=1)
        mine = gate_row == head + ML_HEADS * dirn
        r_row = jnp.sum(jnp.where(mine, gq_ref[:half, pl.ds(base, t)], 0.0), axis=0, keepdims=True)
        b_row = jnp.sum(jnp.where(mine, gq_ref[half:, pl.ds(base, t)], 0.0), axis=0, keepdims=True)
        m_state = m_scr[dirn, 0:1, 0:1]
        mask = masks[dirn]
        m_col = jnp.maximum(jnp.max(jnp.where(mask, r_row, -jnp.inf), axis=1, keepdims=True), m_state)
        b_col = jnp.sum(jnp.where(eye, b_row, 0.0), axis=1, keepdims=True)
        m_all = jnp.maximum(jnp.max(r_row, axis=1, keepdims=True), m_state)
        b_last = b_row[:, 0:1] if dirn else b_row[:, t - 1:t]
        s = s_scr[dirn] * jnp.exp(jnp.where(mask, r_row - m_col, -jnp.inf))
        both = jnp.exp(m_state - m_col) * _dot(q, c_scr[dirn].astype(bf16)) + _dot(s.astype(bf16), v)
        floor = jnp.exp(-(b_col + m_col))
        h_scr[pl.ds(base, t), :] = both[:, :dh] / jnp.maximum(jnp.abs(both[:, dh:dh + 1]), floor)
        ktw = (kt.astype(f32) * jnp.exp(r_row - m_all)).astype(bf16)
        c_scr[dirn] = jnp.exp(m_state - m_all) * c_scr[dirn] + _dot(ktw, v)
        m_scr[dirn] = jnp.broadcast_to(b_last + m_all, m_scr.shape[1:])

    def finish(c):
        base = pl.multiple_of(c * t, t)
        hs = hf_scr[pl.ds(base, t), :] + hb_scr[pl.ds(base, t), :]
        mu = jnp.mean(hs, axis=1, keepdims=True)
        var = jnp.mean(jnp.square(hs - mu), axis=1, keepdims=True)
        hn = (hs - mu) * lax.rsqrt(var + EPS) * ng_ref[...]
        gate = jax.nn.sigmoid(o_ref[pl.ds(base, t), :].astype(f32))
        out_ref[pl.ds(base, t), :] = (gate * hn).astype(out_ref.dtype)

    def scan_step(c):
        chunk(c, 0, hf_scr)
        chunk(nc - 1 - c, 1, hb_scr)
        qk_ahead(c + 1, 0)
        qk_ahead(nc - 2 - c, 1)

    def first_half(c, carry):
        scan_step(c)
        return carry

    def second_half(c, carry):
        finish(c - 1)
        finish(nc - c)
        scan_step(c)
        return carry

    qk_ahead(0, 0)
    qk_ahead(nc - 1, 1)
    meet = (nc + 1) // 2
    lax.fori_loop(0, meet, first_half, 0, unroll=4)
    lax.fori_loop(meet, nc, second_half, 0, unroll=4)
    finish(nc - 1)
    finish(0)


def _mlstm(u, kt, gq, ml_norm_g, layer, bsz, lp):
    r = u.shape[0]
    dh = ML_HEAD_DIM

    def col(first):
        return pl.BlockSpec((lp, dh), lambda b, h: (b, first // dh + h))

    return pl.pallas_call(
        _mlstm_kernel,
        grid=(bsz, ML_HEADS),
        in_specs=[col(COL_Q), pl.BlockSpec((dh, lp), lambda b, h: (h, b)), col(COL_V), col(COL_O),
                  pl.BlockSpec((None, N_GATES, lp), lambda b, h: (b, 0, 0)),
                  pl.BlockSpec((None, 1, dh), lambda b, h: (layer, 0, h))],
        out_specs=pl.BlockSpec((lp, dh), lambda b, h: (b, h)),
        out_shape=jax.ShapeDtypeStruct((r, ML_WIDTH), bf16),
        scratch_shapes=[pltpu.VMEM((lp, dh), f32), pltpu.VMEM((lp, dh), f32),
                        pltpu.VMEM((2, dh, dh + LANE), f32), pltpu.VMEM((2, 8, LANE), f32),
                        pltpu.VMEM((2, CHUNK, CHUNK), f32)],
        compiler_params=_params(("parallel", "arbitrary"), 48),
        name="mlstm",
    )(u, kt, u, u, gq, ml_norm_g)


def _rms(x, g):
    return x * lax.rsqrt(jnp.mean(jnp.square(x), axis=-1, keepdims=True) + EPS) * g


def _rope(x, cos, sin_a, sin_b):
    n = x.shape[1]
    reps = n // LANE
    if reps > 1:
        cos, sin_a, sin_b = (jnp.concatenate([a] * reps, axis=1) for a in (cos, sin_a, sin_b))
    return x * cos + pltpu.roll(x, n - ROPE_HALF, 1) * sin_a + pltpu.roll(x, ROPE_HALF, 1) * sin_b


def _mla_prep_kernel(dq_ref, dkv_ref, kr_ref, wuq_ref, wukv_ref, qg_ref, kvg_ref,
                     cos_ref, sa_ref, sb_ref, qc_ref, kc_ref, v_ref):
    cos, sa, sb = cos_ref[...], sa_ref[...], sb_ref[...]
    q = _dot(_rms(dq_ref[...].astype(f32), qg_ref[...]).astype(bf16), wuq_ref[...]) * SOFTMAX_SCALE_LOG2
    kv =_dot(_rms(dkv_ref[...].astype(f32), kvg_ref[...]).astype(bf16), wukv_ref[...])
    q_rope = _rope(q[:, MLA_HEADS * QK_NOPE:], cos, sa, sb)
    k_rope = _rope(kr_ref[...].astype(f32), cos, sa, sb)
    tm = q.shape[0]
    lane = lax.broadcasted_iota(jnp.int32, (tm, LANE), 1)
    pos = pl.program_id(1) * tm + lax.broadcasted_iota(jnp.int32, (tm, LANE), 0)
    low = lane < QK_ROPE
    marker = lane == QK_ROPE
    one_hot = jnp.where(marker, 1.0, 0.0)
    k_tail = jnp.where(low, k_rope, jnp.where(marker & (pos < FRONT), -jnp.inf, 0.0)).astype(bf16)
    ones_col = jnp.where(lane == 0, 1.0, 0.0).astype(bf16)
    for h in range(MLA_HEADS):
        pair = q_rope[:, (h // 2) * LANE:(h // 2 + 1) * LANE]
        if h % 2:
            pair = pltpu.roll(pair, QK_ROPE, 1)
        qc_ref[h, :, :QK_NOPE] = q[:, h * QK_NOPE:(h + 1) * QK_NOPE].astype(bf16)
        qc_ref[h, :, QK_NOPE:] = jnp.where(low, pair, one_hot).astype(bf16)
        kc_ref[h, :, :QK_NOPE] = kv[:, h * QK_NOPE:(h + 1) * QK_NOPE].astype(bf16)
        kc_ref[h, :, QK_NOPE:] = k_tail
        v_ref[h, :, :V_HEAD] = kv[:, (MLA_HEADS + h) * V_HEAD:(MLA_HEADS + h + 1) * V_HEAD].astype(bf16)
        v_ref[h, :, V_HEAD:] = ones_col


def _mla_prep(u, w_uq_p, w_ukv_p, q_norm_g, kv_norm_g, tables, layer, bsz, lp):
    r = u.shape[0]
    tm = _pick(lp, (384, 128))
    nt = lp // tm
    kd = QK_NOPE + LANE
    tab = pl.BlockSpec((tm, LANE), lambda b, i: (i, 0))
    row = lambda b, i: b * nt + i
    return pl.pallas_call(
        _mla_prep_kernel,
        grid=(bsz, nt),
        in_specs=[pl.BlockSpec((tm, Q_LORA), lambda b, i: (row(b, i), COL_DQ // Q_LORA)),
                  pl.BlockSpec((tm, KV_LORA), lambda b, i: (row(b, i), COL_DKV // KV_LORA)),
                  pl.BlockSpec((tm, LANE), lambda b, i: (row(b, i), COL_KR // LANE)),
                  pl.BlockSpec((None,) + w_uq_p.shape[1:], lambda b, i: (layer, 0, 0)),
                  pl.BlockSpec((None,) + w_ukv_p.shape[1:], lambda b, i: (layer, 0, 0)),
                  pl.BlockSpec((None, 1, Q_LORA), lambda b, i: (layer, 0, 0)),
                  pl.BlockSpec((None, 1, KV_LORA), lambda b, i: (layer, 0, 0)),
                  tab, tab, tab],
        out_specs=[pl.BlockSpec((MLA_HEADS, tm, kd), lambda b, i: (0, row(b, i), 0)),
                   pl.BlockSpec((MLA_HEADS, tm, kd), lambda b, i: (0, row(b, i), 0)),
                   pl.BlockSpec((MLA_HEADS, tm, 2 * V_HEAD), lambda b, i: (0, row(b, i), 0))],
        out_shape=[jax.ShapeDtypeStruct((MLA_HEADS, r, kd), bf16),
                   jax.ShapeDtypeStruct((MLA_HEADS, r, kd), bf16),
                   jax.ShapeDtypeStruct((MLA_HEADS, r, 2 * V_HEAD), bf16)],
        compiler_params=_params(("parallel", "arbitrary"), 48),
        name="mla_prep",
    )(u, u, u, w_uq_p, w_ukv_p, q_norm_g, kv_norm_g, *tables)


ATTN_HEADS_PER_STEP = 2
ATTN_TILES_PER_BODY = 8


def _attn_kernel(q_ref, k_ref, v_ref, o_ref, s0_ref, s1_ref, p0_ref, p1_ref, *, tq):
    nh, lp, _ = k_ref.shape
    nq = lp // tq
    n_tiles = nh * nq
    s_refs, p_refs = (s0_ref, s1_ref), (p0_ref, p1_ref)
    for ref in s_refs + p_refs:
        ref[...] = jnp.zeros_like(ref)

    def tile(x):
        x = jnp.clip(x, 0, n_tiles - 1)
        hh = x // nq
        return hh, pl.ds(pl.multiple_of((x - hh * nq) * tq, tq), tq)

    def scores(x, slot):
        hh, rows = tile(x)
        s_refs[slot][...] = _dot_nt(q_ref[hh, rows, :], k_ref[hh])

    def softmax(slot):
        s = s_refs[slot][...]
        p_refs[slot][...] = jnp.exp2(s - jnp.max(s, axis=1, keepdims=True)).astype(bf16)

    def weighted(x, slot):
        hh, rows = tile(x)
        acc = _dot(p_refs[slot][...], v_ref[hh])
        o_ref[hh, rows, :] = (acc[:, :V_HEAD] * (1.0 / acc[:, V_HEAD:V_HEAD + 1])).astype(o_ref.dtype)

    n_bodies = -(-(n_tiles + 2) // ATTN_TILES_PER_BODY)
    start = n_tiles + 2 - ATTN_TILES_PER_BODY * n_bodies

    def body(u, carry):
        for e in range(ATTN_TILES_PER_BODY):
            x = ATTN_TILES_PER_BODY * u + e + start
            weighted(x - 2, e % 2)
            softmax((e + 1) % 2)
            scores(x, e % 2)
        return carry

    lax.fori_loop(0, n_bodies, body, 0)


def _attention(qc, kc, v, bsz, lp):
    r = v.shape[1]
    tq = _pick(lp, (384, 128))
    kd = qc.shape[-1]
    nh = ATTN_HEADS_PER_STEP
    blk = lambda width: pl.BlockSpec((nh, lp, width), lambda b, h: (h, b, 0))
    return pl.pallas_call(
        functools.partial(_attn_kernel, tq=tq),
        grid=(bsz, MLA_HEADS // nh),
        in_specs=[blk(kd), blk(kd), blk(2 * V_HEAD)],
        out_specs=blk(V_HEAD),
        out_shape=jax.ShapeDtypeStruct((MLA_HEADS, r, V_HEAD), bf16),
        scratch_shapes=[pltpu.VMEM((tq, lp), f32), pltpu.VMEM((tq, lp), f32),
                        pltpu.VMEM((tq, lp), bf16), pltpu.VMEM((tq, lp), bf16)],
        compiler_params=_params(("parallel", "arbitrary"), 60),
        name="attention",
    )(qc, kc, v)


def _layer_norm(y, g, b):
    mu = jnp.mean(y, axis=-1, keepdims=True)
    var = jnp.mean(jnp.square(y - mu), axis=-1, keepdims=True)
    return (y - mu) * lax.rsqrt(var + EPS) * g + b


OUTPROJ_SUBTILES = 2


def _outproj_kernel(ml_ref, at_ref, w_ref, h_ref, g_ref, b_ref, o_ref, ob_ref, *, alpha):
    ts = h_ref.shape[0] // OUTPROJ_SUBTILES
    for sub in range(OUTPROJ_SUBTILES):
        rows = slice(sub * ts, (sub + 1) * ts)
        at = jnp.concatenate([at_ref[h, rows, :] for h in range(MLA_HEADS)], axis=1)
        mix = _dot(ml_ref[rows, :], w_ref[:ML_WIDTH, :]) + _dot(at, w_ref[ML_WIDTH:, :])
        y = _layer_norm(alpha * h_ref[rows, :] + mix, g_ref[...], b_ref[...])
        o_ref[rows, :] = y
        ob_ref[rows, :] = y.astype(bf16)


def _outproj(ml, at, w_out, h, ln_g, ln_b, layer, alpha):
    r, d = h.shape
    tm = _pick(r, (768, 384, 128))
    vec = pl.BlockSpec((None, 1, d), lambda i: (layer, 0, 0))
    return pl.pallas_call(
        functools.partial(_outproj_kernel, alpha=alpha),
        grid=(r // tm,),
        in_specs=[pl.BlockSpec((tm, ML_WIDTH), lambda i: (i, 0)),
                  pl.BlockSpec((MLA_HEADS, tm, V_HEAD), lambda i: (0, i, 0)),
                  pl.BlockSpec((None,) + w_out.shape[1:], lambda i: (layer, 0, 0), pipeline_mode=pl.Buffered(1)),
                  pl.BlockSpec((tm, d), lambda i: (i, 0)), vec, vec],
        out_specs=[pl.BlockSpec((tm, d), lambda i: (i, 0)), pl.BlockSpec((tm, d), lambda i: (i, 0))],
        out_shape=[jax.ShapeDtypeStruct((r, d), f32), jax.ShapeDtypeStruct((r, d), bf16)],
        compiler_params=_params(("parallel",), 56),
        name="outproj_ln",
    )(ml, at, w_out, h, ln_g, ln_b)


HALO = 16


def _ffn_up_kernel(prev_ref, main_ref, next_ref, wg_ref, wv_ref, cw_ref, cb_ref, o_ref, ext_scr, *, lp):
    tm = main_ref.shape[0]

    @pl.when(pl.program_id(2) == 0)
    def _():
        ext_scr[:HALO, :] = prev_ref[...]
        ext_scr[HALO:HALO + tm, :] = main_ref[...]
        ext_scr[HALO + tm:, :] = next_ref[...]

    gate = _dot(ext_scr[...], wg_ref[...])
    pos = pl.program_id(1) * tm - HALO + lax.broadcasted_iota(jnp.int32, (tm + 2 * HALO, 1), 0)
    gate = jnp.where((pos >= FRONT) & (pos < lp), gate, 0.0)
    rows = tm + 2 * HALO
    left = pltpu.roll(gate, 1, 0)[HALO:HALO + tm]
    right = pltpu.roll(gate, rows - 1, 0)[HALO:HALO + tm]
    conv = left * cw_ref[0:1, :] + gate[HALO:HALO + tm] * cw_ref[1:2, :] + right * cw_ref[2:3, :] + cb_ref[...]
    val = _dot(main_ref[...], wv_ref[...])
    o_ref[...] = (conv * jax.nn.sigmoid(conv) * val).astype(o_ref.dtype)


def _ffn_up(hb, w_up, conv_w, conv_b, layer, bsz, lp):
    r, d = hb.shape
    dff = conv_b.shape[-1]
    tm = _pick(lp, (1408, 384, 128))
    tn = _pick(dff, (512, 256, 128))
    nt, nj = lp // tm, dff // tn
    hblk = tm // HALO
    last = r // HALO - 1
    return pl.pallas_call(
        functools.partial(_ffn_up_kernel, lp=lp),
        grid=(bsz, nt, nj),
        in_specs=[pl.BlockSpec((HALO, d), lambda b, i, j: (jnp.maximum((b * nt + i) * hblk - 1, 0), 0)),
                  pl.BlockSpec((tm, d), lambda b, i, j: (b * nt + i, 0)),
                  pl.BlockSpec((HALO, d), lambda b, i, j: (jnp.minimum((b * nt + i + 1) * hblk, last), 0)),
                  pl.BlockSpec((None, d, tn), lambda b, i, j: (layer, 0, j)),
                  pl.BlockSpec((None, d, tn), lambda b, i, j: (layer, 0, nj + j)),
                  pl.BlockSpec((None, 3, tn), lambda b, i, j: (layer, 0, j)),
                  pl.BlockSpec((None, 1, tn), lambda b, i, j: (layer, 0, j))],
        out_specs=pl.BlockSpec((tm, tn), lambda b, i, j: (b * nt + i, j)),
        out_shape=jax.ShapeDtypeStruct((r, dff), bf16),
        scratch_shapes=[pltpu.VMEM((tm + 2 * HALO, d), bf16)],
        compiler_params=_params(("parallel", "parallel", "arbitrary"), 56),
        name="ffn_up_conv",
    )(hb, hb, hb, w_up, w_up, conv_w, conv_b)


FFN_DOWN_SUBTILES = 2


def _ffn_down_kernel(a_ref, w_ref, h_ref, g_ref, b_ref, o_ref, ob_ref, *, alpha):
    ts = h_ref.shape[0] // FFN_DOWN_SUBTILES
    for sub in range(FFN_DOWN_SUBTILES):
        rows = slice(sub * ts, (sub + 1) * ts)
        y = _layer_norm(alpha * h_ref[rows, :] + _dot(a_ref[rows, :], w_ref[...]), g_ref[...], b_ref[...])
        o_ref[rows, :] = y
        ob_ref[rows, :] = y.astype(bf16)


def _ffn_down(act, w_down, h, ln_g, ln_b, layer, alpha):
    r, d = h.shape
    dff = act.shape[1]
    tm = _pick(r, (384, 128))
    vec = pl.BlockSpec((None, 1, d), lambda i: (layer, 0, 0))
    return pl.pallas_call(
        functools.partial(_ffn_down_kernel, alpha=alpha),
        grid=(r // tm,),
        in_specs=[pl.BlockSpec((tm, dff), lambda i: (i, 0)),
                  pl.BlockSpec((None, dff, d), lambda i: (layer, 0, 0), pipeline_mode=pl.Buffered(1)),
                  pl.BlockSpec((tm, d), lambda i: (i, 0)), vec, vec],
        out_specs=[pl.BlockSpec((tm, d), lambda i: (i, 0)), pl.BlockSpec((tm, d), lambda i: (i, 0))],
        out_shape=[jax.ShapeDtypeStruct((r, d), f32), jax.ShapeDtypeStruct((r, d), bf16)],
        compiler_params=_params(("parallel",), 60),
        name="ffn_down_ln",
    )(act, w_down, h, ln_g, ln_b)


def _rope_tables(lp):
    inv_freq = ROPE_THETA ** (-jnp.arange(0, QK_ROPE, 2, dtype=f32) / QK_ROPE)
    pos = jnp.maximum(jnp.arange(lp, dtype=f32) - FRONT, 0.0)
    ang = pos[:, None] * inv_freq[None, :]
    cos, sin = jnp.cos(ang), jnp.sin(ang)
    zero = jnp.zeros_like(sin)
    reps = LANE // QK_ROPE
    cos_t = jnp.tile(jnp.concatenate([cos, cos], axis=1), (1, reps))
    sin_a = jnp.tile(jnp.concatenate([-sin, zero], axis=1), (1, reps))
    sin_b = jnp.tile(jnp.concatenate([zero, sin], axis=1), (1, reps))
    return cos_t, sin_a, sin_b


def kernel(x_prompt, x_sample, meta_tokens, w_in, b_gates, ml_norm_g, q_norm_g, kv_norm_g, w_uq, w_ukv,
           w_out, ln1_g, ln1_b, w_up, conv_w, conv_b, w_down, ln2_g, ln2_b):
    depth, d, _ = w_in.shape
    assert x_prompt.shape[1:] == x_sample.shape[1:]
    n_prompt = x_prompt.shape[0]
    x = jnp.concatenate([x_prompt, x_sample], axis=0)
    bsz, seq, _ = x.shape
    assert seq % LANE == 0
    lp = LANE + seq
    alpha = (2 * depth) ** 0.25

    gate_lo = 4 * ML_WIDTH
    w_main = jnp.concatenate(
        [w_in[..., :ML_WIDTH], w_in[..., 2 * ML_WIDTH:gate_lo], w_in[..., gate_lo + N_GATES:], w_in[..., -QK_ROPE:],
         jnp.zeros((depth, d, N_MAIN - COL_KR - 2 * QK_ROPE), w_in.dtype)], axis=-1).astype(bf16)
    wk_t = jnp.swapaxes(w_in[..., ML_WIDTH:2 * ML_WIDTH], 1, 2).astype(bf16)
    h4 = ML_HEADS
    perm = jnp.array(list(range(0, h4)) + list(range(2 * h4, 3 * h4))
                     + list(range(h4, 2 * h4)) + list(range(3 * h4, 4 * h4)))
    wg_t = jnp.swapaxes(w_in[..., gate_lo:gate_lo + N_GATES], 1, 2)[:, perm, :].astype(bf16)
    bg = b_gates[:, perm, None].astype(f32)
    uq = w_uq.reshape(depth, Q_LORA, MLA_HEADS, QK_NOPE + QK_ROPE)
    w_uq_p = jnp.concatenate([uq[..., :QK_NOPE].reshape(depth, Q_LORA, -1),
                              uq[..., QK_NOPE:].reshape(depth, Q_LORA, -1)], axis=-1).astype(bf16)
    ukv = w_ukv.reshape(depth, KV_LORA, MLA_HEADS, QK_NOPE + V_HEAD)
    w_ukv_p = jnp.concatenate([ukv[..., :QK_NOPE].reshape(depth, KV_LORA, -1),
                               ukv[..., QK_NOPE:].reshape(depth, KV_LORA, -1)], axis=-1).astype(bf16)
    w_out_b, w_up_b, w_down_b = w_out.astype(bf16), w_up.astype(bf16), w_down.astype(bf16)
    vec3 = lambda a: a[:, None, :].astype(f32)
    ml_g, q_g, kv_g = vec3(ml_norm_g), vec3(q_norm_g), vec3(kv_norm_g)
    l1g, l1b, l2g, l2b, cb = vec3(ln1_g), vec3(ln1_b), vec3(ln2_g), vec3(ln2_b), vec3(conv_b)
    cw = conv_w.astype(f32)
    tables = _rope_tables(lp)

    meta = jnp.broadcast_to(meta_tokens.astype(x.dtype)[None], (bsz, N_META, d))
    h = jnp.concatenate([jnp.zeros((bsz, FRONT, d), x.dtype), meta, x], axis=1).reshape(bsz * lp, d)
    hb = h.astype(bf16)
    for layer in range(depth):
        u = _inproj(hb, w_main, layer)
        gq = _gates(hb, wg_t, bg, layer, bsz, lp)
        kt = _keys_t(hb, wk_t, layer)
        ml = _mlstm(u, kt, gq, ml_g, layer, bsz, lp)
        qc, kc, v = _mla_prep(u, w_uq_p, w_ukv_p, q_g, kv_g, tables, layer, bsz, lp)
        at = _attention(qc, kc, v, bsz, lp)
        h, hb = _outproj(ml, at, w_out_b, h, l1g, l1b, layer, alpha)
        act = _ffn_up(hb, w_up_b, cw, cb, layer, bsz, lp)
        h, hb = _ffn_down(act, w_down_b, h, l2g, l2b, layer, alpha)
    y = h.reshape(bsz, lp, d)[:, LANE:]
    return (y[:n_prompt], y[n_prompt:])
```

```python
import functools
import math

import jax
import jax.numpy as jnp
from jax import lax
from jax.experimental import pallas as pl
from jax.experimental.pallas import tpu as pltpu

N_META = 16
ML_HEADS = 4
ML_HEAD_DIM = 256
ML_WIDTH = ML_HEADS * ML_HEAD_DIM
N_GATES = 4 * ML_HEADS
MLA_HEADS = 8
Q_LORA = 512
KV_LORA = 256
QK_NOPE = 128
QK_ROPE = 64
V_HEAD = 128
MLA_WIDTH = MLA_HEADS * V_HEAD
ROPE_THETA = 10000.0
EPS = 1e-5
NEG = -1e30

LANE = 128
FRONT = LANE - N_META
CHUNK = 128
ROPE_HALF = QK_ROPE // 2
SOFTMAX_SCALE_LOG2 = (QK_NOPE + QK_ROPE) ** -0.5 * math.log2(math.e)
MIB = 1024 * 1024

COL_Q, COL_V, COL_O = 0, ML_WIDTH, 2 * ML_WIDTH
COL_DQ = 3 * ML_WIDTH
COL_DKV = COL_DQ + Q_LORA
COL_KR = COL_DKV + KV_LORA
N_MAIN = COL_KR + 2 * QK_ROPE + LANE

f32 = jnp.float32
bf16 = jnp.bfloat16


def _pick(n, cands):
    for c in cands:
        if n % c == 0:
            return c
    raise ValueError(f"no tile for {n} in {cands}")


def _params(sem, vmem_mib):
    return pltpu.CompilerParams(dimension_semantics=sem, vmem_limit_bytes=vmem_mib * MIB)


def _dot(a, b):
    return jnp.dot(a, b, preferred_element_type=f32)


def _dot_nt(a, b):
    return lax.dot_general(a, b, (((1,), (1,)), ((), ())), preferred_element_type=f32)


def _inproj_kernel(a_ref, w_ref, o_ref):
    o_ref[...] = _dot(a_ref[...], w_ref[...]).astype(o_ref.dtype)


def _keys_t_kernel(w_ref, h_ref, o_ref):
    o_ref[...] = (_dot_nt(w_ref[...], h_ref[...]) * ML_HEAD_DIM ** -0.5).astype(o_ref.dtype)


def _keys_t(hb, wk_t, layer):
    r, d = hb.shape
    tm = _pick(r, (1536, 384, 128))
    return pl.pallas_call(
        _keys_t_kernel,
        grid=(r // tm,),
        in_specs=[pl.BlockSpec((None, ML_WIDTH, d), lambda i: (layer, 0, 0)),
                  pl.BlockSpec((tm, d), lambda i: (i, 0))],
        out_specs=pl.BlockSpec((ML_WIDTH, tm), lambda i: (0, i)),
        out_shape=jax.ShapeDtypeStruct((ML_WIDTH, r), bf16),
        compiler_params=_params(("parallel",), 48),
        name="keys_t",
    )(wk_t, hb)


def _inproj(hb, w_main, layer):
    r, d = hb.shape
    n = w_main.shape[-1]
    tm = _pick(r, (1408, 384, 128))
    tn = _pick(n, (1024, 512, 256, 128))
    return pl.pallas_call(
        _inproj_kernel,
        grid=(r // tm, n // tn),
        in_specs=[pl.BlockSpec((tm, d), lambda i, j: (i, 0)),
                  pl.BlockSpec((None, d, tn), lambda i, j: (layer, 0, j))],
        out_specs=pl.BlockSpec((tm, tn), lambda i, j: (i, j)),
        out_shape=jax.ShapeDtypeStruct((r, n), bf16),
        compiler_params=_params(("parallel", "arbitrary"), 48),
        name="inproj",
    )(hb, w_main)


def _gates_kernel(h_ref, wg_ref, bg_ref, o_ref):
    tm = h_ref.shape[0]
    half = N_GATES // 2
    g = _dot_nt(wg_ref[...], h_ref[...]) + bg_ref[...]
    pos = pl.program_id(1) * tm + lax.broadcasted_iota(jnp.int32, (half, tm), 1)
    unused = pos < FRONT
    gi, gf = g[:half], g[half:]
    li = jnp.where(unused, NEG, gi)
    lf = jnp.where(unused, 0.0, jnp.minimum(gf, 0.0) - jnp.log1p(jnp.exp(-jnp.abs(gf))))
    lane = lax.broadcasted_iota(jnp.int32, (half, CHUNK), 1)
    is_fwd = lax.broadcasted_iota(jnp.int32, (half, CHUNK), 0) < ML_HEADS
    for c in range(tm // CHUNK):
        sl = slice(c * CHUNK, (c + 1) * CHUNK)
        pre = suf = lf[:, sl]
        k = 1
        while k < CHUNK:
            pre = pre + jnp.where(lane >= k, pltpu.roll(pre, k, 1), 0.0)
            suf = suf + jnp.where(lane < CHUNK - k, pltpu.roll(suf, CHUNK - k, 1), 0.0)
            k *= 2
        b = jnp.where(is_fwd, pre, suf)
        o_ref[:half, sl] = li[:, sl] - b
        o_ref[half:, sl] = b


def _gates(hb, wg_t, bg, layer, bsz, lp):
    r, d = hb.shape
    tm = _pick(lp, (1408, 384, 128))
    nt = lp // tm
    return pl.pallas_call(
        _gates_kernel,
        grid=(bsz, nt),
        in_specs=[pl.BlockSpec((tm, d), lambda b, i: (b * nt + i, 0)),
                  pl.BlockSpec((None, N_GATES, d), lambda b, i: (layer, 0, 0)),
                  pl.BlockSpec((None, N_GATES, 1), lambda b, i: (layer, 0, 0))],
        out_specs=pl.BlockSpec((None, N_GATES, tm), lambda b, i: (b, 0, i)),
        out_shape=jax.ShapeDtypeStruct((bsz, N_GATES, lp), f32),
        compiler_params=_params(("parallel", "arbitrary"), 32),
        name="gates",
    )(hb, wg_t, bg)


MLSTM_UNROLL = 8


def _mlstm_kernel(q_ref, kt_ref, v_ref, o_ref, gq_ref, ng_ref, out_ref, hf_scr, hb_scr, c_scr, m_scr, s_scr):
    head = pl.program_id(1)
    lp, dh = q_ref.shape
    t = CHUNK
    nc = lp // t
    c_scr[...] = jnp.zeros_like(c_scr)
    m_scr[...] = jnp.zeros_like(m_scr)
    ti = lax.broadcasted_iota(jnp.int32, (t, t), 0)
    si = lax.broadcasted_iota(jnp.int32, (t, t), 1)
    masks = (si <= ti, si >= ti)
    eye = si == ti
    half = N_GATES // 2
    gate_row = lax.broadcasted_iota(jnp.int32, (half, t), 0)
    ones_col = jnp.where(lax.broadcasted_iota(jnp.int32, (t, LANE), 1) == 0, 1.0, 0.0).astype(bf16)

    def qk_ahead(c, dirn):
        base = pl.multiple_of(jnp.clip(c, 0, nc - 1) * t, t)
        s_scr[dirn] = _dot(q_ref[pl.ds(base, t), :], kt_ref[:, pl.ds(base, t)])

    def chunk(c, dirn, h_scr):
        base = pl.multiple_of(c * t, t)
        q = q_ref[pl.ds(base, t), :]
        kt = kt_ref[:, pl.ds(base, t)]
        v = jnp.concatenate([v_ref[pl.ds(base, t), :], ones_col], axis=1)
        mine = gate_row == head + ML_HEADS * dirn
        r_row = jnp.sum(jnp.where(mine, gq_ref[:half, pl.ds(base, t)], 0.0), axis=0, keepdims=True)
        b_row = jnp.sum(jnp.where(mine, gq_ref[half:, pl.ds(base, t)], 0.0), axis=0, keepdims=True)
        m_state = m_scr[dirn, 0:1, 0:1]
        mask = masks[dirn]
        m_col = jnp.maximum(jnp.max(jnp.where(mask, r_row, -jnp.inf), axis=1, keepdims=True), m_state)
        b_col = jnp.sum(jnp.where(eye, b_row, 0.0), axis=1, keepdims=True)
        m_all = jnp.maximum(jnp.max(r_row, axis=1, keepdims=True), m_state)
        b_last = b_row[:, 0:1] if dirn else b_row[:, t - 1:t]
        s = s_scr[dirn] * jnp.exp(jnp.where(mask, r_row - m_col, -jnp.inf))
        both = jnp.exp(m_state - m_col) * _dot(q, c_scr[dirn].astype(bf16)) + _dot(s.astype(bf16), v)
        floor = jnp.exp(-(b_col + m_col))
        h_scr[pl.ds(base, t), :] = both[:, :dh] / jnp.maximum(jnp.abs(both[:, dh:dh + 1]), floor)
        ktw = (kt.astype(f32) * jnp.exp(r_row - m_all)).astype(bf16)
        c_scr[dirn] = jnp.exp(m_state - m_all) * c_scr[dirn] + _dot(ktw, v)
        m_scr[dirn] = jnp.broadcast_to(b_last + m_all, m_scr.shape[1:])

    def finish(c):
        base = pl.multiple_of(c * t, t)
        hs = hf_scr[pl.ds(base, t), :] + hb_scr[pl.ds(base, t), :]
        mu = jnp.mean(hs, axis=1, keepdims=True)
        var = jnp.mean(jnp.square(hs - mu), axis=1, keepdims=True)
        hn = (hs - mu) * lax.rsqrt(var + EPS) * ng_ref[...]
        gate = jax.nn.sigmoid(o_ref[pl.ds(base, t), :].astype(f32))
        out_ref[pl.ds(base, t), :] = (gate * hn).astype(out_ref.dtype)

    def scan_step(c):
        chunk(c, 0, hf_scr)
        chunk(nc - 1 - c, 1, hb_scr)
        qk_ahead(c + 1, 0)
        qk_ahead(nc - 2 - c, 1)

    def first_half(c, carry):
        scan_step(c)
        return carry

    def second_half(c, carry):
        finish(c - 1)
        finish(nc - c)
        scan_step(c)
        return carry

    qk_ahead(0, 0)
    qk_ahead(nc - 1, 1)
    meet = (nc + 1) // 2
    lax.fori_loop(0, meet, first_half, 0, unroll=MLSTM_UNROLL)
    lax.fori_loop(meet, nc, second_half, 0, unroll=MLSTM_UNROLL)
    finish(nc - 1)
    finish(0)


def _mlstm(u, kt, gq, ml_norm_g, layer, bsz, lp):
    r = u.shape[0]
    dh = ML_HEAD_DIM

    def col(first):
        return pl.BlockSpec((lp, dh), lambda b, h: (b, first // dh + h))

    return pl.pallas_call(
        _mlstm_kernel,
        grid=(bsz, ML_HEADS),
        in_specs=[col(COL_Q), pl.BlockSpec((dh, lp), lambda b, h: (h, b)), col(COL_V), col(COL_O),
                  pl.BlockSpec((None, N_GATES, lp), lambda b, h: (b, 0, 0)),
                  pl.BlockSpec((None, 1, dh), lambda b, h: (layer, 0, h))],
        out_specs=pl.BlockSpec((lp, dh), lambda b, h: (b, h)),
        out_shape=jax.ShapeDtypeStruct((r, ML_WIDTH), bf16),
        scratch_shapes=[pltpu.VMEM((lp, dh), f32), pltpu.VMEM((lp, dh), f32),
                        pltpu.VMEM((2, dh, dh + LANE), f32), pltpu.VMEM((2, 8, LANE), f32),
                        pltpu.VMEM((2, CHUNK, CHUNK), f32)],
        compiler_params=_params(("parallel", "arbitrary"), 48),
        name="mlstm",
    )(u, kt, u, u, gq, ml_norm_g)


def _rms(x, g):
    return x * lax.rsqrt(jnp.mean(jnp.square(x), axis=-1, keepdims=True) + EPS) * g


def _rope(x, cos, sin_a, sin_b):
    n = x.shape[1]
    reps = n // LANE
    if reps > 1:
        cos, sin_a, sin_b = (jnp.concatenate([a] * reps, axis=1) for a in (cos, sin_a, sin_b))
    return x * cos + pltpu.roll(x, n - ROPE_HALF, 1) * sin_a + pltpu.roll(x, ROPE_HALF, 1) * sin_b


def _mla_prep_kernel(dq_ref, dkv_ref, kr_ref, wuq_ref, wukv_ref, qg_ref, kvg_ref,
                     cos_ref, sa_ref, sb_ref, qc_ref, kc_ref, v_ref):
    cos, sa, sb = cos_ref[...], sa_ref[...], sb_ref[...]
    q = _dot(_rms(dq_ref[...].astype(f32), qg_ref[...]).astype(bf16), wuq_ref[...]) * SOFTMAX_SCALE_LOG2
    kv =_dot(_rms(dkv_ref[...].astype(f32), kvg_ref[...]).astype(bf16), wukv_ref[...])
    q_rope = _rope(q[:, MLA_HEADS * QK_NOPE:], cos, sa, sb)
    k_rope = _rope(kr_ref[...].astype(f32), cos, sa, sb)
    tm = q.shape[0]
    lane = lax.broadcasted_iota(jnp.int32, (tm, LANE), 1)
    pos = pl.program_id(1) * tm + lax.broadcasted_iota(jnp.int32, (tm, LANE), 0)
    low = lane < QK_ROPE
    marker = lane == QK_ROPE
    one_hot = jnp.where(marker, 1.0, 0.0)
    k_tail = jnp.where(low, k_rope, jnp.where(marker & (pos < FRONT), -jnp.inf, 0.0)).astype(bf16)
    ones_col = jnp.where(lane == 0, 1.0, 0.0).astype(bf16)
    for h in range(MLA_HEADS):
        pair = q_rope[:, (h // 2) * LANE:(h // 2 + 1) * LANE]
        if h % 2:
            pair = pltpu.roll(pair, QK_ROPE, 1)
        qc_ref[h, :, :QK_NOPE] = q[:, h * QK_NOPE:(h + 1) * QK_NOPE].astype(bf16)
        qc_ref[h, :, QK_NOPE:] = jnp.where(low, pair, one_hot).astype(bf16)
        kc_ref[h, :, :QK_NOPE] = kv[:, h * QK_NOPE:(h + 1) * QK_NOPE].astype(bf16)
        kc_ref[h, :, QK_NOPE:] = k_tail
        v_ref[h, :, :V_HEAD] = kv[:, (MLA_HEADS + h) * V_HEAD:(MLA_HEADS + h + 1) * V_HEAD].astype(bf16)
        v_ref[h, :, V_HEAD:] = ones_col


def _mla_prep(u, w_uq_p, w_ukv_p, q_norm_g, kv_norm_g, tables, layer, bsz, lp):
    r = u.shape[0]
    tm = _pick(lp, (384, 128))
    nt = lp // tm
    kd = QK_NOPE + LANE
    tab = pl.BlockSpec((tm, LANE), lambda b, i: (i, 0))
    row = lambda b, i: b * nt + i
    return pl.pallas_call(
        _mla_prep_kernel,
        grid=(bsz, nt),
        in_specs=[pl.BlockSpec((tm, Q_LORA), lambda b, i: (row(b, i), COL_DQ // Q_LORA)),
                  pl.BlockSpec((tm, KV_LORA), lambda b, i: (row(b, i), COL_DKV // KV_LORA)),
                  pl.BlockSpec((tm, LANE), lambda b, i: (row(b, i), COL_KR // LANE)),
                  pl.BlockSpec((None,) + w_uq_p.shape[1:], lambda b, i: (layer, 0, 0)),
                  pl.BlockSpec((None,) + w_ukv_p.shape[1:], lambda b, i: (layer, 0, 0)),
                  pl.BlockSpec((None, 1, Q_LORA), lambda b, i: (layer, 0, 0)),
                  pl.BlockSpec((None, 1, KV_LORA), lambda b, i: (layer, 0, 0)),
                  tab, tab, tab],
        out_specs=[pl.BlockSpec((MLA_HEADS, tm, kd), lambda b, i: (0, row(b, i), 0)),
                   pl.BlockSpec((MLA_HEADS, tm, kd), lambda b, i: (0, row(b, i), 0)),
                   pl.BlockSpec((MLA_HEADS, tm, 2 * V_HEAD), lambda b, i: (0, row(b, i), 0))],
        out_shape=[jax.ShapeDtypeStruct((MLA_HEADS, r, kd), bf16),
                   jax.ShapeDtypeStruct((MLA_HEADS, r, kd), bf16),
                   jax.ShapeDtypeStruct((MLA_HEADS, r, 2 * V_HEAD), bf16)],
        compiler_params=_params(("parallel", "arbitrary"), 48),
        name="mla_prep",
    )(u, u, u, w_uq_p, w_ukv_p, q_norm_g, kv_norm_g, *tables)


ATTN_HEADS_PER_STEP = 2
ATTN_TILES_PER_BODY = 8


def _attn_kernel(q_ref, k_ref, v_ref, o_ref, s0_ref, s1_ref, p0_ref, p1_ref, *, tq):
    nh, lp, _ = k_ref.shape
    nq = lp // tq
    n_tiles = nh * nq
    s_refs, p_refs = (s0_ref, s1_ref), (p0_ref, p1_ref)

    @pl.when((pl.program_id(0) == 0) & (pl.program_id(1) == 0))
    def _():
        for ref in s_refs + p_refs:
            ref[...] = jnp.zeros_like(ref)

    def tile(x):
        x = jnp.clip(x, 0, n_tiles - 1)
        hh = x // nq
        return hh, pl.ds(pl.multiple_of((x - hh * nq) * tq, tq), tq)

    def scores(x, slot):
        hh, rows = tile(x)
        s_refs[slot][...] = _dot_nt(q_ref[hh, rows, :], k_ref[hh])

    def softmax(slot):
        s = s_refs[slot][...]
        p_refs[slot][...] = jnp.exp2(s - jnp.max(s, axis=1, keepdims=True)).astype(bf16)

    def weighted(x, slot):
        hh, rows = tile(x)
        acc = _dot(p_refs[slot][...], v_ref[hh])
        o_ref[hh, rows, :] = (acc[:, :V_HEAD] * (1.0 / acc[:, V_HEAD:V_HEAD + 1])).astype(o_ref.dtype)

    n_bodies = -(-(n_tiles + 2) // ATTN_TILES_PER_BODY)
    start = n_tiles + 2 - ATTN_TILES_PER_BODY * n_bodies

    def body(u, carry):
        for e in range(ATTN_TILES_PER_BODY):
            x = ATTN_TILES_PER_BODY * u + e + start
            weighted(x - 2, e % 2)
            softmax((e + 1) % 2)
            scores(x, e % 2)
        return carry

    lax.fori_loop(0, n_bodies, body, 0)


def _attention(qc, kc, v, bsz, lp):
    r = v.shape[1]
    tq = _pick(lp, (384, 128))
    kd = qc.shape[-1]
    nh = ATTN_HEADS_PER_STEP
    blk = lambda width: pl.BlockSpec((nh, lp, width), lambda b, h: (h, b, 0))
    return pl.pallas_call(
        functools.partial(_attn_kernel, tq=tq),
        grid=(bsz, MLA_HEADS // nh),
        in_specs=[blk(kd), blk(kd), blk(2 * V_HEAD)],
        out_specs=blk(V_HEAD),
        out_shape=jax.ShapeDtypeStruct((MLA_HEADS, r, V_HEAD), bf16),
        scratch_shapes=[pltpu.VMEM((tq, lp), f32), pltpu.VMEM((tq, lp), f32),
                        pltpu.VMEM((tq, lp), bf16), pltpu.VMEM((tq, lp), bf16)],
        compiler_params=_params(("arbitrary", "arbitrary"), 60),
        name="attention",
    )(qc, kc, v)


def _layer_norm(y, g, b):
    mu = jnp.mean(y, axis=-1, keepdims=True)
    var = jnp.mean(jnp.square(y - mu), axis=-1, keepdims=True)
    return (y - mu) * lax.rsqrt(var + EPS) * g + b


OUTPROJ_SUBTILES = 2


def _outproj_kernel(ml_ref, at_ref, w_ref, h_ref, g_ref, b_ref, o_ref, ob_ref, *, alpha):
    ts = h_ref.shape[0] // OUTPROJ_SUBTILES
    for sub in range(OUTPROJ_SUBTILES):
        rows = slice(sub * ts, (sub + 1) * ts)
        at = jnp.concatenate([at_ref[h, rows, :] for h in range(MLA_HEADS)], axis=1)
        mix = _dot(ml_ref[rows, :], w_ref[:ML_WIDTH, :]) + _dot(at, w_ref[ML_WIDTH:, :])
        y = _layer_norm(alpha * h_ref[rows, :] + mix, g_ref[...], b_ref[...])
        o_ref[rows, :] = y
        ob_ref[rows, :] = y.astype(bf16)


def _outproj(ml, at, w_out, h, ln_g, ln_b, layer, alpha):
    r, d = h.shape
    tm = _pick(r, (768, 384, 128))
    vec = pl.BlockSpec((None, 1, d), lambda i: (layer, 0, 0))
    return pl.pallas_call(
        functools.partial(_outproj_kernel, alpha=alpha),
        grid=(r // tm,),
        in_specs=[pl.BlockSpec((tm, ML_WIDTH), lambda i: (i, 0)),
                  pl.BlockSpec((MLA_HEADS, tm, V_HEAD), lambda i: (0, i, 0)),
                  pl.BlockSpec((None,) + w_out.shape[1:], lambda i: (layer, 0, 0), pipeline_mode=pl.Buffered(1)),
                  pl.BlockSpec((tm, d), lambda i: (i, 0)), vec, vec],
        out_specs=[pl.BlockSpec((tm, d), lambda i: (i, 0)), pl.BlockSpec((tm, d), lambda i: (i, 0))],
        out_shape=[jax.ShapeDtypeStruct((r, d), f32), jax.ShapeDtypeStruct((r, d), bf16)],
        compiler_params=_params(("parallel",), 56),
        name="outproj_ln",
    )(ml, at, w_out, h, ln_g, ln_b)


HALO = 16


def _ffn_up_kernel(prev_ref, main_ref, next_ref, wg_ref, wv_ref, cw_ref, cb_ref, o_ref, ext_scr, gate_scr, *, lp):
    tm = main_ref.shape[0]

    @pl.when(pl.program_id(2) == 0)
    def _():
        ext_scr[:HALO, :] = prev_ref[...]
        ext_scr[HALO:HALO + tm, :] = main_ref[...]
        ext_scr[HALO + tm:, :] = next_ref[...]

    rows = tm + 2 * HALO
    pos = pl.program_id(1) * tm - HALO + lax.broadcasted_iota(jnp.int32, (rows, 1), 0)
    inside = (pos >= FRONT) & (pos < lp)
    gate_scr[...] = jnp.where(inside, _dot(ext_scr[...], wg_ref[...]), 0.0)
    taps = [gate_scr[pl.ds(HALO - 1 + j, tm), :] * cw_ref[j:j + 1, :] for j in range(3)]
    conv = taps[0] + taps[1] + taps[2] + cb_ref[...]
    val = _dot(main_ref[...], wv_ref[...])
    o_ref[...] = (conv * jax.nn.sigmoid(conv) * val).astype(o_ref.dtype)


def _ffn_up(hb, w_up, conv_w, conv_b, layer, bsz, lp):
    r, d = hb.shape
    dff = conv_b.shape[-1]
    tm = _pick(lp, (1408, 384, 128))
    tn = _pick(dff, (512, 256, 128))
    nt, nj = lp // tm, dff // tn
    hblk = tm // HALO
    last = r // HALO - 1
    return pl.pallas_call(
        functools.partial(_ffn_up_kernel, lp=lp),
        grid=(bsz, nt, nj),
        in_specs=[pl.BlockSpec((HALO, d), lambda b, i, j: (jnp.maximum((b * nt + i) * hblk - 1, 0), 0)),
                  pl.BlockSpec((tm, d), lambda b, i, j: (b * nt + i, 0)),
                  pl.BlockSpec((HALO, d), lambda b, i, j: (jnp.minimum((b * nt + i + 1) * hblk, last), 0)),
                  pl.BlockSpec((None, d, tn), lambda b, i, j: (layer, 0, j)),
                  pl.BlockSpec((None, d, tn), lambda b, i, j: (layer, 0, nj + j)),
                  pl.BlockSpec((None, 3, tn), lambda b, i, j: (layer, 0, j)),
                  pl.BlockSpec((None, 1, tn), lambda b, i, j: (layer, 0, j))],
        out_specs=pl.BlockSpec((tm, tn), lambda b, i, j: (b * nt + i, j)),
        out_shape=jax.ShapeDtypeStruct((r, dff), bf16),
        scratch_shapes=[pltpu.VMEM((tm + 2 * HALO, d), bf16), pltpu.VMEM((tm + 2 * HALO, tn), f32)],
        compiler_params=_params(("parallel", "parallel", "arbitrary"), 56),
        name="ffn_up_conv",
    )(hb, hb, hb, w_up, w_up, conv_w, conv_b)


FFN_DOWN_SUBTILES = 2


def _ffn_down_kernel(a_ref, w_ref, h_ref, g_ref, b_ref, o_ref, ob_ref, *, alpha):
    ts = h_ref.shape[0] // FFN_DOWN_SUBTILES
    for sub in range(FFN_DOWN_SUBTILES):
        rows = slice(sub * ts, (sub + 1) * ts)
        y = _layer_norm(alpha * h_ref[rows, :] + _dot(a_ref[rows, :], w_ref[...]), g_ref[...], b_ref[...])
        o_ref[rows, :] = y
        ob_ref[rows, :] = y.astype(bf16)


def _ffn_down(act, w_down, h, ln_g, ln_b, layer, alpha):
    r, d = h.shape
    dff = act.shape[1]
    tm = _pick(r, (384, 128))
    vec = pl.BlockSpec((None, 1, d), lambda i: (layer, 0, 0))
    return pl.pallas_call(
        functools.partial(_ffn_down_kernel, alpha=alpha),
        grid=(r // tm,),
        in_specs=[pl.BlockSpec((tm, dff), lambda i: (i, 0)),
                  pl.BlockSpec((None, dff, d), lambda i: (layer, 0, 0), pipeline_mode=pl.Buffered(1)),
                  pl.BlockSpec((tm, d), lambda i: (i, 0)), vec, vec],
        out_specs=[pl.BlockSpec((tm, d), lambda i: (i, 0)), pl.BlockSpec((tm, d), lambda i: (i, 0))],
        out_shape=[jax.ShapeDtypeStruct((r, d), f32), jax.ShapeDtypeStruct((r, d), bf16)],
        compiler_params=_params(("parallel",), 60),
        name="ffn_down_ln",
    )(act, w_down, h, ln_g, ln_b)


def _rope_tables(lp):
    inv_freq = ROPE_THETA ** (-jnp.arange(0, QK_ROPE, 2, dtype=f32) / QK_ROPE)
    pos = jnp.maximum(jnp.arange(lp, dtype=f32) - FRONT, 0.0)
    ang = pos[:, None] * inv_freq[None, :]
    cos, sin = jnp.cos(ang), jnp.sin(ang)
    zero = jnp.zeros_like(sin)
    reps = LANE // QK_ROPE
    cos_t = jnp.tile(jnp.concatenate([cos, cos], axis=1), (1, reps))
    sin_a = jnp.tile(jnp.concatenate([-sin, zero], axis=1), (1, reps))
    sin_b = jnp.tile(jnp.concatenate([zero, sin], axis=1), (1, reps))
    return cos_t, sin_a, sin_b


def kernel(x_prompt, x_sample, meta_tokens, w_in, b_gates, ml_norm_g, q_norm_g, kv_norm_g, w_uq, w_ukv,
           w_out, ln1_g, ln1_b, w_up, conv_w, conv_b, w_down, ln2_g, ln2_b):
    depth, d, _ = w_in.shape
    assert x_prompt.shape[1:] == x_sample.shape[1:]
    n_prompt = x_prompt.shape[0]
    x = jnp.concatenate([x_prompt, x_sample], axis=0)
    bsz, seq, _ = x.shape
    assert seq % LANE == 0
    lp = LANE + seq
    alpha = (2 * depth) ** 0.25

    gate_lo = 4 * ML_WIDTH
    w_main = jnp.concatenate(
        [w_in[..., :ML_WIDTH], w_in[..., 2 * ML_WIDTH:gate_lo], w_in[..., gate_lo + N_GATES:], w_in[..., -QK_ROPE:],
         jnp.zeros((depth, d, N_MAIN - COL_KR - 2 * QK_ROPE), w_in.dtype)], axis=-1).astype(bf16)
    wk_t = jnp.swapaxes(w_in[..., ML_WIDTH:2 * ML_WIDTH], 1, 2).astype(bf16)
    h4 = ML_HEADS
    perm = jnp.array(list(range(0, h4)) + list(range(2 * h4, 3 * h4))
                     + list(range(h4, 2 * h4)) + list(range(3 * h4, 4 * h4)))
    wg_t = jnp.swapaxes(w_in[..., gate_lo:gate_lo + N_GATES], 1, 2)[:, perm, :].astype(bf16)
    bg = b_gates[:, perm, None].astype(f32)
    uq = w_uq.reshape(depth, Q_LORA, MLA_HEADS, QK_NOPE + QK_ROPE)
    w_uq_p = jnp.concatenate([uq[..., :QK_NOPE].reshape(depth, Q_LORA, -1),
                              uq[..., QK_NOPE:].reshape(depth, Q_LORA, -1)], axis=-1).astype(bf16)
    ukv = w_ukv.reshape(depth, KV_LORA, MLA_HEADS, QK_NOPE + V_HEAD)
    w_ukv_p = jnp.concatenate([ukv[..., :QK_NOPE].reshape(depth, KV_LORA, -1),
                               ukv[..., QK_NOPE:].reshape(depth, KV_LORA, -1)], axis=-1).astype(bf16)
    w_out_b, w_up_b, w_down_b = w_out.astype(bf16), w_up.astype(bf16), w_down.astype(bf16)
    vec3 = lambda a: a[:, None, :].astype(f32)
    ml_g, q_g, kv_g = vec3(ml_norm_g), vec3(q_norm_g), vec3(kv_norm_g)
    l1g, l1b, l2g, l2b, cb = vec3(ln1_g), vec3(ln1_b), vec3(ln2_g), vec3(ln2_b), vec3(conv_b)
    cw = conv_w.astype(f32)
    tables = _rope_tables(lp)

    meta = jnp.broadcast_to(meta_tokens.astype(x.dtype)[None], (bsz, N_META, d))
    h = jnp.concatenate([jnp.zeros((bsz, FRONT, d), x.dtype), meta, x], axis=1).reshape(bsz * lp, d)
    hb = h.astype(bf16)
    for layer in range(depth):
        u = _inproj(hb, w_main, layer)
        gq = _gates(hb, wg_t, bg, layer, bsz, lp)
        kt = _keys_t(hb, wk_t, layer)
        ml = _mlstm(u, kt, gq, ml_g, layer, bsz, lp)
        qc, kc, v = _mla_prep(u, w_uq_p, w_ukv_p, q_g, kv_g, tables, layer, bsz, lp)
        at = _attention(qc, kc, v, bsz, lp)
        h, hb = _outproj(ml, at, w_out_b, h, l1g, l1b, layer, alpha)
        act = _ffn_up(hb, w_up_b, cw, cb, layer, bsz, lp)
        h, hb = _ffn_down(act, w_down_b, h, l2g, l2b, layer, alpha)
    y = h.reshape(bsz, lp, d)[:, LANE:]
    return (y[:n_prompt], y[n_prompt:])
```

```python
import functools
import math

import jax
import jax.numpy as jnp
from jax import lax
from jax.experimental import pallas as pl
from jax.experimental.pallas import tpu as pltpu

N_META = 16
ML_HEADS = 4
ML_HEAD_DIM = 256
ML_WIDTH = ML_HEADS * ML_HEAD_DIM
N_GATES = 4 * ML_HEADS
MLA_HEADS = 8
Q_LORA = 512
KV_LORA = 256
QK_NOPE = 128
QK_ROPE = 64
V_HEAD = 128
ROPE_THETA = 10000.0
EPS = 1e-5
NEG = -1e30

LANE = 128
FRONT = LANE - N_META
CHUNK = 128
ROPE_HALF = QK_ROPE // 2
SOFTMAX_SCALE_LOG2 = (QK_NOPE + QK_ROPE) ** -0.5 * math.log2(math.e)
MIB = 1024 * 1024

COL_Q, COL_V, COL_O = 0, ML_WIDTH, 2 * ML_WIDTH
COL_DQ = 3 * ML_WIDTH
COL_DKV = COL_DQ + Q_LORA
COL_KR = COL_DKV + KV_LORA
N_MAIN = COL_KR + 2 * QK_ROPE + LANE

f32 = jnp.float32
bf16 = jnp.bfloat16


def _pick(n, cands):
    for c in cands:
        if n % c == 0:
            return c
    raise ValueError(f"no tile for {n} in {cands}")


def _params(sem, vmem_mib):
    return pltpu.CompilerParams(dimension_semantics=sem, vmem_limit_bytes=vmem_mib * MIB)


def _dot(a, b):
    return jnp.dot(a, b, preferred_element_type=f32)


def _dot_nt(a, b):
    return lax.dot_general(a, b, (((1,), (1,)), ((), ())), preferred_element_type=f32)


def _inproj_kernel(a_ref, w_ref, o_ref):
    o_ref[...] = _dot(a_ref[...], w_ref[...]).astype(o_ref.dtype)


def _keys_t_kernel(w_ref, h_ref, o_ref):
    o_ref[...] = (_dot_nt(w_ref[...], h_ref[...]) * ML_HEAD_DIM ** -0.5).astype(o_ref.dtype)


def _keys_t(hb, wk_t, layer):
    r, d = hb.shape
    tm = _pick(r, (1536, 384, 128))
    return pl.pallas_call(
        _keys_t_kernel,
        grid=(r // tm,),
        in_specs=[pl.BlockSpec((None, ML_WIDTH, d), lambda i: (layer, 0, 0)),
                  pl.BlockSpec((tm, d), lambda i: (i, 0))],
        out_specs=pl.BlockSpec((ML_WIDTH, tm), lambda i: (0, i)),
        out_shape=jax.ShapeDtypeStruct((ML_WIDTH, r), bf16),
        compiler_params=_params(("parallel",), 48),
        name="keys_t",
    )(wk_t, hb)


def _inproj(hb, w_main, layer):
    r, d = hb.shape
    n = w_main.shape[-1]
    tm = _pick(r, (1408, 384, 128))
    tn = _pick(n, (1024, 512, 256, 128))
    return pl.pallas_call(
        _inproj_kernel,
        grid=(r // tm, n // tn),
        in_specs=[pl.BlockSpec((tm, d), lambda i, j: (i, 0)),
                  pl.BlockSpec((None, d, tn), lambda i, j: (layer, 0, j))],
        out_specs=pl.BlockSpec((tm, tn), lambda i, j: (i, j)),
        out_shape=jax.ShapeDtypeStruct((r, n), bf16),
        compiler_params=_params(("parallel", "arbitrary"), 48),
        name="inproj",
    )(hb, w_main)


def _gates_kernel(h_ref, wg_ref, bg_ref, o_ref):
    tm = h_ref.shape[0]
    half = N_GATES // 2
    g = _dot_nt(wg_ref[...], h_ref[...]) + bg_ref[...]
    pos = pl.program_id(1) * tm + lax.broadcasted_iota(jnp.int32, (half, tm), 1)
    unused = pos < FRONT
    gi, gf = g[:half], g[half:]
    li = jnp.where(unused, NEG, gi)
    lf = jnp.where(unused, 0.0, jnp.minimum(gf, 0.0) - jnp.log1p(jnp.exp(-jnp.abs(gf))))
    lane = lax.broadcasted_iota(jnp.int32, (half, CHUNK), 1)
    is_fwd = lax.broadcasted_iota(jnp.int32, (half, CHUNK), 0) < ML_HEADS
    for c in range(tm // CHUNK):
        sl = slice(c * CHUNK, (c + 1) * CHUNK)
        pre = suf = lf[:, sl]
        k = 1
        while k < CHUNK:
            pre = pre + jnp.where(lane >= k, pltpu.roll(pre, k, 1), 0.0)
            suf = suf + jnp.where(lane < CHUNK - k, pltpu.roll(suf, CHUNK - k, 1), 0.0)
            k *= 2
        b = jnp.where(is_fwd, pre, suf)
        o_ref[:half, sl] = li[:, sl] - b
        o_ref[half:, sl] = b


def _gates(hb, wg_t, bg, layer, bsz, lp):
    r, d = hb.shape
    tm = _pick(lp, (1408, 384, 128))
    nt = lp // tm
    return pl.pallas_call(
        _gates_kernel,
        grid=(bsz, nt),
        in_specs=[pl.BlockSpec((tm, d), lambda b, i: (b * nt + i, 0)),
                  pl.BlockSpec((None, N_GATES, d), lambda b, i: (layer, 0, 0)),
                  pl.BlockSpec((None, N_GATES, 1), lambda b, i: (layer, 0, 0))],
        out_specs=pl.BlockSpec((None, N_GATES, tm), lambda b, i: (b, 0, i)),
        out_shape=jax.ShapeDtypeStruct((bsz, N_GATES, lp), f32),
        compiler_params=_params(("parallel", "arbitrary"), 32),
        name="gates",
    )(hb, wg_t, bg)


MLSTM_UNROLL = 16


def _mlstm_kernel(q_ref, kt_ref, v_ref, o_ref, gq_ref, ng_ref, out_ref, hf_scr, hb_scr, c_scr, m_scr, s_scr):
    head = pl.program_id(1)
    lp, dh = q_ref.shape
    t = CHUNK
    nc = lp // t
    c_scr[...] = jnp.zeros_like(c_scr)
    m_scr[...] = jnp.zeros_like(m_scr)
    ti = lax.broadcasted_iota(jnp.int32, (t, t), 0)
    si = lax.broadcasted_iota(jnp.int32, (t, t), 1)
    masks = (si <= ti, si >= ti)
    eye = si == ti
    half = N_GATES // 2
    gate_row = lax.broadcasted_iota(jnp.int32, (half, t), 0)
    ones_col = jnp.where(lax.broadcasted_iota(jnp.int32, (t, LANE), 1) == 0, 1.0, 0.0).astype(bf16)

    def qk_ahead(c, dirn):
        base = pl.multiple_of(jnp.clip(c, 0, nc - 1) * t, t)
        s_scr[dirn] = _dot(q_ref[pl.ds(base, t), :], kt_ref[:, pl.ds(base, t)])

    def chunk(c, dirn, h_scr):
        base = pl.multiple_of(c * t, t)
        q = q_ref[pl.ds(base, t), :]
        kt = kt_ref[:, pl.ds(base, t)]
        v = jnp.concatenate([v_ref[pl.ds(base, t), :], ones_col], axis=1)
        mine = gate_row == head + ML_HEADS * dirn
        r_row = jnp.sum(jnp.where(mine, gq_ref[:half, pl.ds(base, t)], 0.0), axis=0, keepdims=True)
        b_row = jnp.sum(jnp.where(mine, gq_ref[half:, pl.ds(base, t)], 0.0), axis=0, keepdims=True)
        m_state = m_scr[dirn, 0:1, 0:1]
        mask = masks[dirn]
        m_col = jnp.maximum(jnp.max(jnp.where(mask, r_row, -jnp.inf), axis=1, keepdims=True), m_state)
        b_col = jnp.sum(jnp.where(eye, b_row, 0.0), axis=1, keepdims=True)
        m_all = jnp.maximum(jnp.max(r_row, axis=1, keepdims=True), m_state)
        b_last = b_row[:, 0:1] if dirn else b_row[:, t - 1:t]
        s = s_scr[dirn] * jnp.exp(jnp.where(mask, r_row - m_col, -jnp.inf))
        both = jnp.exp(m_state - m_col) * _dot(q, c_scr[dirn].astype(bf16)) + _dot(s.astype(bf16), v)
        floor = jnp.exp(-(b_col + m_col))
        h_scr[pl.ds(base, t), :] = both[:, :dh] / jnp.maximum(jnp.abs(both[:, dh:dh + 1]), floor)
        ktw = (kt.astype(f32) * jnp.exp(r_row - m_all)).astype(bf16)
        c_scr[dirn] = jnp.exp(m_state - m_all) * c_scr[dirn] + _dot(ktw, v)
        m_scr[dirn] = jnp.broadcast_to(b_last + m_all, m_scr.shape[1:])

    def finish(c):
        base = pl.multiple_of(c * t, t)
        hs = hf_scr[pl.ds(base, t), :] + hb_scr[pl.ds(base, t), :]
        mu = jnp.mean(hs, axis=1, keepdims=True)
        var = jnp.mean(jnp.square(hs - mu), axis=1, keepdims=True)
        hn = (hs - mu) * lax.rsqrt(var + EPS) * ng_ref[...]
        gate = jax.nn.sigmoid(o_ref[pl.ds(base, t), :].astype(f32))
        out_ref[pl.ds(base, t), :] = (gate * hn).astype(out_ref.dtype)

    def scan_step(c):
        chunk(c, 0, hf_scr)
        chunk(nc - 1 - c, 1, hb_scr)
        qk_ahead(c + 1, 0)
        qk_ahead(nc - 2 - c, 1)

    def first_half(c, carry):
        scan_step(c)
        return carry

    def second_half(c, carry):
        finish(c - 1)
        finish(nc - c)
        scan_step(c)
        return carry

    qk_ahead(0, 0)
    qk_ahead(nc - 1, 1)
    meet = (nc + 1) // 2
    lax.fori_loop(0, meet, first_half, 0, unroll=MLSTM_UNROLL)
    lax.fori_loop(meet, nc, second_half, 0, unroll=MLSTM_UNROLL)
    finish(nc - 1)
    finish(0)


def _mlstm(u, kt, gq, ml_norm_g, layer, bsz, lp):
    r = u.shape[0]
    dh = ML_HEAD_DIM

    def col(first):
        return pl.BlockSpec((lp, dh), lambda b, h: (b, first // dh + h))

    return pl.pallas_call(
        _mlstm_kernel,
        grid=(bsz, ML_HEADS),
        in_specs=[col(COL_Q), pl.BlockSpec((dh, lp), lambda b, h: (h, b)), col(COL_V), col(COL_O),
                  pl.BlockSpec((None, N_GATES, lp), lambda b, h: (b, 0, 0)),
                  pl.BlockSpec((None, 1, dh), lambda b, h: (layer, 0, h))],
        out_specs=pl.BlockSpec((lp, dh), lambda b, h: (b, h)),
        out_shape=jax.ShapeDtypeStruct((r, ML_WIDTH), bf16),
        scratch_shapes=[pltpu.VMEM((lp, dh), f32), pltpu.VMEM((lp, dh), f32),
                        pltpu.VMEM((2, dh, dh + LANE), f32), pltpu.VMEM((2, 8, LANE), f32),
                        pltpu.VMEM((2, CHUNK, CHUNK), f32)],
        compiler_params=_params(("parallel", "arbitrary"), 48),
        name="mlstm",
    )(u, kt, u, u, gq, ml_norm_g)


def _rms(x, g):
    return x * lax.rsqrt(jnp.mean(jnp.square(x), axis=-1, keepdims=True) + EPS) * g


def _rope(x, cos, sin_a, sin_b):
    n = x.shape[1]
    reps = n // LANE
    if reps > 1:
        cos, sin_a, sin_b = (jnp.concatenate([a] * reps, axis=1) for a in (cos, sin_a, sin_b))
    return x * cos + pltpu.roll(x, n - ROPE_HALF, 1) * sin_a + pltpu.roll(x, ROPE_HALF, 1) * sin_b


def _mla_prep_kernel(dq_ref, dkv_ref, kr_ref, wuq_ref, wukv_ref, qg_ref, kvg_ref,
                     cos_ref, sa_ref, sb_ref, qc_ref, kc_ref, v_ref):
    cos, sa, sb = cos_ref[...], sa_ref[...], sb_ref[...]
    q = _dot(_rms(dq_ref[...].astype(f32), qg_ref[...]).astype(bf16), wuq_ref[...]) * SOFTMAX_SCALE_LOG2
    kv =_dot(_rms(dkv_ref[...].astype(f32), kvg_ref[...]).astype(bf16), wukv_ref[...])
    q_rope = _rope(q[:, MLA_HEADS * QK_NOPE:], cos, sa, sb)
    k_rope = _rope(kr_ref[...].astype(f32), cos, sa, sb)
    tm = q.shape[0]
    lane = lax.broadcasted_iota(jnp.int32, (tm, LANE), 1)
    pos = pl.program_id(1) * tm + lax.broadcasted_iota(jnp.int32, (tm, LANE), 0)
    low = lane < QK_ROPE
    marker = lane == QK_ROPE
    one_hot = jnp.where(marker, 1.0, 0.0)
    k_tail = jnp.where(low, k_rope, jnp.where(marker & (pos < FRONT), -jnp.inf, 0.0)).astype(bf16)
    ones_col = jnp.where(lane == 0, 1.0, 0.0).astype(bf16)
    for h in range(MLA_HEADS):
        pair = q_rope[:, (h // 2) * LANE:(h // 2 + 1) * LANE]
        if h % 2:
            pair = pltpu.roll(pair, QK_ROPE, 1)
        qc_ref[h, :, :QK_NOPE] = q[:, h * QK_NOPE:(h + 1) * QK_NOPE].astype(bf16)
        qc_ref[h, :, QK_NOPE:] = jnp.where(low, pair, one_hot).astype(bf16)
        kc_ref[h, :, :QK_NOPE] = kv[:, h * QK_NOPE:(h + 1) * QK_NOPE].astype(bf16)
        kc_ref[h, :, QK_NOPE:] = k_tail
        v_ref[h, :, :V_HEAD] = kv[:, (MLA_HEADS + h) * V_HEAD:(MLA_HEADS + h + 1) * V_HEAD].astype(bf16)
        v_ref[h, :, V_HEAD:] = ones_col


def _mla_prep(u, w_uq_p, w_ukv_p, q_norm_g, kv_norm_g, tables, layer, bsz, lp):
    r = u.shape[0]
    tm = _pick(lp, (384, 128))
    nt = lp // tm
    kd = QK_NOPE + LANE
    tab = pl.BlockSpec((tm, LANE), lambda b, i: (i, 0))
    row = lambda b, i: b * nt + i
    return pl.pallas_call(
        _mla_prep_kernel,
        grid=(bsz, nt),
        in_specs=[pl.BlockSpec((tm, Q_LORA), lambda b, i: (row(b, i), COL_DQ // Q_LORA)),
                  pl.BlockSpec((tm, KV_LORA), lambda b, i: (row(b, i), COL_DKV // KV_LORA)),
                  pl.BlockSpec((tm, LANE), lambda b, i: (row(b, i), COL_KR // LANE)),
                  pl.BlockSpec((None,) + w_uq_p.shape[1:], lambda b, i: (layer, 0, 0)),
                  pl.BlockSpec((None,) + w_ukv_p.shape[1:], lambda b, i: (layer, 0, 0)),
                  pl.BlockSpec((None, 1, Q_LORA), lambda b, i: (layer, 0, 0)),
                  pl.BlockSpec((None, 1, KV_LORA), lambda b, i: (layer, 0, 0)),
                  tab, tab, tab],
        out_specs=[pl.BlockSpec((MLA_HEADS, tm, kd), lambda b, i: (0, row(b, i), 0)),
                   pl.BlockSpec((MLA_HEADS, tm, kd), lambda b, i: (0, row(b, i), 0)),
                   pl.BlockSpec((MLA_HEADS, tm, 2 * V_HEAD), lambda b, i: (0, row(b, i), 0))],
        out_shape=[jax.ShapeDtypeStruct((MLA_HEADS, r, kd), bf16),
                   jax.ShapeDtypeStruct((MLA_HEADS, r, kd), bf16),
                   jax.ShapeDtypeStruct((MLA_HEADS, r, 2 * V_HEAD), bf16)],
        compiler_params=_params(("parallel", "arbitrary"), 48),
        name="mla_prep",
    )(u, u, u, w_uq_p, w_ukv_p, q_norm_g, kv_norm_g, *tables)


ATTN_HEADS_PER_STEP = 2
ATTN_TILES_PER_BODY = 12


def _attn_kernel(q_ref, k_ref, v_ref, o_ref, s0_ref, s1_ref, p0_ref, p1_ref, *, tq):
    nh, lp, _ = k_ref.shape
    nq = lp // tq
    n_tiles = nh * nq
    s_refs, p_refs = (s0_ref, s1_ref), (p0_ref, p1_ref)

    @pl.when((pl.program_id(0) == 0) & (pl.program_id(1) == 0))
    def _():
        for ref in s_refs + p_refs:
            ref[...] = jnp.zeros_like(ref)

    def tile(x):
        x = jnp.clip(x, 0, n_tiles - 1)
        hh = x // nq
        return hh, pl.ds(pl.multiple_of((x - hh * nq) * tq, tq), tq)

    def scores(x, slot):
        hh, rows = tile(x)
        s_refs[slot][...] = _dot_nt(q_ref[hh, rows, :], k_ref[hh])

    def softmax(slot):
        s = s_refs[slot][...]
        p_refs[slot][...] = jnp.exp2(s - jnp.max(s, axis=1, keepdims=True)).astype(bf16)

    def weighted(x, slot):
        hh, rows = tile(x)
        acc = _dot(p_refs[slot][...], v_ref[hh])
        o_ref[hh, rows, :] = (acc[:, :V_HEAD] * (1.0 / acc[:, V_HEAD:V_HEAD + 1])).astype(o_ref.dtype)

    n_bodies = -(-(n_tiles + 2) // ATTN_TILES_PER_BODY)
    start = n_tiles + 2 - ATTN_TILES_PER_BODY * n_bodies

    def body(u, carry):
        for e in range(ATTN_TILES_PER_BODY):
            x = ATTN_TILES_PER_BODY * u + e + start
            weighted(x - 2, e % 2)
            softmax((e + 1) % 2)
            scores(x, e % 2)
        return carry

    lax.fori_loop(0, n_bodies, body, 0)


def _attention(qc, kc, v, bsz, lp):
    r = v.shape[1]
    tq = _pick(lp, (384, 128))
    kd = qc.shape[-1]
    nh = ATTN_HEADS_PER_STEP
    blk = lambda width: pl.BlockSpec((nh, lp, width), lambda b, h: (h, b, 0))
    return pl.pallas_call(
        functools.partial(_attn_kernel, tq=tq),
        grid=(bsz, MLA_HEADS // nh),
        in_specs=[blk(kd), blk(kd), blk(2 * V_HEAD)],
        out_specs=blk(V_HEAD),
        out_shape=jax.ShapeDtypeStruct((MLA_HEADS, r, V_HEAD), bf16),
        scratch_shapes=[pltpu.VMEM((tq, lp), f32), pltpu.VMEM((tq, lp), f32),
                        pltpu.VMEM((tq, lp), bf16), pltpu.VMEM((tq, lp), bf16)],
        compiler_params=_params(("arbitrary", "arbitrary"), 60),
        name="attention",
    )(qc, kc, v)


def _layer_norm(y, g, b):
    mu = jnp.mean(y, axis=-1, keepdims=True)
    var = jnp.mean(jnp.square(y - mu), axis=-1, keepdims=True)
    return (y - mu) * lax.rsqrt(var + EPS) * g + b


OUTPROJ_SUBTILES = 2


def _outproj_kernel(ml_ref, at_ref, w_ref, h_ref, g_ref, b_ref, o_ref, ob_ref, *, alpha):
    ts = h_ref.shape[0] // OUTPROJ_SUBTILES
    for sub in range(OUTPROJ_SUBTILES):
        rows = slice(sub * ts, (sub + 1) * ts)
        at = jnp.concatenate([at_ref[h, rows, :] for h in range(MLA_HEADS)], axis=1)
        mix = _dot(ml_ref[rows, :], w_ref[:ML_WIDTH, :]) + _dot(at, w_ref[ML_WIDTH:, :])
        y = _layer_norm(alpha * h_ref[rows, :] + mix, g_ref[...], b_ref[...])
        o_ref[rows, :] = y
        ob_ref[rows, :] = y.astype(bf16)


def _outproj(ml, at, w_out, h, ln_g, ln_b, layer, alpha):
    r, d = h.shape
    tm = _pick(r, (768, 384, 128))
    vec = pl.BlockSpec((None, 1, d), lambda i: (layer, 0, 0))
    return pl.pallas_call(
        functools.partial(_outproj_kernel, alpha=alpha),
        grid=(r // tm,),
        in_specs=[pl.BlockSpec((tm, ML_WIDTH), lambda i: (i, 0)),
                  pl.BlockSpec((MLA_HEADS, tm, V_HEAD), lambda i: (0, i, 0)),
                  pl.BlockSpec((None,) + w_out.shape[1:], lambda i: (layer, 0, 0), pipeline_mode=pl.Buffered(1)),
                  pl.BlockSpec((tm, d), lambda i: (i, 0)), vec, vec],
        out_specs=[pl.BlockSpec((tm, d), lambda i: (i, 0)), pl.BlockSpec((tm, d), lambda i: (i, 0))],
        out_shape=[jax.ShapeDtypeStruct((r, d), f32), jax.ShapeDtypeStruct((r, d), bf16)],
        compiler_params=_params(("parallel",), 56),
        name="outproj_ln",
    )(ml, at, w_out, h, ln_g, ln_b)


HALO = 16


def _ffn_up_kernel(prev_ref, main_ref, next_ref, wg_ref, wv_ref, cw_ref, cb_ref, o_ref, ext_scr, gate_scr, *, lp):
    tm = main_ref.shape[0]

    @pl.when(pl.program_id(2) == 0)
    def _():
        ext_scr[:HALO, :] = prev_ref[...]
        ext_scr[HALO:HALO + tm, :] = main_ref[...]
        ext_scr[HALO + tm:, :] = next_ref[...]

    rows = tm + 2 * HALO
    pos = pl.program_id(1) * tm - HALO + lax.broadcasted_iota(jnp.int32, (rows, 1), 0)
    inside = (pos >= FRONT) & (pos < lp)
    gate_scr[...] = jnp.where(inside, _dot(ext_scr[...], wg_ref[...]), 0.0)
    taps = [gate_scr[pl.ds(HALO - 1 + j, tm), :] * cw_ref[j:j + 1, :] for j in range(3)]
    conv = taps[0] + taps[1] + taps[2] + cb_ref[...]
    val = _dot(main_ref[...], wv_ref[...])
    o_ref[...] = (conv * jax.nn.sigmoid(conv) * val).astype(o_ref.dtype)


def _ffn_up(hb, w_up, conv_w, conv_b, layer, bsz, lp):
    r, d = hb.shape
    dff = conv_b.shape[-1]
    tm = _pick(lp, (1408, 384, 128))
    tn = _pick(dff, (512, 256, 128))
    nt, nj = lp // tm, dff // tn
    hblk = tm // HALO
    last = r // HALO - 1
    return pl.pallas_call(
        functools.partial(_ffn_up_kernel, lp=lp),
        grid=(bsz, nt, nj),
        in_specs=[pl.BlockSpec((HALO, d), lambda b, i, j: (jnp.maximum((b * nt + i) * hblk - 1, 0), 0)),
                  pl.BlockSpec((tm, d), lambda b, i, j: (b * nt + i, 0)),
                  pl.BlockSpec((HALO, d), lambda b, i, j: (jnp.minimum((b * nt + i + 1) * hblk, last), 0)),
                  pl.BlockSpec((None, d, tn), lambda b, i, j: (layer, 0, j)),
                  pl.BlockSpec((None, d, tn), lambda b, i, j: (layer, 0, nj + j)),
                  pl.BlockSpec((None, 3, tn), lambda b, i, j: (layer, 0, j)),
                  pl.BlockSpec((None, 1, tn), lambda b, i, j: (layer, 0, j))],
        out_specs=pl.BlockSpec((tm, tn), lambda b, i, j: (b * nt + i, j)),
        out_shape=jax.ShapeDtypeStruct((r, dff), bf16),
        scratch_shapes=[pltpu.VMEM((tm + 2 * HALO, d), bf16), pltpu.VMEM((tm + 2 * HALO, tn), f32)],
        compiler_params=_params(("parallel", "parallel", "arbitrary"), 56),
        name="ffn_up_conv",
    )(hb, hb, hb, w_up, w_up, conv_w, conv_b)


FFN_DOWN_SUBTILES = 2


def _ffn_down_kernel(a_ref, w_ref, h_ref, g_ref, b_ref, o_ref, ob_ref, *, alpha):
    ts = h_ref.shape[0] // FFN_DOWN_SUBTILES
    for sub in range(FFN_DOWN_SUBTILES):
        rows = slice(sub * ts, (sub + 1) * ts)
        y = _layer_norm(alpha * h_ref[rows, :] + _dot(a_ref[rows, :], w_ref[...]), g_ref[...], b_ref[...])
        o_ref[rows, :] = y
        ob_ref[rows, :] = y.astype(bf16)


def _ffn_down(act, w_down, h, ln_g, ln_b, layer, alpha):
    r, d = h.shape
    dff = act.shape[1]
    tm = _pick(r, (384, 128))
    vec = pl.BlockSpec((None, 1, d), lambda i: (layer, 0, 0))
    return pl.pallas_call(
        functools.partial(_ffn_down_kernel, alpha=alpha),
        grid=(r // tm,),
        in_specs=[pl.BlockSpec((tm, dff), lambda i: (i, 0)),
                  pl.BlockSpec((None, dff, d), lambda i: (layer, 0, 0), pipeline_mode=pl.Buffered(1)),
                  pl.BlockSpec((tm, d), lambda i: (i, 0)), vec, vec],
        out_specs=[pl.BlockSpec((tm, d), lambda i: (i, 0)), pl.BlockSpec((tm, d), lambda i: (i, 0))],
        out_shape=[jax.ShapeDtypeStruct((r, d), f32), jax.ShapeDtypeStruct((r, d), bf16)],
        compiler_params=_params(("parallel",), 60),
        name="ffn_down_ln",
    )(act, w_down, h, ln_g, ln_b)


def _rope_tables(lp):
    inv_freq = ROPE_THETA ** (-jnp.arange(0, QK_ROPE, 2, dtype=f32) / QK_ROPE)
    pos = jnp.maximum(jnp.arange(lp, dtype=f32) - FRONT, 0.0)
    ang = pos[:, None] * inv_freq[None, :]
    cos, sin = jnp.cos(ang), jnp.sin(ang)
    zero = jnp.zeros_like(sin)
    reps = LANE // QK_ROPE
    cos_t = jnp.tile(jnp.concatenate([cos, cos], axis=1), (1, reps))
    sin_a = jnp.tile(jnp.concatenate([-sin, zero], axis=1), (1, reps))
    sin_b = jnp.tile(jnp.concatenate([zero, sin], axis=1), (1, reps))
    return cos_t, sin_a, sin_b


def kernel(x_prompt, x_sample, meta_tokens, w_in, b_gates, ml_norm_g, q_norm_g, kv_norm_g, w_uq, w_ukv,
           w_out, ln1_g, ln1_b, w_up, conv_w, conv_b, w_down, ln2_g, ln2_b):
    depth, d, _ = w_in.shape
    assert x_prompt.shape[1:] == x_sample.shape[1:]
    n_prompt = x_prompt.shape[0]
    x = jnp.concatenate([x_prompt, x_sample], axis=0)
    bsz, seq, _ = x.shape
    assert seq % LANE == 0
    lp = LANE + seq
    alpha = (2 * depth) ** 0.25

    gate_lo = 4 * ML_WIDTH
    w_main = jnp.concatenate(
        [w_in[..., :ML_WIDTH], w_in[..., 2 * ML_WIDTH:gate_lo], w_in[..., gate_lo + N_GATES:], w_in[..., -QK_ROPE:],
         jnp.zeros((depth, d, N_MAIN - COL_KR - 2 * QK_ROPE), w_in.dtype)], axis=-1).astype(bf16)
    wk_t = jnp.swapaxes(w_in[..., ML_WIDTH:2 * ML_WIDTH], 1, 2).astype(bf16)
    h4 = ML_HEADS
    perm = jnp.array(list(range(0, h4)) + list(range(2 * h4, 3 * h4))
                     + list(range(h4, 2 * h4)) + list(range(3 * h4, 4 * h4)))
    wg_t = jnp.swapaxes(w_in[..., gate_lo:gate_lo + N_GATES], 1, 2)[:, perm, :].astype(bf16)
    bg = b_gates[:, perm, None].astype(f32)
    uq = w_uq.reshape(depth, Q_LORA, MLA_HEADS, QK_NOPE + QK_ROPE)
    w_uq_p = jnp.concatenate([uq[..., :QK_NOPE].reshape(depth, Q_LORA, -1),
                              uq[..., QK_NOPE:].reshape(depth, Q_LORA, -1)], axis=-1).astype(bf16)
    ukv = w_ukv.reshape(depth, KV_LORA, MLA_HEADS, QK_NOPE + V_HEAD)
    w_ukv_p = jnp.concatenate([ukv[..., :QK_NOPE].reshape(depth, KV_LORA, -1),
                               ukv[..., QK_NOPE:].reshape(depth, KV_LORA, -1)], axis=-1).astype(bf16)
    w_out_b, w_up_b, w_down_b = w_out.astype(bf16), w_up.astype(bf16), w_down.astype(bf16)
    vec3 = lambda a: a[:, None, :].astype(f32)
    ml_g, q_g, kv_g = vec3(ml_norm_g), vec3(q_norm_g), vec3(kv_norm_g)
    l1g, l1b, l2g, l2b, cb = vec3(ln1_g), vec3(ln1_b), vec3(ln2_g), vec3(ln2_b), vec3(conv_b)
    cw = conv_w.astype(f32)
    tables = _rope_tables(lp)

    meta = jnp.broadcast_to(meta_tokens.astype(x.dtype)[None], (bsz, N_META, d))
    h = jnp.concatenate([jnp.zeros((bsz, FRONT, d), x.dtype), meta, x], axis=1).reshape(bsz * lp, d)
    hb = h.astype(bf16)
    for layer in range(depth):
        u = _inproj(hb, w_main, layer)
        gq = _gates(hb, wg_t, bg, layer, bsz, lp)
        kt = _keys_t(hb, wk_t, layer)
        ml = _mlstm(u, kt, gq, ml_g, layer, bsz, lp)
        qc, kc, v = _mla_prep(u, w_uq_p, w_ukv_p, q_g, kv_g, tables, layer, bsz, lp)
        at = _attention(qc, kc, v, bsz, lp)
        h, hb = _outproj(ml, at, w_out_b, h, l1g, l1b, layer, alpha)
        act = _ffn_up(hb, w_up_b, cw, cb, layer, bsz, lp)
        h, hb = _ffn_down(act, w_down_b, h, l2g, l2b, layer, alpha)
    y = h.reshape(bsz, lp, d)[:, LANE:]
    return (y[:n_prompt], y[n_prompt:])
```

```python
import functools
import math

import jax
import jax.numpy as jnp
from jax import lax
from jax.experimental import pallas as pl
from jax.experimental.pallas import tpu as pltpu

N_META = 16
ML_HEADS = 4
ML_HEAD_DIM = 256
ML_WIDTH = ML_HEADS * ML_HEAD_DIM
N_GATES = 4 * ML_HEADS
MLA_HEADS = 8
Q_LORA = 512
KV_LORA = 256
QK_NOPE = 128
QK_ROPE = 64
V_HEAD = 128
ROPE_THETA = 10000.0
EPS = 1e-5
NEG = -1e30

LANE = 128
FRONT = LANE - N_META
CHUNK = 128
ROPE_HALF = QK_ROPE // 2
SOFTMAX_SCALE_LOG2 = (QK_NOPE + QK_ROPE) ** -0.5 * math.log2(math.e)
MIB = 1024 * 1024

COL_Q, COL_V, COL_O = 0, ML_WIDTH, 2 * ML_WIDTH
COL_DQ = 3 * ML_WIDTH
COL_DKV = COL_DQ + Q_LORA
COL_KR = COL_DKV + KV_LORA
N_MAIN = COL_KR + 2 * QK_ROPE + LANE

f32 = jnp.float32
bf16 = jnp.bfloat16


def _pick(n, cands):
    for c in cands:
        if n % c == 0:
            return c
    raise ValueError(f"no tile for {n} in {cands}")


def _params(sem, vmem_mib):
    return pltpu.CompilerParams(dimension_semantics=sem, vmem_limit_bytes=vmem_mib * MIB)


def _dot(a, b):
    return jnp.dot(a, b, preferred_element_type=f32)


def _dot_nt(a, b):
    return lax.dot_general(a, b, (((1,), (1,)), ((), ())), preferred_element_type=f32)


def _inproj_kernel(a_ref, w_ref, o_ref):
    o_ref[...] = _dot(a_ref[...], w_ref[...]).astype(o_ref.dtype)


def _keys_t_kernel(w_ref, h_ref, o_ref):
    o_ref[...] = (_dot_nt(w_ref[...], h_ref[...]) * ML_HEAD_DIM ** -0.5).astype(o_ref.dtype)


def _keys_t(hb, wk_t, layer):
    r, d = hb.shape
    tm = _pick(r, (1536, 384, 128))
    return pl.pallas_call(
        _keys_t_kernel,
        grid=(r // tm,),
        in_specs=[pl.BlockSpec((None, ML_WIDTH, d), lambda i: (layer, 0, 0)),
                  pl.BlockSpec((tm, d), lambda i: (i, 0))],
        out_specs=pl.BlockSpec((ML_WIDTH, tm), lambda i: (0, i)),
        out_shape=jax.ShapeDtypeStruct((ML_WIDTH, r), bf16),
        compiler_params=_params(("parallel",), 48),
        name="keys_t",
    )(wk_t, hb)


def _inproj(hb, w_main, layer):
    r, d = hb.shape
    n = w_main.shape[-1]
    tm = _pick(r, (1408, 384, 128))
    tn = _pick(n, (1024, 512, 256, 128))
    return pl.pallas_call(
        _inproj_kernel,
        grid=(r // tm, n // tn),
        in_specs=[pl.BlockSpec((tm, d), lambda i, j: (i, 0)),
                  pl.BlockSpec((None, d, tn), lambda i, j: (layer, 0, j))],
        out_specs=pl.BlockSpec((tm, tn), lambda i, j: (i, j)),
        out_shape=jax.ShapeDtypeStruct((r, n), bf16),
        compiler_params=_params(("parallel", "arbitrary"), 48),
        name="inproj",
    )(hb, w_main)


def _gates_kernel(h_ref, wg_ref, bg_ref, o_ref):
    tm = h_ref.shape[0]
    half = N_GATES // 2
    g = _dot_nt(wg_ref[...], h_ref[...]) + bg_ref[...]
    pos = pl.program_id(1) * tm + lax.broadcasted_iota(jnp.int32, (half, tm), 1)
    unused = pos < FRONT
    gi, gf = g[:half], g[half:]
    li = jnp.where(unused, NEG, gi)
    lf = jnp.where(unused, 0.0, jnp.minimum(gf, 0.0) - jnp.log1p(jnp.exp(-jnp.abs(gf))))
    lane = lax.broadcasted_iota(jnp.int32, (half, CHUNK), 1)
    is_fwd = lax.broadcasted_iota(jnp.int32, (half, CHUNK), 0) < ML_HEADS
    for c in range(tm // CHUNK):
        sl = slice(c * CHUNK, (c + 1) * CHUNK)
        pre = suf = lf[:, sl]
        k = 1
        while k < CHUNK:
            pre = pre + jnp.where(lane >= k, pltpu.roll(pre, k, 1), 0.0)
            suf = suf + jnp.where(lane < CHUNK - k, pltpu.roll(suf, CHUNK - k, 1), 0.0)
            k *= 2
        b = jnp.where(is_fwd, pre, suf)
        o_ref[:half, sl] = li[:, sl] - b
        o_ref[half:, sl] = b


def _gates(hb, wg_t, bg, layer, bsz, lp):
    r, d = hb.shape
    tm = _pick(lp, (1408, 384, 128))
    nt = lp // tm
    return pl.pallas_call(
        _gates_kernel,
        grid=(bsz, nt),
        in_specs=[pl.BlockSpec((tm, d), lambda b, i: (b * nt + i, 0)),
                  pl.BlockSpec((None, N_GATES, d), lambda b, i: (layer, 0, 0)),
                  pl.BlockSpec((None, N_GATES, 1), lambda b, i: (layer, 0, 0))],
        out_specs=pl.BlockSpec((None, N_GATES, tm), lambda b, i: (b, 0, i)),
        out_shape=jax.ShapeDtypeStruct((bsz, N_GATES, lp), f32),
        compiler_params=_params(("parallel", "arbitrary"), 32),
        name="gates",
    )(hb, wg_t, bg)


MLSTM_UNROLL = 16


def _mlstm_kernel(q_ref, kt_ref, v_ref, o_ref, gq_ref, ng_ref, out_ref, hf_scr, hb_scr, c_scr, m_scr, s_scr):
    head = pl.program_id(1)
    lp, dh = q_ref.shape
    t = CHUNK
    nc = lp // t
    c_scr[...] = jnp.zeros_like(c_scr)
    m_scr[...] = jnp.zeros_like(m_scr)
    ti = lax.broadcasted_iota(jnp.int32, (t, t), 0)
    si = lax.broadcasted_iota(jnp.int32, (t, t), 1)
    masks = (si <= ti, si >= ti)
    eye = si == ti
    half = N_GATES // 2
    gate_row = lax.broadcasted_iota(jnp.int32, (half, t), 0)
    ones_col = jnp.where(lax.broadcasted_iota(jnp.int32, (t, LANE), 1) == 0, 1.0, 0.0).astype(bf16)

    def qk_ahead(c, dirn):
        base = pl.multiple_of(jnp.clip(c, 0, nc - 1) * t, t)
        s_scr[dirn] = _dot(q_ref[pl.ds(base, t), :], kt_ref[:, pl.ds(base, t)])

    def chunk(c, dirn, h_scr):
        base = pl.multiple_of(c * t, t)
        q = q_ref[pl.ds(base, t), :]
        kt = kt_ref[:, pl.ds(base, t)]
        v = jnp.concatenate([v_ref[pl.ds(base, t), :], ones_col], axis=1)
        mine = gate_row == head + ML_HEADS * dirn
        r_row = jnp.sum(jnp.where(mine, gq_ref[:half, pl.ds(base, t)], 0.0), axis=0, keepdims=True)
        b_row = jnp.sum(jnp.where(mine, gq_ref[half:, pl.ds(base, t)], 0.0), axis=0, keepdims=True)
        m_state = m_scr[dirn, 0:1, 0:1]
        mask = masks[dirn]
        m_col = jnp.maximum(jnp.max(jnp.where(mask, r_row, -jnp.inf), axis=1, keepdims=True), m_state)
        b_col = jnp.sum(jnp.where(eye, b_row, 0.0), axis=1, keepdims=True)
        m_all = jnp.maximum(jnp.max(r_row, axis=1, keepdims=True), m_state)
        b_last = b_row[:, 0:1] if dirn else b_row[:, t - 1:t]
        s = s_scr[dirn] * jnp.exp(jnp.where(mask, r_row - m_col, -jnp.inf))
        both = jnp.exp(m_state - m_col) * _dot(q, c_scr[dirn].astype(bf16)) + _dot(s.astype(bf16), v)
        floor = jnp.exp(-(b_col + m_col))
        h_scr[pl.ds(base, t), :] = both[:, :dh] / jnp.maximum(jnp.abs(both[:, dh:dh + 1]), floor)
        ktw = (kt.astype(f32) * jnp.exp(r_row - m_all)).astype(bf16)
        c_scr[dirn] = jnp.exp(m_state - m_all) * c_scr[dirn] + _dot(ktw, v)
        m_scr[dirn] = jnp.broadcast_to(b_last + m_all, m_scr.shape[1:])

    def finish(c):
        base = pl.multiple_of(c * t, t)
        hs = hf_scr[pl.ds(base, t), :] + hb_scr[pl.ds(base, t), :]
        mu = jnp.mean(hs, axis=1, keepdims=True)
        var = jnp.mean(jnp.square(hs - mu), axis=1, keepdims=True)
        hn = (hs - mu) * lax.rsqrt(var + EPS) * ng_ref[...]
        gate = jax.nn.sigmoid(o_ref[pl.ds(base, t), :].astype(f32))
        out_ref[pl.ds(base, t), :] = (gate * hn).astype(out_ref.dtype)

    def scan_step(c):
        chunk(c, 0, hf_scr)
        chunk(nc - 1 - c, 1, hb_scr)
        qk_ahead(c + 1, 0)
        qk_ahead(nc - 2 - c, 1)

    def first_half(c, carry):
        scan_step(c)
        return carry

    def second_half(c, carry):
        finish(c - 1)
        finish(nc - c)
        scan_step(c)
        return carry

    qk_ahead(0, 0)
    qk_ahead(nc - 1, 1)
    meet = (nc + 1) // 2
    lax.fori_loop(0, meet, first_half, 0, unroll=MLSTM_UNROLL)
    lax.fori_loop(meet, nc, second_half, 0, unroll=MLSTM_UNROLL)
    finish(nc - 1)
    finish(0)


def _mlstm(u, kt, gq, ml_norm_g, layer, bsz, lp):
    r = u.shape[0]
    dh = ML_HEAD_DIM

    def col(first):
        return pl.BlockSpec((lp, dh), lambda b, h: (b, first // dh + h))

    return pl.pallas_call(
        _mlstm_kernel,
        grid=(bsz, ML_HEADS),
        in_specs=[col(COL_Q), pl.BlockSpec((dh, lp), lambda b, h: (h, b)), col(COL_V), col(COL_O),
                  pl.BlockSpec((None, N_GATES, lp), lambda b, h: (b, 0, 0)),
                  pl.BlockSpec((None, 1, dh), lambda b, h: (layer, 0, h))],
        out_specs=pl.BlockSpec((lp, dh), lambda b, h: (b, h)),
        out_shape=jax.ShapeDtypeStruct((r, ML_WIDTH), bf16),
        scratch_shapes=[pltpu.VMEM((lp, dh), f32), pltpu.VMEM((lp, dh), f32),
                        pltpu.VMEM((2, dh, dh + LANE), f32), pltpu.VMEM((2, 8, LANE), f32),
                        pltpu.VMEM((2, CHUNK, CHUNK), f32)],
        compiler_params=_params(("parallel", "arbitrary"), 48),
        name="mlstm",
    )(u, kt, u, u, gq, ml_norm_g)


def _rms(x, g):
    return x * lax.rsqrt(jnp.mean(jnp.square(x), axis=-1, keepdims=True) + EPS) * g


def _rope(x, cos, sin_a, sin_b):
    n = x.shape[1]
    reps = n // LANE
    if reps > 1:
        cos, sin_a, sin_b = (jnp.concatenate([a] * reps, axis=1) for a in (cos, sin_a, sin_b))
    return x * cos + pltpu.roll(x, n - ROPE_HALF, 1) * sin_a + pltpu.roll(x, ROPE_HALF, 1) * sin_b


def _mla_prep_kernel(dq_ref, dkv_ref, kr_ref, wuq_ref, wukv_ref, qg_ref, kvg_ref,
                     cos_ref, sa_ref, sb_ref, qc_ref, kc_ref, v_ref):
    cos, sa, sb = cos_ref[...], sa_ref[...], sb_ref[...]
    q = _dot(_rms(dq_ref[...].astype(f32), qg_ref[...]).astype(bf16), wuq_ref[...]) * SOFTMAX_SCALE_LOG2
    kv =_dot(_rms(dkv_ref[...].astype(f32), kvg_ref[...]).astype(bf16), wukv_ref[...])
    q_rope = _rope(q[:, MLA_HEADS * QK_NOPE:], cos, sa, sb)
    k_rope = _rope(kr_ref[...].astype(f32), cos, sa, sb)
    tm = q.shape[0]
    lane = lax.broadcasted_iota(jnp.int32, (tm, LANE), 1)
    pos = pl.program_id(1) * tm + lax.broadcasted_iota(jnp.int32, (tm, LANE), 0)
    low = lane < QK_ROPE
    marker = lane == QK_ROPE
    one_hot = jnp.where(marker, 1.0, 0.0)
    k_tail = jnp.where(low, k_rope, jnp.where(marker & (pos < FRONT), -jnp.inf, 0.0)).astype(bf16)
    ones_col = jnp.where(lane == 0, 1.0, 0.0).astype(bf16)
    for h in range(MLA_HEADS):
        pair = q_rope[:, (h // 2) * LANE:(h // 2 + 1) * LANE]
        if h % 2:
            pair = pltpu.roll(pair, QK_ROPE, 1)
        qc_ref[h, :, :QK_NOPE] = q[:, h * QK_NOPE:(h + 1) * QK_NOPE].astype(bf16)
        qc_ref[h, :, QK_NOPE:] = jnp.where(low, pair, one_hot).astype(bf16)
        kc_ref[h, :, :QK_NOPE] = kv[:, h * QK_NOPE:(h + 1) * QK_NOPE].astype(bf16)
        kc_ref[h, :, QK_NOPE:] = k_tail
        v_ref[h, :, :V_HEAD] = kv[:, (MLA_HEADS + h) * V_HEAD:(MLA_HEADS + h + 1) * V_HEAD].astype(bf16)
        v_ref[h, :, V_HEAD:] = ones_col


def _mla_prep(u, w_uq_p, w_ukv_p, q_norm_g, kv_norm_g, tables, layer, bsz, lp):
    r = u.shape[0]
    tm = _pick(lp, (384, 128))
    nt = lp // tm
    kd = QK_NOPE + LANE
    tab = pl.BlockSpec((tm, LANE), lambda b, i: (i, 0))
    row = lambda b, i: b * nt + i
    return pl.pallas_call(
        _mla_prep_kernel,
        grid=(bsz, nt),
        in_specs=[pl.BlockSpec((tm, Q_LORA), lambda b, i: (row(b, i), COL_DQ // Q_LORA)),
                  pl.BlockSpec((tm, KV_LORA), lambda b, i: (row(b, i), COL_DKV // KV_LORA)),
                  pl.BlockSpec((tm, LANE), lambda b, i: (row(b, i), COL_KR // LANE)),
                  pl.BlockSpec((None,) + w_uq_p.shape[1:], lambda b, i: (layer, 0, 0)),
                  pl.BlockSpec((None,) + w_ukv_p.shape[1:], lambda b, i: (layer, 0, 0)),
                  pl.BlockSpec((None, 1, Q_LORA), lambda b, i: (layer, 0, 0)),
                  pl.BlockSpec((None, 1, KV_LORA), lambda b, i: (layer, 0, 0)),
                  tab, tab, tab],
        out_specs=[pl.BlockSpec((MLA_HEADS, tm, kd), lambda b, i: (0, row(b, i), 0)),
                   pl.BlockSpec((MLA_HEADS, tm, kd), lambda b, i: (0, row(b, i), 0)),
                   pl.BlockSpec((MLA_HEADS, tm, 2 * V_HEAD), lambda b, i: (0, row(b, i), 0))],
        out_shape=[jax.ShapeDtypeStruct((MLA_HEADS, r, kd), bf16),
                   jax.ShapeDtypeStruct((MLA_HEADS, r, kd), bf16),
                   jax.ShapeDtypeStruct((MLA_HEADS, r, 2 * V_HEAD), bf16)],
        compiler_params=_params(("parallel", "arbitrary"), 48),
        name="mla_prep",
    )(u, u, u, w_uq_p, w_ukv_p, q_norm_g, kv_norm_g, *tables)


ATTN_HEADS_PER_STEP = 2


def _attn_kernel(q_ref, k_ref, v_ref, o_ref, s0_ref, s1_ref, p0_ref, p1_ref, *, tq):
    nh, lp, _ = k_ref.shape
    nq = lp // tq
    n_tiles = nh * nq
    s_refs, p_refs = (s0_ref, s1_ref), (p0_ref, p1_ref)

    def tile(x):
        return x // nq, slice((x % nq) * tq, (x % nq + 1) * tq)

    def scores(x):
        hh, rows = tile(x)
        s_refs[x % 2][...] = _dot_nt(q_ref[hh, rows, :], k_ref[hh])

    def softmax(x):
        s = s_refs[x % 2][...]
        p_refs[x % 2][...] = jnp.exp2(s - jnp.max(s, axis=1, keepdims=True)).astype(bf16)

    def weighted(x):
        hh, rows = tile(x)
        acc = _dot(p_refs[x % 2][...], v_ref[hh])
        o_ref[hh, rows, :] = (acc[:, :V_HEAD] * (1.0 / acc[:, V_HEAD:V_HEAD + 1])).astype(o_ref.dtype)

    for x in range(n_tiles + 2):
        if x >= 2:
            weighted(x - 2)
        if 1 <= x <= n_tiles:
            softmax(x - 1)
        if x < n_tiles:
            scores(x)


def _attention(qc, kc, v, bsz, lp):
    r = v.shape[1]
    tq = _pick(lp, (384, 128))
    kd = qc.shape[-1]
    nh = ATTN_HEADS_PER_STEP
    blk = lambda width: pl.BlockSpec((nh, lp, width), lambda b, h: (h, b, 0))
    return pl.pallas_call(
        functools.partial(_attn_kernel, tq=tq),
        grid=(bsz, MLA_HEADS // nh),
        in_specs=[blk(kd), blk(kd), blk(2 * V_HEAD)],
        out_specs=blk(V_HEAD),
        out_shape=jax.ShapeDtypeStruct((MLA_HEADS, r, V_HEAD), bf16),
        scratch_shapes=[pltpu.VMEM((tq, lp), f32), pltpu.VMEM((tq, lp), f32),
                        pltpu.VMEM((tq, lp), bf16), pltpu.VMEM((tq, lp), bf16)],
        compiler_params=_params(("parallel", "arbitrary"), 60),
        name="attention",
    )(qc, kc, v)


def _layer_norm(y, g, b):
    mu = jnp.mean(y, axis=-1, keepdims=True)
    var = jnp.mean(jnp.square(y - mu), axis=-1, keepdims=True)
    return (y - mu) * lax.rsqrt(var + EPS) * g + b


OUTPROJ_SUBTILES = 2


def _outproj_kernel(ml_ref, at_ref, w_ref, h_ref, g_ref, b_ref, o_ref, ob_ref, *, alpha):
    ts = h_ref.shape[0] // OUTPROJ_SUBTILES
    for sub in range(OUTPROJ_SUBTILES):
        rows = slice(sub * ts, (sub + 1) * ts)
        at = jnp.concatenate([at_ref[h, rows, :] for h in range(MLA_HEADS)], axis=1)
        mix = _dot(ml_ref[rows, :], w_ref[:ML_WIDTH, :]) + _dot(at, w_ref[ML_WIDTH:, :])
        y = _layer_norm(alpha * h_ref[rows, :] + mix, g_ref[...], b_ref[...])
        o_ref[rows, :] = y
        ob_ref[rows, :] = y.astype(bf16)


def _outproj(ml, at, w_out, h, ln_g, ln_b, layer, alpha):
    r, d = h.shape
    tm = _pick(r, (768, 384, 128))
    vec = pl.BlockSpec((None, 1, d), lambda i: (layer, 0, 0))
    return pl.pallas_call(
        functools.partial(_outproj_kernel, alpha=alpha),
        grid=(r // tm,),
        in_specs=[pl.BlockSpec((tm, ML_WIDTH), lambda i: (i, 0)),
                  pl.BlockSpec((MLA_HEADS, tm, V_HEAD), lambda i: (0, i, 0)),
                  pl.BlockSpec((None,) + w_out.shape[1:], lambda i: (layer, 0, 0), pipeline_mode=pl.Buffered(1)),
                  pl.BlockSpec((tm, d), lambda i: (i, 0)), vec, vec],
        out_specs=[pl.BlockSpec((tm, d), lambda i: (i, 0)), pl.BlockSpec((tm, d), lambda i: (i, 0))],
        out_shape=[jax.ShapeDtypeStruct((r, d), f32), jax.ShapeDtypeStruct((r, d), bf16)],
        compiler_params=_params(("parallel",), 56),
        name="outproj_ln",
    )(ml, at, w_out, h, ln_g, ln_b)


HALO = 16


def _ffn_up_kernel(prev_ref, main_ref, next_ref, wg_ref, wv_ref, cw_ref, cb_ref, o_ref, ext_scr, gate_scr, *, lp):
    tm = main_ref.shape[0]

    @pl.when(pl.program_id(2) == 0)
    def _():
        ext_scr[:HALO, :] = prev_ref[...]
        ext_scr[HALO:HALO + tm, :] = main_ref[...]
        ext_scr[HALO + tm:, :] = next_ref[...]

    rows = tm + 2 * HALO
    pos = pl.program_id(1) * tm - HALO + lax.broadcasted_iota(jnp.int32, (rows, 1), 0)
    inside = (pos >= FRONT) & (pos < lp)
    gate_scr[...] = jnp.where(inside, _dot(ext_scr[...], wg_ref[...]), 0.0)
    taps = [gate_scr[pl.ds(HALO - 1 + j, tm), :] * cw_ref[j:j + 1, :] for j in range(3)]
    conv = taps[0] + taps[1] + taps[2] + cb_ref[...]
    val = _dot(main_ref[...], wv_ref[...])
    o_ref[...] = (conv * jax.nn.sigmoid(conv) * val).astype(o_ref.dtype)


def _ffn_up(hb, w_up, conv_w, conv_b, layer, bsz, lp):
    r, d = hb.shape
    dff = conv_b.shape[-1]
    tm = _pick(lp, (1408, 384, 128))
    tn = _pick(dff, (512, 256, 128))
    nt, nj = lp // tm, dff // tn
    hblk = tm // HALO
    last = r // HALO - 1
    return pl.pallas_call(
        functools.partial(_ffn_up_kernel, lp=lp),
        grid=(bsz, nt, nj),
        in_specs=[pl.BlockSpec((HALO, d), lambda b, i, j: (jnp.maximum((b * nt + i) * hblk - 1, 0), 0)),
                  pl.BlockSpec((tm, d), lambda b, i, j: (b * nt + i, 0)),
                  pl.BlockSpec((HALO, d), lambda b, i, j: (jnp.minimum((b * nt + i + 1) * hblk, last), 0)),
                  pl.BlockSpec((None, d, tn), lambda b, i, j: (layer, 0, j)),
                  pl.BlockSpec((None, d, tn), lambda b, i, j: (layer, 0, nj + j)),
                  pl.BlockSpec((None, 3, tn), lambda b, i, j: (layer, 0, j)),
                  pl.BlockSpec((None, 1, tn), lambda b, i, j: (layer, 0, j))],
        out_specs=pl.BlockSpec((tm, tn), lambda b, i, j: (b * nt + i, j)),
        out_shape=jax.ShapeDtypeStruct((r, dff), bf16),
        scratch_shapes=[pltpu.VMEM((tm + 2 * HALO, d), bf16), pltpu.VMEM((tm + 2 * HALO, tn), f32)],
        compiler_params=_params(("parallel", "parallel", "arbitrary"), 56),
        name="ffn_up_conv",
    )(hb, hb, hb, w_up, w_up, conv_w, conv_b)


FFN_DOWN_SUBTILES = 2


def _ffn_down_kernel(a_ref, w_ref, h_ref, g_ref, b_ref, o_ref, ob_ref, *, alpha):
    ts = h_ref.shape[0] // FFN_DOWN_SUBTILES
    for sub in range(FFN_DOWN_SUBTILES):
        rows = slice(sub * ts, (sub + 1) * ts)
        y = _layer_norm(alpha * h_ref[rows, :] + _dot(a_ref[rows, :], w_ref[...]), g_ref[...], b_ref[...])
        o_ref[rows, :] = y
        ob_ref[rows, :] = y.astype(bf16)


def _ffn_down(act, w_down, h, ln_g, ln_b, layer, alpha):
    r, d = h.shape
    dff = act.shape[1]
    tm = _pick(r, (384, 128))
    vec = pl.BlockSpec((None, 1, d), lambda i: (layer, 0, 0))
    return pl.pallas_call(
        functools.partial(_ffn_down_kernel, alpha=alpha),
        grid=(r // tm,),
        in_specs=[pl.BlockSpec((tm, dff), lambda i: (i, 0)),
                  pl.BlockSpec((None, dff, d), lambda i: (layer, 0, 0), pipeline_mode=pl.Buffered(1)),
                  pl.BlockSpec((tm, d), lambda i: (i, 0)), vec, vec],
        out_specs=[pl.BlockSpec((tm, d), lambda i: (i, 0)), pl.BlockSpec((tm, d), lambda i: (i, 0))],
        out_shape=[jax.ShapeDtypeStruct((r, d), f32), jax.ShapeDtypeStruct((r, d), bf16)],
        compiler_params=_params(("parallel",), 60),
        name="ffn_down_ln",
    )(act, w_down, h, ln_g, ln_b)


def _rope_tables(lp):
    inv_freq = ROPE_THETA ** (-jnp.arange(0, QK_ROPE, 2, dtype=f32) / QK_ROPE)
    pos = jnp.maximum(jnp.arange(lp, dtype=f32) - FRONT, 0.0)
    ang = pos[:, None] * inv_freq[None, :]
    cos, sin = jnp.cos(ang), jnp.sin(ang)
    zero = jnp.zeros_like(sin)
    reps = LANE // QK_ROPE
    cos_t = jnp.tile(jnp.concatenate([cos, cos], axis=1), (1, reps))
    sin_a = jnp.tile(jnp.concatenate([-sin, zero], axis=1), (1, reps))
    sin_b = jnp.tile(jnp.concatenate([zero, sin], axis=1), (1, reps))
    return cos_t, sin_a, sin_b


def kernel(x_prompt, x_sample, meta_tokens, w_in, b_gates, ml_norm_g, q_norm_g, kv_norm_g, w_uq, w_ukv,
           w_out, ln1_g, ln1_b, w_up, conv_w, conv_b, w_down, ln2_g, ln2_b):
    depth, d, _ = w_in.shape
    assert x_prompt.shape[1:] == x_sample.shape[1:]
    n_prompt = x_prompt.shape[0]
    x = jnp.concatenate([x_prompt, x_sample], axis=0)
    bsz, seq, _ = x.shape
    assert seq % LANE == 0
    lp = LANE + seq
    alpha = (2 * depth) ** 0.25

    gate_lo = 4 * ML_WIDTH
    w_main = jnp.concatenate(
        [w_in[..., :ML_WIDTH], w_in[..., 2 * ML_WIDTH:gate_lo], w_in[..., gate_lo + N_GATES:], w_in[..., -QK_ROPE:],
         jnp.zeros((depth, d, N_MAIN - COL_KR - 2 * QK_ROPE), w_in.dtype)], axis=-1).astype(bf16)
    wk_t = jnp.swapaxes(w_in[..., ML_WIDTH:2 * ML_WIDTH], 1, 2).astype(bf16)
    h4 = ML_HEADS
    perm = jnp.array(list(range(0, h4)) + list(range(2 * h4, 3 * h4))
                     + list(range(h4, 2 * h4)) + list(range(3 * h4, 4 * h4)))
    wg_t = jnp.swapaxes(w_in[..., gate_lo:gate_lo + N_GATES], 1, 2)[:, perm, :].astype(bf16)
    bg = b_gates[:, perm, None].astype(f32)
    uq = w_uq.reshape(depth, Q_LORA, MLA_HEADS, QK_NOPE + QK_ROPE)
    w_uq_p = jnp.concatenate([uq[..., :QK_NOPE].reshape(depth, Q_LORA, -1),
                              uq[..., QK_NOPE:].reshape(depth, Q_LORA, -1)], axis=-1).astype(bf16)
    ukv = w_ukv.reshape(depth, KV_LORA, MLA_HEADS, QK_NOPE + V_HEAD)
    w_ukv_p = jnp.concatenate([ukv[..., :QK_NOPE].reshape(depth, KV_LORA, -1),
                               ukv[..., QK_NOPE:].reshape(depth, KV_LORA, -1)], axis=-1).astype(bf16)
    w_out_b, w_up_b, w_down_b = w_out.astype(bf16), w_up.astype(bf16), w_down.astype(bf16)
    vec3 = lambda a: a[:, None, :].astype(f32)
    ml_g, q_g, kv_g = vec3(ml_norm_g), vec3(q_norm_g), vec3(kv_norm_g)
    l1g, l1b, l2g, l2b, cb = vec3(ln1_g), vec3(ln1_b), vec3(ln2_g), vec3(ln2_b), vec3(conv_b)
    cw = conv_w.astype(f32)
    tables = _rope_tables(lp)

    meta = jnp.broadcast_to(meta_tokens.astype(x.dtype)[None], (bsz, N_META, d))
    h = jnp.concatenate([jnp.zeros((bsz, FRONT, d), x.dtype), meta, x], axis=1).reshape(bsz * lp, d)
    hb = h.astype(bf16)
    for layer in range(depth):
        u = _inproj(hb, w_main, layer)
        gq = _gates(hb, wg_t, bg, layer, bsz, lp)
        kt = _keys_t(hb, wk_t, layer)
        ml = _mlstm(u, kt, gq, ml_g, layer, bsz, lp)
        qc, kc, v = _mla_prep(u, w_uq_p, w_ukv_p, q_g, kv_g, tables, layer, bsz, lp)
        at = _attention(qc, kc, v, bsz, lp)
        h, hb = _outproj(ml, at, w_out_b, h, l1g, l1b, layer, alpha)
        act = _ffn_up(hb, w_up_b, cw, cb, layer, bsz, lp)
        h, hb = _ffn_down(act, w_down_b, h, l2g, l2b, layer, alpha)
    y = h.reshape(bsz, lp, d)[:, LANE:]
    return (y[:n_prompt], y[n_prompt:])
```

```python
import functools
import math

import jax
import jax.numpy as jnp
from jax import lax
from jax.experimental import pallas as pl
from jax.experimental.pallas import tpu as pltpu

N_META = 16
ML_HEADS = 4
ML_HEAD_DIM = 256
ML_WIDTH = ML_HEADS * ML_HEAD_DIM
N_GATES = 4 * ML_HEADS
MLA_HEADS = 8
Q_LORA = 512
KV_LORA = 256
QK_NOPE = 128
QK_ROPE = 64
V_HEAD = 128
ROPE_THETA = 10000.0
EPS = 1e-5
NEG = -1e30

LANE = 128
FRONT = LANE - N_META
CHUNK = 128
ROPE_HALF = QK_ROPE // 2
SOFTMAX_SCALE_LOG2 = (QK_NOPE + QK_ROPE) ** -0.5 * math.log2(math.e)
MIB = 1024 * 1024

COL_Q, COL_V, COL_O = 0, ML_WIDTH, 2 * ML_WIDTH
COL_DQ = 3 * ML_WIDTH
COL_DKV = COL_DQ + Q_LORA
COL_KR = COL_DKV + KV_LORA
N_MAIN = COL_KR + 2 * QK_ROPE + LANE

f32 = jnp.float32
bf16 = jnp.bfloat16


def _pick(n, cands):
    for c in cands:
        if n % c == 0:
            return c
    raise ValueError(f"no tile for {n} in {cands}")


def _params(sem, vmem_mib):
    return pltpu.CompilerParams(dimension_semantics=sem, vmem_limit_bytes=vmem_mib * MIB)


def _dot(a, b):
    return jnp.dot(a, b, preferred_element_type=f32)


def _dot_nt(a, b):
    return lax.dot_general(a, b, (((1,), (1,)), ((), ())), preferred_element_type=f32)


def _inproj_kernel(a_ref, w_ref, o_ref, *, tiles_per_seq):
    first = pl.program_id(0) % tiles_per_seq == 0

    @pl.when(first)
    def _():
        o_ref[:FRONT, :] = jnp.zeros((FRONT, o_ref.shape[1]), o_ref.dtype)
        o_ref[FRONT:, :] = _dot(a_ref[FRONT:, :], w_ref[...]).astype(o_ref.dtype)

    @pl.when(jnp.logical_not(first))
    def _():
        o_ref[...] = _dot(a_ref[...], w_ref[...]).astype(o_ref.dtype)


def _keys_t_kernel(w_ref, h_ref, o_ref):
    o_ref[...] = (_dot_nt(w_ref[...], h_ref[...]) * ML_HEAD_DIM ** -0.5).astype(o_ref.dtype)


def _keys_t(hb, wk_t, layer):
    r, d = hb.shape
    tm = _pick(r, (1536, 384, 128))
    return pl.pallas_call(
        _keys_t_kernel,
        grid=(r // tm,),
        in_specs=[pl.BlockSpec((None, ML_WIDTH, d), lambda i: (layer, 0, 0)),
                  pl.BlockSpec((tm, d), lambda i: (i, 0))],
        out_specs=pl.BlockSpec((ML_WIDTH, tm), lambda i: (0, i)),
        out_shape=jax.ShapeDtypeStruct((ML_WIDTH, r), bf16),
        compiler_params=_params(("parallel",), 48),
        name="keys_t",
    )(wk_t, hb)


def _inproj(hb, w_main, layer, lp):
    r, d = hb.shape
    n = w_main.shape[-1]
    tm = _pick(lp, (1408, 384, 128))
    tn = _pick(n, (1024, 512, 256, 128))
    return pl.pallas_call(
        functools.partial(_inproj_kernel, tiles_per_seq=lp // tm),
        grid=(r // tm, n // tn),
        in_specs=[pl.BlockSpec((tm, d), lambda i, j: (i, 0)),
                  pl.BlockSpec((None, d, tn), lambda i, j: (layer, 0, j))],
        out_specs=pl.BlockSpec((tm, tn), lambda i, j: (i, j)),
        out_shape=jax.ShapeDtypeStruct((r, n), bf16),
        compiler_params=_params(("parallel", "arbitrary"), 48),
        name="inproj",
    )(hb, w_main)


def _gates_kernel(h_ref, wg_ref, bg_ref, o_ref):
    tm = h_ref.shape[0]
    half = N_GATES // 2
    g = _dot_nt(wg_ref[...], h_ref[...]) + bg_ref[...]
    pos = pl.program_id(1) * tm + lax.broadcasted_iota(jnp.int32, (half, tm), 1)
    unused = pos < FRONT
    gi, gf = g[:half], g[half:]
    li = jnp.where(unused, NEG, gi)
    lf = jnp.where(unused, 0.0, jnp.minimum(gf, 0.0) - jnp.log1p(jnp.exp(-jnp.abs(gf))))
    lane = lax.broadcasted_iota(jnp.int32, (half, CHUNK), 1)
    is_fwd = lax.broadcasted_iota(jnp.int32, (half, CHUNK), 0) < ML_HEADS
    for c in range(tm // CHUNK):
        sl = slice(c * CHUNK, (c + 1) * CHUNK)
        pre = suf = lf[:, sl]
        k = 1
        while k < CHUNK:
            pre = pre + jnp.where(lane >= k, pltpu.roll(pre, k, 1), 0.0)
            suf = suf + jnp.where(lane < CHUNK - k, pltpu.roll(suf, CHUNK - k, 1), 0.0)
            k *= 2
        b = jnp.where(is_fwd, pre, suf)
        o_ref[:half, sl] = li[:, sl] - b
        o_ref[half:, sl] = b


def _gates(hb, wg_t, bg, layer, bsz, lp):
    r, d = hb.shape
    tm = _pick(lp, (1408, 384, 128))
    nt = lp // tm
    return pl.pallas_call(
        _gates_kernel,
        grid=(bsz, nt),
        in_specs=[pl.BlockSpec((tm, d), lambda b, i: (b * nt + i, 0)),
                  pl.BlockSpec((None, N_GATES, d), lambda b, i: (layer, 0, 0)),
                  pl.BlockSpec((None, N_GATES, 1), lambda b, i: (layer, 0, 0))],
        out_specs=pl.BlockSpec((None, N_GATES, tm), lambda b, i: (b, 0, i)),
        out_shape=jax.ShapeDtypeStruct((bsz, N_GATES, lp), f32),
        compiler_params=_params(("parallel", "arbitrary"), 32),
        name="gates",
    )(hb, wg_t, bg)


MLSTM_UNROLL = 16


def _mlstm_kernel(q_ref, kt_ref, v_ref, o_ref, gq_ref, ng_ref, out_ref, hf_scr, hb_scr, c_scr, m_scr, s_scr):
    head = pl.program_id(1)
    lp, dh = q_ref.shape
    t = CHUNK
    nc = lp // t
    c_scr[...] = jnp.zeros_like(c_scr)
    m_scr[...] = jnp.zeros_like(m_scr)
    ti = lax.broadcasted_iota(jnp.int32, (t, t), 0)
    si = lax.broadcasted_iota(jnp.int32, (t, t), 1)
    masks = (si <= ti, si >= ti)
    eye = si == ti
    half = N_GATES // 2
    gate_row = lax.broadcasted_iota(jnp.int32, (half, t), 0)
    ones_col = jnp.where(lax.broadcasted_iota(jnp.int32, (t, LANE), 1) == 0, 1.0, 0.0).astype(bf16)

    def qk_ahead(c, dirn):
        base = pl.multiple_of(jnp.clip(c, 0, nc - 1) * t, t)
        s_scr[dirn] = _dot(q_ref[pl.ds(base, t), :], kt_ref[:, pl.ds(base, t)])

    def chunk(c, dirn, h_scr):
        base = pl.multiple_of(c * t, t)
        q = q_ref[pl.ds(base, t), :]
        kt = kt_ref[:, pl.ds(base, t)]
        v = jnp.concatenate([v_ref[pl.ds(base, t), :], ones_col], axis=1)
        mine = gate_row == head + ML_HEADS * dirn
        r_row = jnp.sum(jnp.where(mine, gq_ref[:half, pl.ds(base, t)], 0.0), axis=0, keepdims=True)
        b_row = jnp.sum(jnp.where(mine, gq_ref[half:, pl.ds(base, t)], 0.0), axis=0, keepdims=True)
        m_state = m_scr[dirn, 0:1, 0:1]
        mask = masks[dirn]
        m_col = jnp.maximum(jnp.max(jnp.where(mask, r_row, -jnp.inf), axis=1, keepdims=True), m_state)
        b_col = jnp.sum(jnp.where(eye, b_row, 0.0), axis=1, keepdims=True)
        m_all = jnp.maximum(jnp.max(r_row, axis=1, keepdims=True), m_state)
        b_last = b_row[:, 0:1] if dirn else b_row[:, t - 1:t]
        s = s_scr[dirn] * jnp.exp(jnp.where(mask, r_row - m_col, -jnp.inf))
        both = jnp.exp(m_state - m_col) * _dot(q, c_scr[dirn].astype(bf16)) + _dot(s.astype(bf16), v)
        floor = jnp.exp(-(b_col + m_col))
        h_scr[pl.ds(base, t), :] = both[:, :dh] / jnp.maximum(jnp.abs(both[:, dh:dh + 1]), floor)
        ktw = (kt.astype(f32) * jnp.exp(r_row - m_all)).astype(bf16)
        c_scr[dirn] = jnp.exp(m_state - m_all) * c_scr[dirn] + _dot(ktw, v)
        m_scr[dirn] = jnp.broadcast_to(b_last + m_all, m_scr.shape[1:])

    def finish(c):
        base = pl.multiple_of(c * t, t)
        hs = hf_scr[pl.ds(base, t), :] + hb_scr[pl.ds(base, t), :]
        mu = jnp.mean(hs, axis=1, keepdims=True)
        var = jnp.mean(jnp.square(hs - mu), axis=1, keepdims=True)
        hn = (hs - mu) * lax.rsqrt(var + EPS) * ng_ref[...]
        gate = jax.nn.sigmoid(o_ref[pl.ds(base, t), :].astype(f32))
        out_ref[pl.ds(base, t), :] = (gate * hn).astype(out_ref.dtype)

    def scan_step(c):
        chunk(c, 0, hf_scr)
        chunk(nc - 1 - c, 1, hb_scr)
        qk_ahead(c + 1, 0)
        qk_ahead(nc - 2 - c, 1)

    def first_half(c, carry):
        scan_step(c)
        return carry

    def second_half(c, carry):
        finish(c - 1)
        finish(nc - c)
        scan_step(c)
        return carry

    qk_ahead(0, 0)
    qk_ahead(nc - 1, 1)
    meet = (nc + 1) // 2
    lax.fori_loop(0, meet, first_half, 0, unroll=MLSTM_UNROLL)
    lax.fori_loop(meet, nc, second_half, 0, unroll=MLSTM_UNROLL)
    finish(nc - 1)
    finish(0)


def _mlstm(u, kt, gq, ml_norm_g, layer, bsz, lp):
    r = u.shape[0]
    dh = ML_HEAD_DIM

    def col(first):
        return pl.BlockSpec((lp, dh), lambda b, h: (b, first // dh + h))

    return pl.pallas_call(
        _mlstm_kernel,
        grid=(bsz, ML_HEADS),
        in_specs=[col(COL_Q), pl.BlockSpec((dh, lp), lambda b, h: (h, b)), col(COL_V), col(COL_O),
                  pl.BlockSpec((None, N_GATES, lp), lambda b, h: (b, 0, 0)),
                  pl.BlockSpec((None, 1, dh), lambda b, h: (layer, 0, h))],
        out_specs=pl.BlockSpec((lp, dh), lambda b, h: (b, h)),
        out_shape=jax.ShapeDtypeStruct((r, ML_WIDTH), bf16),
        scratch_shapes=[pltpu.VMEM((lp, dh), f32), pltpu.VMEM((lp, dh), f32),
                        pltpu.VMEM((2, dh, dh + LANE), f32), pltpu.VMEM((2, 8, LANE), f32),
                        pltpu.VMEM((2, CHUNK, CHUNK), f32)],
        compiler_params=_params(("parallel", "arbitrary"), 48),
        name="mlstm",
    )(u, kt, u, u, gq, ml_norm_g)


def _rms(x, g):
    return x * lax.rsqrt(jnp.mean(jnp.square(x), axis=-1, keepdims=True) + EPS) * g


def _rope(x, cos, sin_a, sin_b):
    n = x.shape[1]
    reps = n // LANE
    if reps > 1:
        cos, sin_a, sin_b = (jnp.concatenate([a] * reps, axis=1) for a in (cos, sin_a, sin_b))
    return x * cos + pltpu.roll(x, n - ROPE_HALF, 1) * sin_a + pltpu.roll(x, ROPE_HALF, 1) * sin_b


def _mla_prep_kernel(dq_ref, dkv_ref, kr_ref, wuq_ref, wukv_ref, qg_ref, kvg_ref,
                     cos_ref, sa_ref, sb_ref, qc_ref, kc_ref, v_ref):
    cos, sa, sb = cos_ref[...], sa_ref[...], sb_ref[...]
    q = _dot(_rms(dq_ref[...].astype(f32), qg_ref[...]).astype(bf16), wuq_ref[...]) * SOFTMAX_SCALE_LOG2
    kv =_dot(_rms(dkv_ref[...].astype(f32), kvg_ref[...]).astype(bf16), wukv_ref[...])
    q_rope = _rope(q[:, MLA_HEADS * QK_NOPE:], cos, sa, sb)
    k_rope = _rope(kr_ref[...].astype(f32), cos, sa, sb)
    tm = q.shape[0]
    lane = lax.broadcasted_iota(jnp.int32, (tm, LANE), 1)
    pos = pl.program_id(1) * tm + lax.broadcasted_iota(jnp.int32, (tm, LANE), 0)
    low = lane < QK_ROPE
    marker = lane == QK_ROPE
    one_hot = jnp.where(marker, 1.0, 0.0)
    k_tail = jnp.where(low, k_rope, jnp.where(marker & (pos < FRONT), -jnp.inf, 0.0)).astype(bf16)
    ones_col = jnp.where(lane == 0, 1.0, 0.0).astype(bf16)
    for h in range(MLA_HEADS):
        pair = q_rope[:, (h // 2) * LANE:(h // 2 + 1) * LANE]
        if h % 2:
            pair = pltpu.roll(pair, QK_ROPE, 1)
        qc_ref[h, :, :QK_NOPE] = q[:, h * QK_NOPE:(h + 1) * QK_NOPE].astype(bf16)
        qc_ref[h, :, QK_NOPE:] = jnp.where(low, pair, one_hot).astype(bf16)
        kc_ref[h, :, :QK_NOPE] = kv[:, h * QK_NOPE:(h + 1) * QK_NOPE].astype(bf16)
        kc_ref[h, :, QK_NOPE:] = k_tail
        v_ref[h, :, :V_HEAD] = kv[:, (MLA_HEADS + h) * V_HEAD:(MLA_HEADS + h + 1) * V_HEAD].astype(bf16)
        v_ref[h, :, V_HEAD:] = ones_col


def _mla_prep(u, w_uq_p, w_ukv_p, q_norm_g, kv_norm_g, tables, layer, bsz, lp):
    r = u.shape[0]
    tm = _pick(lp, (384, 128))
    nt = lp // tm
    kd = QK_NOPE + LANE
    tab = pl.BlockSpec((tm, LANE), lambda b, i: (i, 0))
    row = lambda b, i: b * nt + i
    return pl.pallas_call(
        _mla_prep_kernel,
        grid=(bsz, nt),
        in_specs=[pl.BlockSpec((tm, Q_LORA), lambda b, i: (row(b, i), COL_DQ // Q_LORA)),
                  pl.BlockSpec((tm, KV_LORA), lambda b, i: (row(b, i), COL_DKV // KV_LORA)),
                  pl.BlockSpec((tm, LANE), lambda b, i: (row(b, i), COL_KR // LANE)),
                  pl.BlockSpec((None,) + w_uq_p.shape[1:], lambda b, i: (layer, 0, 0)),
                  pl.BlockSpec((None,) + w_ukv_p.shape[1:], lambda b, i: (layer, 0, 0)),
                  pl.BlockSpec((None, 1, Q_LORA), lambda b, i: (layer, 0, 0)),
                  pl.BlockSpec((None, 1, KV_LORA), lambda b, i: (layer, 0, 0)),
                  tab, tab, tab],
        out_specs=[pl.BlockSpec((MLA_HEADS, tm, kd), lambda b, i: (0, row(b, i), 0)),
                   pl.BlockSpec((MLA_HEADS, tm, kd), lambda b, i: (0, row(b, i), 0)),
                   pl.BlockSpec((MLA_HEADS, tm, 2 * V_HEAD), lambda b, i: (0, row(b, i), 0))],
        out_shape=[jax.ShapeDtypeStruct((MLA_HEADS, r, kd), bf16),
                   jax.ShapeDtypeStruct((MLA_HEADS, r, kd), bf16),
                   jax.ShapeDtypeStruct((MLA_HEADS, r, 2 * V_HEAD), bf16)],
        compiler_params=_params(("parallel", "arbitrary"), 48),
        name="mla_prep",
    )(u, u, u, w_uq_p, w_ukv_p, q_norm_g, kv_norm_g, *tables)


ATTN_HEADS_PER_STEP = 2
ATTN_TILES_PER_BODY = 12


def _attn_kernel(q_ref, k_ref, v_ref, o_ref, s0_ref, s1_ref, p0_ref, p1_ref, *, tq):
    nh, lp, _ = k_ref.shape
    nq = lp // tq
    n_tiles = nh * nq
    s_refs, p_refs = (s0_ref, s1_ref), (p0_ref, p1_ref)

    @pl.when((pl.program_id(0) == 0) & (pl.program_id(1) == 0))
    def _():
        for ref in s_refs + p_refs:
            ref[...] = jnp.zeros_like(ref)

    def tile(x):
        x = jnp.clip(x, 0, n_tiles - 1)
        hh = x // nq
        return hh, pl.ds(pl.multiple_of((x - hh * nq) * tq, tq), tq)

    def scores(x, slot):
        hh, rows = tile(x)
        s_refs[slot][...] = _dot_nt(q_ref[hh, rows, :], k_ref[hh])

    def softmax(slot):
        s = s_refs[slot][...]
        p_refs[slot][...] = jnp.exp2(s - jnp.max(s, axis=1, keepdims=True)).astype(bf16)

    def weighted(x, slot):
        hh, rows = tile(x)
        acc = _dot(p_refs[slot][...], v_ref[hh])
        o_ref[hh, rows, :] = (acc[:, :V_HEAD] * (1.0 / acc[:, V_HEAD:V_HEAD + 1])).astype(o_ref.dtype)

    n_bodies = -(-(n_tiles + 2) // ATTN_TILES_PER_BODY)
    start = n_tiles + 2 - ATTN_TILES_PER_BODY * n_bodies

    def body(u, carry):
        for e in range(ATTN_TILES_PER_BODY):
            x = ATTN_TILES_PER_BODY * u + e + start
            weighted(x - 2, e % 2)
            softmax((e + 1) % 2)
            scores(x, e % 2)
        return carry

    lax.fori_loop(0, n_bodies, body, 0)


def _attention(qc, kc, v, bsz, lp):
    r = v.shape[1]
    tq = _pick(lp, (384, 128))
    kd = qc.shape[-1]
    nh = ATTN_HEADS_PER_STEP
    blk = lambda width: pl.BlockSpec((nh, lp, width), lambda b, h: (h, b, 0))
    return pl.pallas_call(
        functools.partial(_attn_kernel, tq=tq),
        grid=(bsz, MLA_HEADS // nh),
        in_specs=[blk(kd), blk(kd), blk(2 * V_HEAD)],
        out_specs=blk(V_HEAD),
        out_shape=jax.ShapeDtypeStruct((MLA_HEADS, r, V_HEAD), bf16),
        scratch_shapes=[pltpu.VMEM((tq, lp), f32), pltpu.VMEM((tq, lp), f32),
                        pltpu.VMEM((tq, lp), bf16), pltpu.VMEM((tq, lp), bf16)],
        compiler_params=_params(("arbitrary", "arbitrary"), 60),
        name="attention",
    )(qc, kc, v)


def _layer_norm(y, g, b):
    mu = jnp.mean(y, axis=-1, keepdims=True)
    var = jnp.mean(jnp.square(y - mu), axis=-1, keepdims=True)
    return (y - mu) * lax.rsqrt(var + EPS) * g + b


OUTPROJ_SUBTILES = 2


def _outproj_kernel(ml_ref, at_ref, w_ref, h_ref, g_ref, b_ref, o_ref, ob_ref, *, alpha):
    ts = h_ref.shape[0] // OUTPROJ_SUBTILES
    for sub in range(OUTPROJ_SUBTILES):
        rows = slice(sub * ts, (sub + 1) * ts)
        at = jnp.concatenate([at_ref[h, rows, :] for h in range(MLA_HEADS)], axis=1)
        mix = _dot(ml_ref[rows, :], w_ref[:ML_WIDTH, :]) + _dot(at, w_ref[ML_WIDTH:, :])
        y = _layer_norm(alpha * h_ref[rows, :] + mix, g_ref[...], b_ref[...])
        o_ref[rows, :] = y
        ob_ref[rows, :] = y.astype(bf16)


def _outproj(ml, at, w_out, h, ln_g, ln_b, layer, alpha):
    r, d = h.shape
    tm = _pick(r, (768, 384, 128))
    vec = pl.BlockSpec((None, 1, d), lambda i: (layer, 0, 0))
    return pl.pallas_call(
        functools.partial(_outproj_kernel, alpha=alpha),
        grid=(r // tm,),
        in_specs=[pl.BlockSpec((tm, ML_WIDTH), lambda i: (i, 0)),
                  pl.BlockSpec((MLA_HEADS, tm, V_HEAD), lambda i: (0, i, 0)),
                  pl.BlockSpec((None,) + w_out.shape[1:], lambda i: (layer, 0, 0), pipeline_mode=pl.Buffered(1)),
                  pl.BlockSpec((tm, d), lambda i: (i, 0)), vec, vec],
        out_specs=[pl.BlockSpec((tm, d), lambda i: (i, 0)), pl.BlockSpec((tm, d), lambda i: (i, 0))],
        out_shape=[jax.ShapeDtypeStruct((r, d), f32), jax.ShapeDtypeStruct((r, d), bf16)],
        compiler_params=_params(("parallel",), 56),
        name="outproj_ln",
    )(ml, at, w_out, h, ln_g, ln_b)


HALO = 16


def _ffn_up_kernel(prev_ref, main_ref, next_ref, wg_ref, wv_ref, cw_ref, cb_ref, o_ref, ext_scr, gate_scr, *, lp):
    tm = main_ref.shape[0]

    @pl.when(pl.program_id(2) == 0)
    def _():
        ext_scr[:HALO, :] = prev_ref[...]
        ext_scr[HALO:HALO + tm, :] = main_ref[...]
        ext_scr[HALO + tm:, :] = next_ref[...]

    def compute(lo):
        pos = pl.program_id(1) * tm - HALO + lo + lax.broadcasted_iota(jnp.int32, (tm + 2 * HALO - lo, 1), 0)
        inside = (pos >= FRONT) & (pos < lp)
        gate_scr[lo:, :] = jnp.where(inside, _dot(ext_scr[lo:, :], wg_ref[...]), 0.0)
        taps = [gate_scr[pl.ds(lo + HALO - 1 + j, tm - lo), :] * cw_ref[j:j + 1, :] for j in range(3)]
        conv = taps[0] + taps[1] + taps[2] + cb_ref[...]
        val = _dot(main_ref[lo:, :], wv_ref[...])
        o_ref[lo:, :] = (conv * jax.nn.sigmoid(conv) * val).astype(o_ref.dtype)

    @pl.when(pl.program_id(1) == 0)
    def _():
        o_ref[:FRONT, :] = jnp.zeros((FRONT, o_ref.shape[1]), o_ref.dtype)
        compute(FRONT)

    @pl.when(pl.program_id(1) != 0)
    def _():
        compute(0)


def _ffn_up(hb, w_up, conv_w, conv_b, layer, bsz, lp):
    r, d = hb.shape
    dff = conv_b.shape[-1]
    tm = _pick(lp, (1408, 384, 128))
    tn = _pick(dff, (512, 256, 128))
    nt, nj = lp // tm, dff // tn
    hblk = tm // HALO
    last = r // HALO - 1
    return pl.pallas_call(
        functools.partial(_ffn_up_kernel, lp=lp),
        grid=(bsz, nt, nj),
        in_specs=[pl.BlockSpec((HALO, d), lambda b, i, j: (jnp.maximum((b * nt + i) * hblk - 1, 0), 0)),
                  pl.BlockSpec((tm, d), lambda b, i, j: (b * nt + i, 0)),
                  pl.BlockSpec((HALO, d), lambda b, i, j: (jnp.minimum((b * nt + i + 1) * hblk, last), 0)),
                  pl.BlockSpec((None, d, tn), lambda b, i, j: (layer, 0, j)),
                  pl.BlockSpec((None, d, tn), lambda b, i, j: (layer, 0, nj + j)),
                  pl.BlockSpec((None, 3, tn), lambda b, i, j: (layer, 0, j)),
                  pl.BlockSpec((None, 1, tn), lambda b, i, j: (layer, 0, j))],
        out_specs=pl.BlockSpec((tm, tn), lambda b, i, j: (b * nt + i, j)),
        out_shape=jax.ShapeDtypeStruct((r, dff), bf16),
        scratch_shapes=[pltpu.VMEM((tm + 2 * HALO, d), bf16), pltpu.VMEM((tm + 2 * HALO, tn), f32)],
        compiler_params=_params(("parallel", "parallel", "arbitrary"), 56),
        name="ffn_up_conv",
    )(hb, hb, hb, w_up, w_up, conv_w, conv_b)


FFN_DOWN_SUBTILES = 2


def _ffn_down_kernel(a_ref, w_ref, h_ref, g_ref, b_ref, o_ref, ob_ref, *, alpha, tiles_per_seq):
    tm = h_ref.shape[0]

    def compute(rows):
        y = _layer_norm(alpha * h_ref[rows, :] + _dot(a_ref[rows, :], w_ref[...]), g_ref[...], b_ref[...])
        o_ref[rows, :] = y
        ob_ref[rows, :] = y.astype(bf16)

    first = pl.program_id(0) % tiles_per_seq == 0

    @pl.when(first)
    def _():
        o_ref[:FRONT, :] = jnp.zeros((FRONT, o_ref.shape[1]), o_ref.dtype)
        ob_ref[:FRONT, :] = jnp.zeros((FRONT, ob_ref.shape[1]), ob_ref.dtype)
        compute(slice(FRONT, tm))

    @pl.when(jnp.logical_not(first))
    def _():
        ts = tm // FFN_DOWN_SUBTILES
        for sub in range(FFN_DOWN_SUBTILES):
            compute(slice(sub * ts, (sub + 1) * ts))


def _ffn_down(act, w_down, h, ln_g, ln_b, layer, alpha, lp):
    r, d = h.shape
    dff = act.shape[1]
    tm = _pick(lp, (384, 128))
    vec = pl.BlockSpec((None, 1, d), lambda i: (layer, 0, 0))
    return pl.pallas_call(
        functools.partial(_ffn_down_kernel, alpha=alpha, tiles_per_seq=lp // tm),
        grid=(r // tm,),
        in_specs=[pl.BlockSpec((tm, dff), lambda i: (i, 0)),
                  pl.BlockSpec((None, dff, d), lambda i: (layer, 0, 0), pipeline_mode=pl.Buffered(1)),
                  pl.BlockSpec((tm, d), lambda i: (i, 0)), vec, vec],
        out_specs=[pl.BlockSpec((tm, d), lambda i: (i, 0)), pl.BlockSpec((tm, d), lambda i: (i, 0))],
        out_shape=[jax.ShapeDtypeStruct((r, d), f32), jax.ShapeDtypeStruct((r, d), bf16)],
        compiler_params=_params(("parallel",), 60),
        name="ffn_down_ln",
    )(act, w_down, h, ln_g, ln_b)


def _rope_tables(lp):
    inv_freq = ROPE_THETA ** (-jnp.arange(0, QK_ROPE, 2, dtype=f32) / QK_ROPE)
    pos = jnp.maximum(jnp.arange(lp, dtype=f32) - FRONT, 0.0)
    ang = pos[:, None] * inv_freq[None, :]
    cos, sin = jnp.cos(ang), jnp.sin(ang)
    zero = jnp.zeros_like(sin)
    reps = LANE // QK_ROPE
    cos_t = jnp.tile(jnp.concatenate([cos, cos], axis=1), (1, reps))
    sin_a = jnp.tile(jnp.concatenate([-sin, zero], axis=1), (1, reps))
    sin_b = jnp.tile(jnp.concatenate([zero, sin], axis=1), (1, reps))
    return cos_t, sin_a, sin_b


def kernel(x_prompt, x_sample, meta_tokens, w_in, b_gates, ml_norm_g, q_norm_g, kv_norm_g, w_uq, w_ukv,
           w_out, ln1_g, ln1_b, w_up, conv_w, conv_b, w_down, ln2_g, ln2_b):
    depth, d, _ = w_in.shape
    assert x_prompt.shape[1:] == x_sample.shape[1:]
    n_prompt = x_prompt.shape[0]
    x = jnp.concatenate([x_prompt, x_sample], axis=0)
    bsz, seq, _ = x.shape
    assert seq % LANE == 0
    lp = LANE + seq
    alpha = (2 * depth) ** 0.25

    gate_lo = 4 * ML_WIDTH
    w_main = jnp.concatenate(
        [w_in[..., :ML_WIDTH], w_in[..., 2 * ML_WIDTH:gate_lo], w_in[..., gate_lo + N_GATES:], w_in[..., -QK_ROPE:],
         jnp.zeros((depth, d, N_MAIN - COL_KR - 2 * QK_ROPE), w_in.dtype)], axis=-1).astype(bf16)
    wk_t = jnp.swapaxes(w_in[..., ML_WIDTH:2 * ML_WIDTH], 1, 2).astype(bf16)
    h4 = ML_HEADS
    perm = jnp.array(list(range(0, h4)) + list(range(2 * h4, 3 * h4))
                     + list(range(h4, 2 * h4)) + list(range(3 * h4, 4 * h4)))
    wg_t = jnp.swapaxes(w_in[..., gate_lo:gate_lo + N_GATES], 1, 2)[:, perm, :].astype(bf16)
    bg = b_gates[:, perm, None].astype(f32)
    uq = w_uq.reshape(depth, Q_LORA, MLA_HEADS, QK_NOPE + QK_ROPE)
    w_uq_p = jnp.concatenate([uq[..., :QK_NOPE].reshape(depth, Q_LORA, -1),
                              uq[..., QK_NOPE:].reshape(depth, Q_LORA, -1)], axis=-1).astype(bf16)
    ukv = w_ukv.reshape(depth, KV_LORA, MLA_HEADS, QK_NOPE + V_HEAD)
    w_ukv_p = jnp.concatenate([ukv[..., :QK_NOPE].reshape(depth, KV_LORA, -1),
                               ukv[..., QK_NOPE:].reshape(depth, KV_LORA, -1)], axis=-1).astype(bf16)
    w_out_b, w_up_b, w_down_b = w_out.astype(bf16), w_up.astype(bf16), w_down.astype(bf16)
    vec3 = lambda a: a[:, None, :].astype(f32)
    ml_g, q_g, kv_g = vec3(ml_norm_g), vec3(q_norm_g), vec3(kv_norm_g)
    l1g, l1b, l2g, l2b, cb = vec3(ln1_g), vec3(ln1_b), vec3(ln2_g), vec3(ln2_b), vec3(conv_b)
    cw = conv_w.astype(f32)
    tables = _rope_tables(lp)

    meta = jnp.broadcast_to(meta_tokens.astype(x.dtype)[None], (bsz, N_META, d))
    h = jnp.concatenate([jnp.zeros((bsz, FRONT, d), x.dtype), meta, x], axis=1).reshape(bsz * lp, d)
    hb = h.astype(bf16)
    for layer in range(depth):
        u = _inproj(hb, w_main, layer, lp)
        gq = _gates(hb, wg_t, bg, layer, bsz, lp)
        kt = _keys_t(hb, wk_t, layer)
        ml = _mlstm(u, kt, gq, ml_g, layer, bsz, lp)
        qc, kc, v = _mla_prep(u, w_uq_p, w_ukv_p, q_g, kv_g, tables, layer, bsz, lp)
        at = _attention(qc, kc, v, bsz, lp)
        h, hb = _outproj(ml, at, w_out_b, h, l1g, l1b, layer, alpha)
        act = _ffn_up(hb, w_up_b, cw, cb, layer, bsz, lp)
        h, hb = _ffn_down(act, w_down_b, h, l2g, l2b, layer, alpha, lp)
    y = h.reshape(bsz, lp, d)[:, LANE:]
    return (y[:n_prompt], y[n_prompt:])
```

```python
import functools
import math

import jax
import jax.numpy as jnp
from jax import lax
from jax.experimental import pallas as pl
from jax.experimental.pallas import tpu as pltpu

N_META = 16
ML_HEADS = 4
ML_HEAD_DIM = 256
ML_WIDTH = ML_HEADS * ML_HEAD_DIM
N_GATES = 4 * ML_HEADS
MLA_HEADS = 8
Q_LORA = 512
KV_LORA = 256
QK_NOPE = 128
QK_ROPE = 64
V_HEAD = 128
ROPE_THETA = 10000.0
EPS = 1e-5
NEG = -1e30

LANE = 128
FRONT = LANE - N_META
CHUNK = 128
ROPE_HALF = QK_ROPE // 2
SOFTMAX_SCALE_LOG2 = (QK_NOPE + QK_ROPE) ** -0.5 * math.log2(math.e)
MIB = 1024 * 1024

COL_Q, COL_V, COL_O = 0, ML_WIDTH, 2 * ML_WIDTH
COL_DQ = 3 * ML_WIDTH
COL_DKV = COL_DQ + Q_LORA
COL_KR = COL_DKV + KV_LORA
N_MAIN = COL_KR + 2 * QK_ROPE + LANE

f32 = jnp.float32
bf16 = jnp.bfloat16


def _pick(n, cands):
    for c in cands:
        if n % c == 0:
            return c
    raise ValueError(f"no tile for {n} in {cands}")


def _params(sem, vmem_mib):
    return pltpu.CompilerParams(dimension_semantics=sem, vmem_limit_bytes=vmem_mib * MIB)


def _dot(a, b):
    return jnp.dot(a, b, preferred_element_type=f32)


def _dot_nt(a, b):
    return lax.dot_general(a, b, (((1,), (1,)), ((), ())), preferred_element_type=f32)


def _inproj_kernel(a_ref, w_ref, o_ref, *, tiles_per_seq):
    first = pl.program_id(0) % tiles_per_seq == 0

    @pl.when(first)
    def _():
        o_ref[:FRONT, :] = jnp.zeros((FRONT, o_ref.shape[1]), o_ref.dtype)
        o_ref[FRONT:, :] = _dot(a_ref[FRONT:, :], w_ref[...]).astype(o_ref.dtype)

    @pl.when(jnp.logical_not(first))
    def _():
        o_ref[...] = _dot(a_ref[...], w_ref[...]).astype(o_ref.dtype)


def _keys_t_kernel(w_ref, h_ref, o_ref):
    o_ref[...] = (_dot_nt(w_ref[...], h_ref[...]) * ML_HEAD_DIM ** -0.5).astype(o_ref.dtype)


def _keys_t(hb, wk_t, layer):
    r, d = hb.shape
    tm = _pick(r, (1536, 384, 128))
    return pl.pallas_call(
        _keys_t_kernel,
        grid=(r // tm,),
        in_specs=[pl.BlockSpec((None, ML_WIDTH, d), lambda i: (layer, 0, 0)),
                  pl.BlockSpec((tm, d), lambda i: (i, 0))],
        out_specs=pl.BlockSpec((ML_WIDTH, tm), lambda i: (0, i)),
        out_shape=jax.ShapeDtypeStruct((ML_WIDTH, r), bf16),
        compiler_params=_params(("parallel",), 48),
        name="keys_t",
    )(wk_t, hb)


def _inproj(hb, w_main, layer, lp):
    r, d = hb.shape
    n = w_main.shape[-1]
    tm = _pick(lp, (1408, 384, 128))
    tn = _pick(n, (1024, 512, 256, 128))
    return pl.pallas_call(
        functools.partial(_inproj_kernel, tiles_per_seq=lp // tm),
        grid=(r // tm, n // tn),
        in_specs=[pl.BlockSpec((tm, d), lambda i, j: (i, 0)),
                  pl.BlockSpec((None, d, tn), lambda i, j: (layer, 0, j))],
        out_specs=pl.BlockSpec((tm, tn), lambda i, j: (i, j)),
        out_shape=jax.ShapeDtypeStruct((r, n), bf16),
        compiler_params=_params(("parallel", "arbitrary"), 48),
        name="inproj",
    )(hb, w_main)


def _gates_kernel(h_ref, wg_ref, bg_ref, o_ref):
    tm = h_ref.shape[0]
    half = N_GATES // 2
    g = _dot_nt(wg_ref[...], h_ref[...]) + bg_ref[...]
    pos = pl.program_id(1) * tm + lax.broadcasted_iota(jnp.int32, (half, tm), 1)
    unused = pos < FRONT
    gi, gf = g[:half], g[half:]
    li = jnp.where(unused, NEG, gi)
    lf = jnp.where(unused, 0.0, jnp.minimum(gf, 0.0) - jnp.log1p(jnp.exp(-jnp.abs(gf))))
    lane = lax.broadcasted_iota(jnp.int32, (half, CHUNK), 1)
    is_fwd = lax.broadcasted_iota(jnp.int32, (half, CHUNK), 0) < ML_HEADS
    for c in range(tm // CHUNK):
        sl = slice(c * CHUNK, (c + 1) * CHUNK)
        pre = suf = lf[:, sl]
        k = 1
        while k < CHUNK:
            pre = pre + jnp.where(lane >= k, pltpu.roll(pre, k, 1), 0.0)
            suf = suf + jnp.where(lane < CHUNK - k, pltpu.roll(suf, CHUNK - k, 1), 0.0)
            k *= 2
        b = jnp.where(is_fwd, pre, suf)
        o_ref[:half, sl] = li[:, sl] - b
        o_ref[half:, sl] = b


def _gates(hb, wg_t, bg, layer, bsz, lp):
    r, d = hb.shape
    tm = _pick(lp, (1408, 384, 128))
    nt = lp // tm
    return pl.pallas_call(
        _gates_kernel,
        grid=(bsz, nt),
        in_specs=[pl.BlockSpec((tm, d), lambda b, i: (b * nt + i, 0)),
                  pl.BlockSpec((None, N_GATES, d), lambda b, i: (layer, 0, 0)),
                  pl.BlockSpec((None, N_GATES, 1), lambda b, i: (layer, 0, 0))],
        out_specs=pl.BlockSpec((None, N_GATES, tm), lambda b, i: (b, 0, i)),
        out_shape=jax.ShapeDtypeStruct((bsz, N_GATES, lp), f32),
        compiler_params=_params(("parallel", "arbitrary"), 32),
        name="gates",
    )(hb, wg_t, bg)


MLSTM_UNROLL = 16


def _mlstm_kernel(q_ref, kt_ref, v_ref, o_ref, gq_ref, ng_ref, out_ref, hf_scr, hb_scr, c_scr, m_scr, s_scr):
    head = pl.program_id(1)
    lp, dh = q_ref.shape
    t = CHUNK
    nc = lp // t
    c_scr[...] = jnp.zeros_like(c_scr)
    m_scr[...] = jnp.zeros_like(m_scr)
    ti = lax.broadcasted_iota(jnp.int32, (t, t), 0)
    si = lax.broadcasted_iota(jnp.int32, (t, t), 1)
    masks = (si <= ti, si >= ti)
    eye = si == ti
    half = N_GATES // 2
    gate_row = lax.broadcasted_iota(jnp.int32, (half, t), 0)
    ones_col = jnp.where(lax.broadcasted_iota(jnp.int32, (t, LANE), 1) == 0, 1.0, 0.0).astype(bf16)

    def qk_ahead(c, dirn):
        base = pl.multiple_of(jnp.clip(c, 0, nc - 1) * t, t)
        s_scr[dirn] = _dot(q_ref[pl.ds(base, t), :], kt_ref[:, pl.ds(base, t)])

    def chunk(c, dirn, h_scr):
        base = pl.multiple_of(c * t, t)
        q = q_ref[pl.ds(base, t), :]
        kt = kt_ref[:, pl.ds(base, t)]
        v = jnp.concatenate([v_ref[pl.ds(base, t), :], ones_col], axis=1)
        mine = gate_row == head + ML_HEADS * dirn
        r_row = jnp.sum(jnp.where(mine, gq_ref[:half, pl.ds(base, t)], 0.0), axis=0, keepdims=True)
        b_row = jnp.sum(jnp.where(mine, gq_ref[half:, pl.ds(base, t)], 0.0), axis=0, keepdims=True)
        m_state = m_scr[dirn, 0:1, 0:1]
        mask = masks[dirn]
        m_col = jnp.maximum(jnp.max(jnp.where(mask, r_row, -jnp.inf), axis=1, keepdims=True), m_state)
        b_col = jnp.sum(jnp.where(eye, b_row, 0.0), axis=1, keepdims=True)
        m_all = jnp.maximum(jnp.max(r_row, axis=1, keepdims=True), m_state)
        b_last = b_row[:, 0:1] if dirn else b_row[:, t - 1:t]
        s = s_scr[dirn] * jnp.exp(jnp.where(mask, r_row - m_col, -jnp.inf))
        both = jnp.exp(m_state - m_col) * _dot(q, c_scr[dirn].astype(bf16)) + _dot(s.astype(bf16), v)
        floor = jnp.exp(-(b_col + m_col))
        h_scr[pl.ds(base, t), :] = both[:, :dh] / jnp.maximum(jnp.abs(both[:, dh:dh + 1]), floor)
        ktw = (kt.astype(f32) * jnp.exp(r_row - m_all)).astype(bf16)
        c_scr[dirn] = jnp.exp(m_state - m_all) * c_scr[dirn] + _dot(ktw, v)
        m_scr[dirn] = jnp.broadcast_to(b_last + m_all, m_scr.shape[1:])

    def finish(c):
        base = pl.multiple_of(c * t, t)
        hs = hf_scr[pl.ds(base, t), :] + hb_scr[pl.ds(base, t), :]
        mu = jnp.mean(hs, axis=1, keepdims=True)
        var = jnp.mean(jnp.square(hs - mu), axis=1, keepdims=True)
        hn = (hs - mu) * lax.rsqrt(var + EPS) * ng_ref[...]
        gate = jax.nn.sigmoid(o_ref[pl.ds(base, t), :].astype(f32))
        out_ref[pl.ds(base, t), :] = (gate * hn).astype(out_ref.dtype)

    def scan_step(c):
        chunk(c, 0, hf_scr)
        chunk(nc - 1 - c, 1, hb_scr)
        qk_ahead(c + 1, 0)
        qk_ahead(nc - 2 - c, 1)

    def first_half(c, carry):
        scan_step(c)
        return carry

    def second_half(c, carry):
        finish(c - 1)
        finish(nc - c)
        scan_step(c)
        return carry

    qk_ahead(0, 0)
    qk_ahead(nc - 1, 1)
    meet = (nc + 1) // 2
    lax.fori_loop(0, meet, first_half, 0, unroll=MLSTM_UNROLL)
    lax.fori_loop(meet, nc, second_half, 0, unroll=MLSTM_UNROLL)
    finish(nc - 1)
    finish(0)


def _mlstm(u, kt, gq, ml_norm_g, layer, bsz, lp):
    r = u.shape[0]
    dh = ML_HEAD_DIM

    def col(first):
        return pl.BlockSpec((lp, dh), lambda b, h: (b, first // dh + h))

    return pl.pallas_call(
        _mlstm_kernel,
        grid=(bsz, ML_HEADS),
        in_specs=[col(COL_Q), pl.BlockSpec((dh, lp), lambda b, h: (h, b)), col(COL_V), col(COL_O),
                  pl.BlockSpec((None, N_GATES, lp), lambda b, h: (b, 0, 0)),
                  pl.BlockSpec((None, 1, dh), lambda b, h: (layer, 0, h))],
        out_specs=pl.BlockSpec((lp, dh), lambda b, h: (b, h)),
        out_shape=jax.ShapeDtypeStruct((r, ML_WIDTH), bf16),
        scratch_shapes=[pltpu.VMEM((lp, dh), f32), pltpu.VMEM((lp, dh), f32),
                        pltpu.VMEM((2, dh, dh + LANE), f32), pltpu.VMEM((2, 8, LANE), f32),
                        pltpu.VMEM((2, CHUNK, CHUNK), f32)],
        compiler_params=_params(("parallel", "arbitrary"), 48),
        name="mlstm",
    )(u, kt, u, u, gq, ml_norm_g)


def _rms(x, g):
    return x * lax.rsqrt(jnp.mean(jnp.square(x), axis=-1, keepdims=True) + EPS) * g


def _rope(x, cos, sin_a, sin_b):
    n = x.shape[1]
    reps = n // LANE
    if reps > 1:
        cos, sin_a, sin_b = (jnp.concatenate([a] * reps, axis=1) for a in (cos, sin_a, sin_b))
    return x * cos + pltpu.roll(x, n - ROPE_HALF, 1) * sin_a + pltpu.roll(x, ROPE_HALF, 1) * sin_b


def _mla_prep_kernel(dq_ref, dkv_ref, kr_ref, wuq_ref, wukv_ref, qg_ref, kvg_ref,
                     cos_ref, sa_ref, sb_ref, qc_ref, kc_ref, v_ref):
    cos, sa, sb = cos_ref[...], sa_ref[...], sb_ref[...]
    q = _dot(_rms(dq_ref[...].astype(f32), qg_ref[...]).astype(bf16), wuq_ref[...]) * SOFTMAX_SCALE_LOG2
    kv =_dot(_rms(dkv_ref[...].astype(f32), kvg_ref[...]).astype(bf16), wukv_ref[...])
    q_rope = _rope(q[:, MLA_HEADS * QK_NOPE:], cos, sa, sb)
    k_rope = _rope(kr_ref[...].astype(f32), cos, sa, sb)
    tm = q.shape[0]
    lane = lax.broadcasted_iota(jnp.int32, (tm, LANE), 1)
    pos = pl.program_id(1) * tm + lax.broadcasted_iota(jnp.int32, (tm, LANE), 0)
    low = lane < QK_ROPE
    marker = lane == QK_ROPE
    one_hot = jnp.where(marker, 1.0, 0.0)
    k_tail = jnp.where(low, k_rope, jnp.where(marker & (pos < FRONT), -jnp.inf, 0.0)).astype(bf16)
    ones_col = jnp.where(lane == 0, 1.0, 0.0).astype(bf16)
    for h in range(MLA_HEADS):
        pair = q_rope[:, (h // 2) * LANE:(h // 2 + 1) * LANE]
        if h % 2:
            pair = pltpu.roll(pair, QK_ROPE, 1)
        qc_ref[h, :, :QK_NOPE] = q[:, h * QK_NOPE:(h + 1) * QK_NOPE].astype(bf16)
        qc_ref[h, :, QK_NOPE:] = jnp.where(low, pair, one_hot).astype(bf16)
        kc_ref[h, :, :QK_NOPE] = kv[:, h * QK_NOPE:(h + 1) * QK_NOPE].astype(bf16)
        kc_ref[h, :, QK_NOPE:] = k_tail
        v_ref[h, :, :V_HEAD] = kv[:, (MLA_HEADS + h) * V_HEAD:(MLA_HEADS + h + 1) * V_HEAD].astype(bf16)
        v_ref[h, :, V_HEAD:] = ones_col


def _mla_prep(u, w_uq_p, w_ukv_p, q_norm_g, kv_norm_g, tables, layer, bsz, lp):
    r = u.shape[0]
    tm = _pick(lp, (384, 128))
    nt = lp // tm
    kd = QK_NOPE + LANE
    tab = pl.BlockSpec((tm, LANE), lambda b, i: (i, 0))
    row = lambda b, i: b * nt + i
    return pl.pallas_call(
        _mla_prep_kernel,
        grid=(bsz, nt),
        in_specs=[pl.BlockSpec((tm, Q_LORA), lambda b, i: (row(b, i), COL_DQ // Q_LORA)),
                  pl.BlockSpec((tm, KV_LORA), lambda b, i: (row(b, i), COL_DKV // KV_LORA)),
                  pl.BlockSpec((tm, LANE), lambda b, i: (row(b, i), COL_KR // LANE)),
                  pl.BlockSpec((None,) + w_uq_p.shape[1:], lambda b, i: (layer, 0, 0)),
                  pl.BlockSpec((None,) + w_ukv_p.shape[1:], lambda b, i: (layer, 0, 0)),
                  pl.BlockSpec((None, 1, Q_LORA), lambda b, i: (layer, 0, 0)),
                  pl.BlockSpec((None, 1, KV_LORA), lambda b, i: (layer, 0, 0)),
                  tab, tab, tab],
        out_specs=[pl.BlockSpec((MLA_HEADS, tm, kd), lambda b, i: (0, row(b, i), 0)),
                   pl.BlockSpec((MLA_HEADS, tm, kd), lambda b, i: (0, row(b, i), 0)),
                   pl.BlockSpec((MLA_HEADS, tm, 2 * V_HEAD), lambda b, i: (0, row(b, i), 0))],
        out_shape=[jax.ShapeDtypeStruct((MLA_HEADS, r, kd), bf16),
                   jax.ShapeDtypeStruct((MLA_HEADS, r, kd), bf16),
                   jax.ShapeDtypeStruct((MLA_HEADS, r, 2 * V_HEAD), bf16)],
        compiler_params=_params(("parallel", "arbitrary"), 48),
        name="mla_prep",
    )(u, u, u, w_uq_p, w_ukv_p, q_norm_g, kv_norm_g, *tables)


ATTN_HEADS_PER_STEP = 2
ATTN_TILES_PER_BODY = 12


def _attn_kernel(q_ref, k_ref, v_ref, o_ref, s0_ref, s1_ref, p0_ref, p1_ref, *, tq):
    nh, lp, _ = k_ref.shape
    nq = lp // tq
    n_tiles = nh * nq
    s_refs, p_refs = (s0_ref, s1_ref), (p0_ref, p1_ref)

    @pl.when((pl.program_id(0) == 0) & (pl.program_id(1) == 0))
    def _():
        for ref in s_refs + p_refs:
            ref[...] = jnp.zeros_like(ref)

    def tile(x):
        x = jnp.clip(x, 0, n_tiles - 1)
        hh = x // nq
        return hh, pl.ds(pl.multiple_of((x - hh * nq) * tq, tq), tq)

    def scores(x, slot):
        hh, rows = tile(x)
        s_refs[slot][...] = _dot_nt(q_ref[hh, rows, :], k_ref[hh])

    def softmax(slot):
        s = s_refs[slot][...]
        p_refs[slot][...] = jnp.exp2(s - jnp.max(s, axis=1, keepdims=True)).astype(bf16)

    def weighted(x, slot):
        hh, rows = tile(x)
        acc = _dot(p_refs[slot][...], v_ref[hh])
        o_ref[hh, rows, :] = (acc[:, :V_HEAD] * (1.0 / acc[:, V_HEAD:V_HEAD + 1])).astype(o_ref.dtype)

    n_bodies = -(-(n_tiles + 2) // ATTN_TILES_PER_BODY)
    start = n_tiles + 2 - ATTN_TILES_PER_BODY * n_bodies

    def body(u, carry):
        for e in range(ATTN_TILES_PER_BODY):
            x = ATTN_TILES_PER_BODY * u + e + start
            weighted(x - 2, e % 2)
            softmax((e + 1) % 2)
            scores(x, e % 2)
        return carry

    lax.fori_loop(0, n_bodies, body, 0)


def _attention(qc, kc, v, bsz, lp):
    r = v.shape[1]
    tq = _pick(lp, (384, 128))
    kd = qc.shape[-1]
    nh = ATTN_HEADS_PER_STEP
    blk = lambda width: pl.BlockSpec((nh, lp, width), lambda b, h: (h, b, 0))
    return pl.pallas_call(
        functools.partial(_attn_kernel, tq=tq),
        grid=(bsz, MLA_HEADS // nh),
        in_specs=[blk(kd), blk(kd), blk(2 * V_HEAD)],
        out_specs=blk(V_HEAD),
        out_shape=jax.ShapeDtypeStruct((MLA_HEADS, r, V_HEAD), bf16),
        scratch_shapes=[pltpu.VMEM((tq, lp), f32), pltpu.VMEM((tq, lp), f32),
                        pltpu.VMEM((tq, lp), bf16), pltpu.VMEM((tq, lp), bf16)],
        compiler_params=_params(("arbitrary", "arbitrary"), 60),
        name="attention",
    )(qc, kc, v)


def _layer_norm(y, g, b):
    mu = jnp.mean(y, axis=-1, keepdims=True)
    var = jnp.mean(jnp.square(y - mu), axis=-1, keepdims=True)
    return (y - mu) * lax.rsqrt(var + EPS) * g + b


OUTPROJ_SUBTILES = 2


def _outproj_kernel(ml_ref, at_ref, w_ref, h_ref, g_ref, b_ref, o_ref, ob_ref, *, alpha):
    ts = h_ref.shape[0] // OUTPROJ_SUBTILES
    for sub in range(OUTPROJ_SUBTILES):
        rows = slice(sub * ts, (sub + 1) * ts)
        at = jnp.concatenate([at_ref[h, rows, :] for h in range(MLA_HEADS)], axis=1)
        mix = _dot(ml_ref[rows, :], w_ref[:ML_WIDTH, :]) + _dot(at, w_ref[ML_WIDTH:, :])
        y = _layer_norm(alpha * h_ref[rows, :] + mix, g_ref[...], b_ref[...])
        o_ref[rows, :] = y
        ob_ref[rows, :] = y.astype(bf16)


def _outproj(ml, at, w_out, h, ln_g, ln_b, layer, alpha):
    r, d = h.shape
    tm = _pick(r, (768, 384, 128))
    vec = pl.BlockSpec((None, 1, d), lambda i: (layer, 0, 0))
    return pl.pallas_call(
        functools.partial(_outproj_kernel, alpha=alpha),
        grid=(r // tm,),
        in_specs=[pl.BlockSpec((tm, ML_WIDTH), lambda i: (i, 0)),
                  pl.BlockSpec((MLA_HEADS, tm, V_HEAD), lambda i: (0, i, 0)),
                  pl.BlockSpec((None,) + w_out.shape[1:], lambda i: (layer, 0, 0), pipeline_mode=pl.Buffered(1)),
                  pl.BlockSpec((tm, d), lambda i: (i, 0)), vec, vec],
        out_specs=[pl.BlockSpec((tm, d), lambda i: (i, 0)), pl.BlockSpec((tm, d), lambda i: (i, 0))],
        out_shape=[jax.ShapeDtypeStruct((r, d), f32), jax.ShapeDtypeStruct((r, d), bf16)],
        compiler_params=_params(("parallel",), 56),
        name="outproj_ln",
    )(ml, at, w_out, h, ln_g, ln_b)


HALO = 16


def _ffn_up_kernel(prev_ref, main_ref, next_ref, w_ref, cw_ref, cb_ref, o_ref, ext_scr, gv_scr, *, lp):
    tm, tn = o_ref.shape

    @pl.when(pl.program_id(2) == 0)
    def _():
        ext_scr[:HALO, :] = prev_ref[...]
        ext_scr[HALO:HALO + tm, :] = main_ref[...]
        ext_scr[HALO + tm:, :] = next_ref[...]

    def compute(lo):
        pos = pl.program_id(1) * tm - HALO + lo + lax.broadcasted_iota(jnp.int32, (tm + 2 * HALO - lo, 1), 0)
        inside = (pos >= FRONT) & (pos < lp)
        gv = _dot(ext_scr[lo:, :], w_ref[...])
        gv_scr[lo:, :tn] = jnp.where(inside, gv[:, :tn], 0.0)
        gv_scr[lo:, tn:] = gv[:, tn:]
        taps = [gv_scr[pl.ds(lo + HALO - 1 + j, tm - lo), :tn] * cw_ref[j:j + 1, :] for j in range(3)]
        conv = taps[0] + taps[1] + taps[2] + cb_ref[...]
        val = gv_scr[pl.ds(lo + HALO, tm - lo), tn:]
        o_ref[lo:, :] = (conv * jax.nn.sigmoid(conv) * val).astype(o_ref.dtype)

    @pl.when(pl.program_id(1) == 0)
    def _():
        o_ref[:FRONT, :] = jnp.zeros((FRONT, o_ref.shape[1]), o_ref.dtype)
        compute(FRONT)

    @pl.when(pl.program_id(1) != 0)
    def _():
        compute(0)


FFN_UP_COLS = (512, 256, 128)


def _ffn_up(hb, w_up, conv_w, conv_b, layer, bsz, lp):
    r, d = hb.shape
    dff = conv_b.shape[-1]
    tm = _pick(lp, (1408, 384, 128))
    tn = _pick(dff, FFN_UP_COLS)
    nt, nj = lp // tm, dff // tn
    hblk = tm // HALO
    last = r // HALO - 1
    return pl.pallas_call(
        functools.partial(_ffn_up_kernel, lp=lp),
        grid=(bsz, nt, nj),
        in_specs=[pl.BlockSpec((HALO, d), lambda b, i, j: (jnp.maximum((b * nt + i) * hblk - 1, 0), 0)),
                  pl.BlockSpec((tm, d), lambda b, i, j: (b * nt + i, 0)),
                  pl.BlockSpec((HALO, d), lambda b, i, j: (jnp.minimum((b * nt + i + 1) * hblk, last), 0)),
                  pl.BlockSpec((None, d, 2 * tn), lambda b, i, j: (layer, 0, j)),
                  pl.BlockSpec((None, 3, tn), lambda b, i, j: (layer, 0, j)),
                  pl.BlockSpec((None, 1, tn), lambda b, i, j: (layer, 0, j))],
        out_specs=pl.BlockSpec((tm, tn), lambda b, i, j: (b * nt + i, j)),
        out_shape=jax.ShapeDtypeStruct((r, dff), bf16),
        scratch_shapes=[pltpu.VMEM((tm + 2 * HALO, d), bf16), pltpu.VMEM((tm + 2 * HALO, 2 * tn), f32)],
        compiler_params=_params(("parallel", "parallel", "arbitrary"), 60),
        name="ffn_up_conv",
    )(hb, hb, hb, w_up, conv_w, conv_b)


FFN_DOWN_SUBTILES = 2


def _ffn_down_kernel(a_ref, w_ref, h_ref, g_ref, b_ref, o_ref, ob_ref, *, alpha, tiles_per_seq):
    tm = h_ref.shape[0]

    def compute(rows):
        y = _layer_norm(alpha * h_ref[rows, :] + _dot(a_ref[rows, :], w_ref[...]), g_ref[...], b_ref[...])
        o_ref[rows, :] = y
        ob_ref[rows, :] = y.astype(bf16)

    first = pl.program_id(0) % tiles_per_seq == 0

    @pl.when(first)
    def _():
        o_ref[:FRONT, :] = jnp.zeros((FRONT, o_ref.shape[1]), o_ref.dtype)
        ob_ref[:FRONT, :] = jnp.zeros((FRONT, ob_ref.shape[1]), ob_ref.dtype)
        compute(slice(FRONT, tm))

    @pl.when(jnp.logical_not(first))
    def _():
        ts = tm // FFN_DOWN_SUBTILES
        for sub in range(FFN_DOWN_SUBTILES):
            compute(slice(sub * ts, (sub + 1) * ts))


def _ffn_down(act, w_down, h, ln_g, ln_b, layer, alpha, lp):
    r, d = h.shape
    dff = act.shape[1]
    tm = _pick(lp, (384, 128))
    vec = pl.BlockSpec((None, 1, d), lambda i: (layer, 0, 0))
    return pl.pallas_call(
        functools.partial(_ffn_down_kernel, alpha=alpha, tiles_per_seq=lp // tm),
        grid=(r // tm,),
        in_specs=[pl.BlockSpec((tm, dff), lambda i: (i, 0)),
                  pl.BlockSpec((None, dff, d), lambda i: (layer, 0, 0), pipeline_mode=pl.Buffered(1)),
                  pl.BlockSpec((tm, d), lambda i: (i, 0)), vec, vec],
        out_specs=[pl.BlockSpec((tm, d), lambda i: (i, 0)), pl.BlockSpec((tm, d), lambda i: (i, 0))],
        out_shape=[jax.ShapeDtypeStruct((r, d), f32), jax.ShapeDtypeStruct((r, d), bf16)],
        compiler_params=_params(("parallel",), 60),
        name="ffn_down_ln",
    )(act, w_down, h, ln_g, ln_b)


def _rope_tables(lp):
    inv_freq = ROPE_THETA ** (-jnp.arange(0, QK_ROPE, 2, dtype=f32) / QK_ROPE)
    pos = jnp.maximum(jnp.arange(lp, dtype=f32) - FRONT, 0.0)
    ang = pos[:, None] * inv_freq[None, :]
    cos, sin = jnp.cos(ang), jnp.sin(ang)
    zero = jnp.zeros_like(sin)
    reps = LANE // QK_ROPE
    cos_t = jnp.tile(jnp.concatenate([cos, cos], axis=1), (1, reps))
    sin_a = jnp.tile(jnp.concatenate([-sin, zero], axis=1), (1, reps))
    sin_b = jnp.tile(jnp.concatenate([zero, sin], axis=1), (1, reps))
    return cos_t, sin_a, sin_b


def kernel(x_prompt, x_sample, meta_tokens, w_in, b_gates, ml_norm_g, q_norm_g, kv_norm_g, w_uq, w_ukv,
           w_out, ln1_g, ln1_b, w_up, conv_w, conv_b, w_down, ln2_g, ln2_b):
    depth, d, _ = w_in.shape
    assert x_prompt.shape[1:] == x_sample.shape[1:]
    n_prompt = x_prompt.shape[0]
    x = jnp.concatenate([x_prompt, x_sample], axis=0)
    bsz, seq, _ = x.shape
    assert seq % LANE == 0
    lp = LANE + seq
    alpha = (2 * depth) ** 0.25

    gate_lo = 4 * ML_WIDTH
    w_main = jnp.concatenate(
        [w_in[..., :ML_WIDTH], w_in[..., 2 * ML_WIDTH:gate_lo], w_in[..., gate_lo + N_GATES:], w_in[..., -QK_ROPE:],
         jnp.zeros((depth, d, N_MAIN - COL_KR - 2 * QK_ROPE), w_in.dtype)], axis=-1).astype(bf16)
    wk_t = jnp.swapaxes(w_in[..., ML_WIDTH:2 * ML_WIDTH], 1, 2).astype(bf16)
    h4 = ML_HEADS
    perm = jnp.array(list(range(0, h4)) + list(range(2 * h4, 3 * h4))
                     + list(range(h4, 2 * h4)) + list(range(3 * h4, 4 * h4)))
    wg_t = jnp.swapaxes(w_in[..., gate_lo:gate_lo + N_GATES], 1, 2)[:, perm, :].astype(bf16)
    bg = b_gates[:, perm, None].astype(f32)
    uq = w_uq.reshape(depth, Q_LORA, MLA_HEADS, QK_NOPE + QK_ROPE)
    w_uq_p = jnp.concatenate([uq[..., :QK_NOPE].reshape(depth, Q_LORA, -1),
                              uq[..., QK_NOPE:].reshape(depth, Q_LORA, -1)], axis=-1).astype(bf16)
    ukv = w_ukv.reshape(depth, KV_LORA, MLA_HEADS, QK_NOPE + V_HEAD)
    w_ukv_p = jnp.concatenate([ukv[..., :QK_NOPE].reshape(depth, KV_LORA, -1),
                               ukv[..., QK_NOPE:].reshape(depth, KV_LORA, -1)], axis=-1).astype(bf16)
    w_out_b, w_down_b = w_out.astype(bf16), w_down.astype(bf16)
    dff = conv_b.shape[-1]
    tn = _pick(dff, FFN_UP_COLS)
    w_up_b = jnp.swapaxes(w_up.astype(bf16).reshape(depth, d, 2, dff // tn, tn), 2, 3).reshape(depth, d, 2 * dff)
    vec3 = lambda a: a[:, None, :].astype(f32)
    ml_g, q_g, kv_g = vec3(ml_norm_g), vec3(q_norm_g), vec3(kv_norm_g)
    l1g, l1b, l2g, l2b, cb = vec3(ln1_g), vec3(ln1_b), vec3(ln2_g), vec3(ln2_b), vec3(conv_b)
    cw = conv_w.astype(f32)
    tables = _rope_tables(lp)

    meta = jnp.broadcast_to(meta_tokens.astype(x.dtype)[None], (bsz, N_META, d))
    h = jnp.concatenate([jnp.zeros((bsz, FRONT, d), x.dtype), meta, x], axis=1).reshape(bsz * lp, d)
    hb = h.astype(bf16)
    for layer in range(depth):
        u = _inproj(hb, w_main, layer, lp)
        gq = _gates(hb, wg_t, bg, layer, bsz, lp)
        kt = _keys_t(hb, wk_t, layer)
        ml = _mlstm(u, kt, gq, ml_g, layer, bsz, lp)
        qc, kc, v = _mla_prep(u, w_uq_p, w_ukv_p, q_g, kv_g, tables, layer, bsz, lp)
        at = _attention(qc, kc, v, bsz, lp)
        h, hb = _outproj(ml, at, w_out_b, h, l1g, l1b, layer, alpha)
        act = _ffn_up(hb, w_up_b, cw, cb, layer, bsz, lp)
        h, hb = _ffn_down(act, w_down_b, h, l2g, l2b, layer, alpha, lp)
    y = h.reshape(bsz, lp, d)[:, LANE:]
    return (y[:n_prompt], y[n_prompt:])
```

```python
import functools
import math

import jax
import jax.numpy as jnp
from jax import lax
from jax.experimental import pallas as pl
from jax.experimental.pallas import tpu as pltpu

N_META = 16
ML_HEADS = 4
ML_HEAD_DIM = 256
ML_WIDTH = ML_HEADS * ML_HEAD_DIM
N_GATES = 4 * ML_HEADS
MLA_HEADS = 8
Q_LORA = 512
KV_LORA = 256
QK_NOPE = 128
QK_ROPE = 64
V_HEAD = 128
ROPE_THETA = 10000.0
EPS = 1e-5
NEG = -1e30

LANE = 128
FRONT = LANE - N_META
CHUNK = 128
ROPE_HALF = QK_ROPE // 2
SOFTMAX_SCALE_LOG2 = (QK_NOPE + QK_ROPE) ** -0.5 * math.log2(math.e)
MIB = 1024 * 1024

COL_Q, COL_V, COL_O = 0, ML_WIDTH, 2 * ML_WIDTH
COL_DQ = 3 * ML_WIDTH
COL_DKV = COL_DQ + Q_LORA
COL_KR = COL_DKV + KV_LORA
N_MAIN = COL_KR + 2 * QK_ROPE + LANE

f32 = jnp.float32
bf16 = jnp.bfloat16


def _pick(n, cands):
    for c in cands:
        if n % c == 0:
            return c
    raise ValueError(f"no tile for {n} in {cands}")


def _params(sem, vmem_mib):
    return pltpu.CompilerParams(dimension_semantics=sem, vmem_limit_bytes=vmem_mib * MIB)


def _dot(a, b):
    return jnp.dot(a, b, preferred_element_type=f32)


def _dot_nt(a, b):
    return lax.dot_general(a, b, (((1,), (1,)), ((), ())), preferred_element_type=f32)


def _inproj_kernel(a_ref, w_ref, o_ref, *, tiles_per_seq):
    first = pl.program_id(0) % tiles_per_seq == 0

    @pl.when(first)
    def _():
        o_ref[:FRONT, :] = jnp.zeros((FRONT, o_ref.shape[1]), o_ref.dtype)
        o_ref[FRONT:, :] = _dot(a_ref[FRONT:, :], w_ref[...]).astype(o_ref.dtype)

    @pl.when(jnp.logical_not(first))
    def _():
        o_ref[...] = _dot(a_ref[...], w_ref[...]).astype(o_ref.dtype)


def _keys_t_kernel(w_ref, h_ref, o_ref):
    o_ref[...] = (_dot_nt(w_ref[...], h_ref[...]) * ML_HEAD_DIM ** -0.5).astype(o_ref.dtype)


def _keys_t(hb, wk_t, layer):
    r, d = hb.shape
    tm = _pick(r, (1536, 384, 128))
    return pl.pallas_call(
        _keys_t_kernel,
        grid=(r // tm,),
        in_specs=[pl.BlockSpec((None, ML_WIDTH, d), lambda i: (layer, 0, 0)),
                  pl.BlockSpec((tm, d), lambda i: (i, 0))],
        out_specs=pl.BlockSpec((ML_WIDTH, tm), lambda i: (0, i)),
        out_shape=jax.ShapeDtypeStruct((ML_WIDTH, r), bf16),
        compiler_params=_params(("parallel",), 48),
        name="keys_t",
    )(wk_t, hb)


def _inproj(hb, w_main, layer, lp):
    r, d = hb.shape
    n = w_main.shape[-1]
    tm = _pick(lp, (1408, 384, 128))
    tn = _pick(n, (1024, 512, 256, 128))
    return pl.pallas_call(
        functools.partial(_inproj_kernel, tiles_per_seq=lp // tm),
        grid=(r // tm, n // tn),
        in_specs=[pl.BlockSpec((tm, d), lambda i, j: (i, 0)),
                  pl.BlockSpec((None, d, tn), lambda i, j: (layer, 0, j))],
        out_specs=pl.BlockSpec((tm, tn), lambda i, j: (i, j)),
        out_shape=jax.ShapeDtypeStruct((r, n), bf16),
        compiler_params=_params(("parallel", "arbitrary"), 48),
        name="inproj",
    )(hb, w_main)


def _gates_kernel(h_ref, wg_ref, bg_ref, o_ref):
    tm = h_ref.shape[0]
    half = N_GATES // 2
    g = _dot_nt(wg_ref[...], h_ref[...]) + bg_ref[...]
    pos = pl.program_id(1) * tm + lax.broadcasted_iota(jnp.int32, (half, tm), 1)
    unused = pos < FRONT
    gi, gf = g[:half], g[half:]
    li = jnp.where(unused, NEG, gi)
    lf = jnp.where(unused, 0.0, jnp.minimum(gf, 0.0) - jnp.log1p(jnp.exp(-jnp.abs(gf))))
    lane = lax.broadcasted_iota(jnp.int32, (half, CHUNK), 1)
    is_fwd = lax.broadcasted_iota(jnp.int32, (half, CHUNK), 0) < ML_HEADS
    for c in range(tm // CHUNK):
        sl = slice(c * CHUNK, (c + 1) * CHUNK)
        pre = suf = lf[:, sl]
        k = 1
        while k < CHUNK:
            pre = pre + jnp.where(lane >= k, pltpu.roll(pre, k, 1), 0.0)
            suf = suf + jnp.where(lane < CHUNK - k, pltpu.roll(suf, CHUNK - k, 1), 0.0)
            k *= 2
        b = jnp.where(is_fwd, pre, suf)
        o_ref[:half, sl] = li[:, sl] - b
        o_ref[half:, sl] = b


def _gates(hb, wg_t, bg, layer, bsz, lp):
    r, d = hb.shape
    tm = _pick(lp, (1408, 384, 128))
    nt = lp // tm
    return pl.pallas_call(
        _gates_kernel,
        grid=(bsz, nt),
        in_specs=[pl.BlockSpec((tm, d), lambda b, i: (b * nt + i, 0)),
                  pl.BlockSpec((None, N_GATES, d), lambda b, i: (layer, 0, 0)),
                  pl.BlockSpec((None, N_GATES, 1), lambda b, i: (layer, 0, 0))],
        out_specs=pl.BlockSpec((None, N_GATES, tm), lambda b, i: (b, 0, i)),
        out_shape=jax.ShapeDtypeStruct((bsz, N_GATES, lp), f32),
        compiler_params=_params(("parallel", "arbitrary"), 32),
        name="gates",
    )(hb, wg_t, bg)


MLSTM_UNROLL = 16


def _mlstm_kernel(q_ref, kt_ref, v_ref, o_ref, gq_ref, ng_ref, out_ref, hf_scr, hb_scr, c_scr, m_scr, s_scr):
    head = pl.program_id(1)
    lp, dh = q_ref.shape
    t = CHUNK
    nc = lp // t
    c_scr[...] = jnp.zeros_like(c_scr)
    m_scr[...] = jnp.zeros_like(m_scr)
    ti = lax.broadcasted_iota(jnp.int32, (t, t), 0)
    si = lax.broadcasted_iota(jnp.int32, (t, t), 1)
    masks = (si <= ti, si >= ti)
    eye = si == ti
    half = N_GATES // 2
    gate_row = lax.broadcasted_iota(jnp.int32, (half, t), 0)
    ones_col = jnp.where(lax.broadcasted_iota(jnp.int32, (t, LANE), 1) == 0, 1.0, 0.0).astype(bf16)

    def qk_ahead(c, dirn):
        base = pl.multiple_of(jnp.clip(c, 0, nc - 1) * t, t)
        s_scr[dirn] = _dot(q_ref[pl.ds(base, t), :], kt_ref[:, pl.ds(base, t)])

    def chunk(c, dirn, h_scr):
        base = pl.multiple_of(c * t, t)
        q = q_ref[pl.ds(base, t), :]
        kt = kt_ref[:, pl.ds(base, t)]
        v = jnp.concatenate([v_ref[pl.ds(base, t), :], ones_col], axis=1)
        mine = gate_row == head + ML_HEADS * dirn
        r_row = jnp.sum(jnp.where(mine, gq_ref[:half, pl.ds(base, t)], 0.0), axis=0, keepdims=True)
        b_row = jnp.sum(jnp.where(mine, gq_ref[half:, pl.ds(base, t)], 0.0), axis=0, keepdims=True)
        m_state = m_scr[dirn, 0:1, 0:1]
        mask = masks[dirn]
        m_col = jnp.maximum(jnp.max(jnp.where(mask, r_row, -jnp.inf), axis=1, keepdims=True), m_state)
        b_col = jnp.sum(jnp.where(eye, b_row, 0.0), axis=1, keepdims=True)
        m_all = jnp.maximum(jnp.max(r_row, axis=1, keepdims=True), m_state)
        b_last = b_row[:, 0:1] if dirn else b_row[:, t - 1:t]
        s = s_scr[dirn] * jnp.exp(jnp.where(mask, r_row - m_col, -jnp.inf))
        both = jnp.exp(m_state - m_col) * _dot(q, c_scr[dirn].astype(bf16)) + _dot(s.astype(bf16), v)
        floor = jnp.exp(-(b_col + m_col))
        h_scr[pl.ds(base, t), :] = both[:, :dh] / jnp.maximum(jnp.abs(both[:, dh:dh + 1]), floor)
        ktw = (kt.astype(f32) * jnp.exp(r_row - m_all)).astype(bf16)
        c_scr[dirn] = jnp.exp(m_state - m_all) * c_scr[dirn] + _dot(ktw, v)
        m_scr[dirn] = jnp.broadcast_to(b_last + m_all, m_scr.shape[1:])

    def finish(c):
        base = pl.multiple_of(c * t, t)
        hs = hf_scr[pl.ds(base, t), :] + hb_scr[pl.ds(base, t), :]
        mu = jnp.mean(hs, axis=1, keepdims=True)
        var = jnp.mean(jnp.square(hs - mu), axis=1, keepdims=True)
        hn = (hs - mu) * lax.rsqrt(var + EPS) * ng_ref[...]
        gate = jax.nn.sigmoid(o_ref[pl.ds(base, t), :].astype(f32))
        out_ref[pl.ds(base, t), :] = (gate * hn).astype(out_ref.dtype)

    def scan_step(c):
        chunk(c, 0, hf_scr)
        chunk(nc - 1 - c, 1, hb_scr)
        qk_ahead(c + 1, 0)
        qk_ahead(nc - 2 - c, 1)

    def first_half(c, carry):
        scan_step(c)
        return carry

    def second_half(c, carry):
        finish(c - 1)
        finish(nc - c)
        scan_step(c)
        return carry

    qk_ahead(0, 0)
    qk_ahead(nc - 1, 1)
    meet = (nc + 1) // 2
    lax.fori_loop(0, meet, first_half, 0, unroll=MLSTM_UNROLL)
    lax.fori_loop(meet, nc, second_half, 0, unroll=MLSTM_UNROLL)
    finish(nc - 1)
    finish(0)


def _mlstm(u, kt, gq, ml_norm_g, layer, bsz, lp):
    r = u.shape[0]
    dh = ML_HEAD_DIM

    def col(first):
        return pl.BlockSpec((lp, dh), lambda b, h: (b, first // dh + h))

    return pl.pallas_call(
        _mlstm_kernel,
        grid=(bsz, ML_HEADS),
        in_specs=[col(COL_Q), pl.BlockSpec((dh, lp), lambda b, h: (h, b)), col(COL_V), col(COL_O),
                  pl.BlockSpec((None, N_GATES, lp), lambda b, h: (b, 0, 0)),
                  pl.BlockSpec((None, 1, dh), lambda b, h: (layer, 0, h))],
        out_specs=pl.BlockSpec((lp, dh), lambda b, h: (b, h)),
        out_shape=jax.ShapeDtypeStruct((r, ML_WIDTH), bf16),
        scratch_shapes=[pltpu.VMEM((lp, dh), f32), pltpu.VMEM((lp, dh), f32),
                        pltpu.VMEM((2, dh, dh + LANE), f32), pltpu.VMEM((2, 8, LANE), f32),
                        pltpu.VMEM((2, CHUNK, CHUNK), f32)],
        compiler_params=_params(("parallel", "arbitrary"), 48),
        name="mlstm",
    )(u, kt, u, u, gq, ml_norm_g)


def _rms(x, g):
    return x * lax.rsqrt(jnp.mean(jnp.square(x), axis=-1, keepdims=True) + EPS) * g


def _rope(x, cos, sin_a, sin_b):
    n = x.shape[1]
    reps = n // LANE
    if reps > 1:
        cos, sin_a, sin_b = (jnp.concatenate([a] * reps, axis=1) for a in (cos, sin_a, sin_b))
    return x * cos + pltpu.roll(x, n - ROPE_HALF, 1) * sin_a + pltpu.roll(x, ROPE_HALF, 1) * sin_b


def _mla_prep_kernel(dq_ref, dkv_ref, kr_ref, wuq_ref, wukv_ref, qg_ref, kvg_ref,
                     cos_ref, sa_ref, sb_ref, qc_ref, kc_ref, v_ref):
    cos, sa, sb = cos_ref[...], sa_ref[...], sb_ref[...]
    q = _dot(_rms(dq_ref[...].astype(f32), qg_ref[...]).astype(bf16), wuq_ref[...]) * SOFTMAX_SCALE_LOG2
    kv =_dot(_rms(dkv_ref[...].astype(f32), kvg_ref[...]).astype(bf16), wukv_ref[...])
    q_rope = _rope(q[:, MLA_HEADS * QK_NOPE:], cos, sa, sb)
    k_rope = _rope(kr_ref[...].astype(f32), cos, sa, sb)
    tm = q.shape[0]
    lane = lax.broadcasted_iota(jnp.int32, (tm, LANE), 1)
    pos = pl.program_id(1) * tm + lax.broadcasted_iota(jnp.int32, (tm, LANE), 0)
    low = lane < QK_ROPE
    marker = lane == QK_ROPE
    one_hot = jnp.where(marker, 1.0, 0.0)
    k_tail = jnp.where(low, k_rope, jnp.where(marker & (pos < FRONT), -jnp.inf, 0.0)).astype(bf16)
    ones_col = jnp.where(lane == 0, 1.0, 0.0).astype(bf16)
    for h in range(MLA_HEADS):
        pair = q_rope[:, (h // 2) * LANE:(h // 2 + 1) * LANE]
        if h % 2:
            pair = pltpu.roll(pair, QK_ROPE, 1)
        qc_ref[h, :, :QK_NOPE] = q[:, h * QK_NOPE:(h + 1) * QK_NOPE].astype(bf16)
        qc_ref[h, :, QK_NOPE:] = jnp.where(low, pair, one_hot).astype(bf16)
        kc_ref[h, :, :QK_NOPE] = kv[:, h * QK_NOPE:(h + 1) * QK_NOPE].astype(bf16)
        kc_ref[h, :, QK_NOPE:] = k_tail
        v_ref[h, :, :V_HEAD] = kv[:, (MLA_HEADS + h) * V_HEAD:(MLA_HEADS + h + 1) * V_HEAD].astype(bf16)
        v_ref[h, :, V_HEAD:] = ones_col


def _mla_prep(u, w_uq_p, w_ukv_p, q_norm_g, kv_norm_g, tables, layer, bsz, lp):
    r = u.shape[0]
    tm = _pick(lp, (384, 128))
    nt = lp // tm
    kd = QK_NOPE + LANE
    tab = pl.BlockSpec((tm, LANE), lambda b, i: (i, 0))
    row = lambda b, i: b * nt + i
    return pl.pallas_call(
        _mla_prep_kernel,
        grid=(bsz, nt),
        in_specs=[pl.BlockSpec((tm, Q_LORA), lambda b, i: (row(b, i), COL_DQ // Q_LORA)),
                  pl.BlockSpec((tm, KV_LORA), lambda b, i: (row(b, i), COL_DKV // KV_LORA)),
                  pl.BlockSpec((tm, LANE), lambda b, i: (row(b, i), COL_KR // LANE)),
                  pl.BlockSpec((None,) + w_uq_p.shape[1:], lambda b, i: (layer, 0, 0)),
                  pl.BlockSpec((None,) + w_ukv_p.shape[1:], lambda b, i: (layer, 0, 0)),
                  pl.BlockSpec((None, 1, Q_LORA), lambda b, i: (layer, 0, 0)),
                  pl.BlockSpec((None, 1, KV_LORA), lambda b, i: (layer, 0, 0)),
                  tab, tab, tab],
        out_specs=[pl.BlockSpec((MLA_HEADS, tm, kd), lambda b, i: (0, row(b, i), 0)),
                   pl.BlockSpec((MLA_HEADS, tm, kd), lambda b, i: (0, row(b, i), 0)),
                   pl.BlockSpec((MLA_HEADS, tm, 2 * V_HEAD), lambda b, i: (0, row(b, i), 0))],
        out_shape=[jax.ShapeDtypeStruct((MLA_HEADS, r, kd), bf16),
                   jax.ShapeDtypeStruct((MLA_HEADS, r, kd), bf16),
                   jax.ShapeDtypeStruct((MLA_HEADS, r, 2 * V_HEAD), bf16)],
        compiler_params=_params(("parallel", "arbitrary"), 48),
        name="mla_prep",
    )(u, u, u, w_uq_p, w_ukv_p, q_norm_g, kv_norm_g, *tables)


ATTN_HEADS_PER_STEP = 2
ATTN_TILES_PER_BODY = 12


def _attn_kernel(q_ref, k_ref, v_ref, o_ref, s0_ref, s1_ref, p0_ref, p1_ref, *, tq):
    nh, lp, _ = k_ref.shape
    nq = lp // tq
    n_tiles = nh * nq
    s_refs, p_refs = (s0_ref, s1_ref), (p0_ref, p1_ref)

    @pl.when((pl.program_id(0) == 0) & (pl.program_id(1) == 0))
    def _():
        for ref in s_refs + p_refs:
            ref[...] = jnp.zeros_like(ref)

    def tile(x):
        x = jnp.clip(x, 0, n_tiles - 1)
        hh = x // nq
        return hh, pl.ds(pl.multiple_of((x - hh * nq) * tq, tq), tq)

    def scores(x, slot):
        hh, rows = tile(x)
        s_refs[slot][...] = _dot_nt(q_ref[hh, rows, :], k_ref[hh])

    def softmax(slot):
        s = s_refs[slot][...]
        p_refs[slot][...] = jnp.exp2(s - jnp.max(s, axis=1, keepdims=True)).astype(bf16)

    def weighted(x, slot):
        hh, rows = tile(x)
        acc = _dot(p_refs[slot][...], v_ref[hh])
        o_ref[hh, rows, :] = (acc[:, :V_HEAD] * (1.0 / acc[:, V_HEAD:V_HEAD + 1])).astype(o_ref.dtype)

    n_bodies = -(-(n_tiles + 2) // ATTN_TILES_PER_BODY)
    start = n_tiles + 2 - ATTN_TILES_PER_BODY * n_bodies

    def body(u, carry):
        for e in range(ATTN_TILES_PER_BODY):
            x = ATTN_TILES_PER_BODY * u + e + start
            weighted(x - 2, e % 2)
            softmax((e + 1) % 2)
            scores(x, e % 2)
        return carry

    lax.fori_loop(0, n_bodies, body, 0)


def _attention(qc, kc, v, bsz, lp):
    r = v.shape[1]
    tq = _pick(lp, (384, 128))
    kd = qc.shape[-1]
    nh = ATTN_HEADS_PER_STEP
    blk = lambda width: pl.BlockSpec((nh, lp, width), lambda b, h: (h, b, 0))
    return pl.pallas_call(
        functools.partial(_attn_kernel, tq=tq),
        grid=(bsz, MLA_HEADS // nh),
        in_specs=[blk(kd), blk(kd), blk(2 * V_HEAD)],
        out_specs=blk(V_HEAD),
        out_shape=jax.ShapeDtypeStruct((MLA_HEADS, r, V_HEAD), bf16),
        scratch_shapes=[pltpu.VMEM((tq, lp), f32), pltpu.VMEM((tq, lp), f32),
                        pltpu.VMEM((tq, lp), bf16), pltpu.VMEM((tq, lp), bf16)],
        compiler_params=_params(("arbitrary", "arbitrary"), 60),
        name="attention",
    )(qc, kc, v)


def _layer_norm(y, g, b):
    mu = jnp.mean(y, axis=-1, keepdims=True)
    var = jnp.mean(jnp.square(y - mu), axis=-1, keepdims=True)
    return (y - mu) * lax.rsqrt(var + EPS) * g + b


OUTPROJ_SUBTILES = 2


def _outproj_kernel(ml_ref, at_ref, w_ref, h_ref, g_ref, b_ref, o_ref, ob_ref, *, alpha):
    ts = h_ref.shape[0] // OUTPROJ_SUBTILES
    for sub in range(OUTPROJ_SUBTILES):
        rows = slice(sub * ts, (sub + 1) * ts)
        at = jnp.concatenate([at_ref[h, rows, :] for h in range(MLA_HEADS)], axis=1)
        mix = _dot(ml_ref[rows, :], w_ref[:ML_WIDTH, :]) + _dot(at, w_ref[ML_WIDTH:, :])
        y = _layer_norm(alpha * h_ref[rows, :] + mix, g_ref[...], b_ref[...])
        o_ref[rows, :] = y
        ob_ref[rows, :] = y.astype(bf16)


def _outproj(ml, at, w_out, h, ln_g, ln_b, layer, alpha):
    r, d = h.shape
    tm = _pick(r, (768, 384, 128))
    vec = pl.BlockSpec((None, 1, d), lambda i: (layer, 0, 0))
    return pl.pallas_call(
        functools.partial(_outproj_kernel, alpha=alpha),
        grid=(r // tm,),
        in_specs=[pl.BlockSpec((tm, ML_WIDTH), lambda i: (i, 0)),
                  pl.BlockSpec((MLA_HEADS, tm, V_HEAD), lambda i: (0, i, 0)),
                  pl.BlockSpec((None,) + w_out.shape[1:], lambda i: (layer, 0, 0), pipeline_mode=pl.Buffered(1)),
                  pl.BlockSpec((tm, d), lambda i: (i, 0)), vec, vec],
        out_specs=[pl.BlockSpec((tm, d), lambda i: (i, 0)), pl.BlockSpec((tm, d), lambda i: (i, 0))],
        out_shape=[jax.ShapeDtypeStruct((r, d), f32), jax.ShapeDtypeStruct((r, d), bf16)],
        compiler_params=_params(("parallel",), 56),
        name="outproj_ln",
    )(ml, at, w_out, h, ln_g, ln_b)


HALO = 16


def _ffn_up_kernel(prev_ref, main_ref, next_ref, wg_ref, wv_ref, cw_ref, cb_ref, o_ref,
                   ext_scr, gate_scr, val_scr, *, lp):
    tm = main_ref.shape[0]

    @pl.when(pl.program_id(2) == 0)
    def _():
        ext_scr[:HALO, :] = prev_ref[...]
        ext_scr[HALO:HALO + tm, :] = main_ref[...]
        ext_scr[HALO + tm:, :] = next_ref[...]

    def compute(lo):
        pos = pl.program_id(1) * tm - HALO + lo + lax.broadcasted_iota(jnp.int32, (tm + 2 * HALO - lo, 1), 0)
        inside = (pos >= FRONT) & (pos < lp)
        val_scr[lo:, :] = _dot(main_ref[lo:, :], wv_ref[...])
        gate_scr[lo:, :] = jnp.where(inside, _dot(ext_scr[lo:, :], wg_ref[...]), 0.0)
        taps = [gate_scr[pl.ds(lo + HALO - 1 + j, tm - lo), :] * cw_ref[j:j + 1, :] for j in range(3)]
        conv = taps[0] + taps[1] + taps[2] + cb_ref[...]
        o_ref[lo:, :] = (conv * jax.nn.sigmoid(conv) * val_scr[lo:, :]).astype(o_ref.dtype)

    @pl.when(pl.program_id(1) == 0)
    def _():
        o_ref[:FRONT, :] = jnp.zeros((FRONT, o_ref.shape[1]), o_ref.dtype)
        compute(FRONT)

    @pl.when(pl.program_id(1) != 0)
    def _():
        compute(0)


def _ffn_up(hb, w_up, conv_w, conv_b, layer, bsz, lp):
    r, d = hb.shape
    dff = conv_b.shape[-1]
    tm = _pick(lp, (1408, 384, 128))
    tn = _pick(dff, (512, 256, 128))
    nt, nj = lp // tm, dff // tn
    hblk = tm // HALO
    last = r // HALO - 1
    return pl.pallas_call(
        functools.partial(_ffn_up_kernel, lp=lp),
        grid=(bsz, nt, nj),
        in_specs=[pl.BlockSpec((HALO, d), lambda b, i, j: (jnp.maximum((b * nt + i) * hblk - 1, 0), 0)),
                  pl.BlockSpec((tm, d), lambda b, i, j: (b * nt + i, 0)),
                  pl.BlockSpec((HALO, d), lambda b, i, j: (jnp.minimum((b * nt + i + 1) * hblk, last), 0)),
                  pl.BlockSpec((None, d, tn), lambda b, i, j: (layer, 0, j)),
                  pl.BlockSpec((None, d, tn), lambda b, i, j: (layer, 0, nj + j)),
                  pl.BlockSpec((None, 3, tn), lambda b, i, j: (layer, 0, j)),
                  pl.BlockSpec((None, 1, tn), lambda b, i, j: (layer, 0, j))],
        out_specs=pl.BlockSpec((tm, tn), lambda b, i, j: (b * nt + i, j)),
        out_shape=jax.ShapeDtypeStruct((r, dff), bf16),
        scratch_shapes=[pltpu.VMEM((tm + 2 * HALO, d), bf16), pltpu.VMEM((tm + 2 * HALO, tn), f32),
                        pltpu.VMEM((tm, tn), f32)],
        compiler_params=_params(("parallel", "parallel", "arbitrary"), 56),
        name="ffn_up_conv",
    )(hb, hb, hb, w_up, w_up, conv_w, conv_b)


FFN_DOWN_SUBTILES = 2


def _ffn_down_kernel(a_ref, w_ref, h_ref, g_ref, b_ref, o_ref, ob_ref, *, alpha, tiles_per_seq):
    tm = h_ref.shape[0]

    def compute(rows):
        y = _layer_norm(alpha * h_ref[rows, :] + _dot(a_ref[rows, :], w_ref[...]), g_ref[...], b_ref[...])
        o_ref[rows, :] = y
        ob_ref[rows, :] = y.astype(bf16)

    first = pl.program_id(0) % tiles_per_seq == 0

    @pl.when(first)
    def _():
        o_ref[:FRONT, :] = jnp.zeros((FRONT, o_ref.shape[1]), o_ref.dtype)
        ob_ref[:FRONT, :] = jnp.zeros((FRONT, ob_ref.shape[1]), ob_ref.dtype)
        compute(slice(FRONT, tm))

    @pl.when(jnp.logical_not(first))
    def _():
        ts = tm // FFN_DOWN_SUBTILES
        for sub in range(FFN_DOWN_SUBTILES):
            compute(slice(sub * ts, (sub + 1) * ts))


def _ffn_down(act, w_down, h, ln_g, ln_b, layer, alpha, lp):
    r, d = h.shape
    dff = act.shape[1]
    tm = _pick(lp, (384, 128))
    vec = pl.BlockSpec((None, 1, d), lambda i: (layer, 0, 0))
    return pl.pallas_call(
        functools.partial(_ffn_down_kernel, alpha=alpha, tiles_per_seq=lp // tm),
        grid=(r // tm,),
        in_specs=[pl.BlockSpec((tm, dff), lambda i: (i, 0)),
                  pl.BlockSpec((None, dff, d), lambda i: (layer, 0, 0), pipeline_mode=pl.Buffered(1)),
                  pl.BlockSpec((tm, d), lambda i: (i, 0)), vec, vec],
        out_specs=[pl.BlockSpec((tm, d), lambda i: (i, 0)), pl.BlockSpec((tm, d), lambda i: (i, 0))],
        out_shape=[jax.ShapeDtypeStruct((r, d), f32), jax.ShapeDtypeStruct((r, d), bf16)],
        compiler_params=_params(("parallel",), 60),
        name="ffn_down_ln",
    )(act, w_down, h, ln_g, ln_b)


def _rope_tables(lp):
    inv_freq = ROPE_THETA ** (-jnp.arange(0, QK_ROPE, 2, dtype=f32) / QK_ROPE)
    pos = jnp.maximum(jnp.arange(lp, dtype=f32) - FRONT, 0.0)
    ang = pos[:, None] * inv_freq[None, :]
    cos, sin = jnp.cos(ang), jnp.sin(ang)
    zero = jnp.zeros_like(sin)
    reps = LANE // QK_ROPE
    cos_t = jnp.tile(jnp.concatenate([cos, cos], axis=1), (1, reps))
    sin_a = jnp.tile(jnp.concatenate([-sin, zero], axis=1), (1, reps))
    sin_b = jnp.tile(jnp.concatenate([zero, sin], axis=1), (1, reps))
    return cos_t, sin_a, sin_b


def kernel(x_prompt, x_sample, meta_tokens, w_in, b_gates, ml_norm_g, q_norm_g, kv_norm_g, w_uq, w_ukv,
           w_out, ln1_g, ln1_b, w_up, conv_w, conv_b, w_down, ln2_g, ln2_b):
    depth, d, _ = w_in.shape
    assert x_prompt.shape[1:] == x_sample.shape[1:]
    n_prompt = x_prompt.shape[0]
    x = jnp.concatenate([x_prompt, x_sample], axis=0)
    bsz, seq, _ = x.shape
    assert seq % LANE == 0
    lp = LANE + seq
    alpha = (2 * depth) ** 0.25

    gate_lo = 4 * ML_WIDTH
    w_main = jnp.concatenate(
        [w_in[..., :ML_WIDTH], w_in[..., 2 * ML_WIDTH:gate_lo], w_in[..., gate_lo + N_GATES:], w_in[..., -QK_ROPE:],
         jnp.zeros((depth, d, N_MAIN - COL_KR - 2 * QK_ROPE), w_in.dtype)], axis=-1).astype(bf16)
    wk_t = jnp.swapaxes(w_in[..., ML_WIDTH:2 * ML_WIDTH], 1, 2).astype(bf16)
    h4 = ML_HEADS
    perm = jnp.array(list(range(0, h4)) + list(range(2 * h4, 3 * h4))
                     + list(range(h4, 2 * h4)) + list(range(3 * h4, 4 * h4)))
    wg_t = jnp.swapaxes(w_in[..., gate_lo:gate_lo + N_GATES], 1, 2)[:, perm, :].astype(bf16)
    bg = b_gates[:, perm, None].astype(f32)
    uq = w_uq.reshape(depth, Q_LORA, MLA_HEADS, QK_NOPE + QK_ROPE)
    w_uq_p = jnp.concatenate([uq[..., :QK_NOPE].reshape(depth, Q_LORA, -1),
                              uq[..., QK_NOPE:].reshape(depth, Q_LORA, -1)], axis=-1).astype(bf16)
    ukv = w_ukv.reshape(depth, KV_LORA, MLA_HEADS, QK_NOPE + V_HEAD)
    w_ukv_p = jnp.concatenate([ukv[..., :QK_NOPE].reshape(depth, KV_LORA, -1),
                               ukv[..., QK_NOPE:].reshape(depth, KV_LORA, -1)], axis=-1).astype(bf16)
    w_out_b, w_up_b, w_down_b = w_out.astype(bf16), w_up.astype(bf16), w_down.astype(bf16)
    vec3 = lambda a: a[:, None, :].astype(f32)
    ml_g, q_g, kv_g = vec3(ml_norm_g), vec3(q_norm_g), vec3(kv_norm_g)
    l1g, l1b, l2g, l2b, cb = vec3(ln1_g), vec3(ln1_b), vec3(ln2_g), vec3(ln2_b), vec3(conv_b)
    cw = conv_w.astype(f32)
    tables = _rope_tables(lp)

    meta = jnp.broadcast_to(meta_tokens.astype(x.dtype)[None], (bsz, N_META, d))
    h = jnp.concatenate([jnp.zeros((bsz, FRONT, d), x.dtype), meta, x], axis=1).reshape(bsz * lp, d)
    hb = h.astype(bf16)
    for layer in range(depth):
        u = _inproj(hb, w_main, layer, lp)
        gq = _gates(hb, wg_t, bg, layer, bsz, lp)
        kt = _keys_t(hb, wk_t, layer)
        ml = _mlstm(u, kt, gq, ml_g, layer, bsz, lp)
        qc, kc, v = _mla_prep(u, w_uq_p, w_ukv_p, q_g, kv_g, tables, layer, bsz, lp)
        at = _attention(qc, kc, v, bsz, lp)
        h, hb = _outproj(ml, at, w_out_b, h, l1g, l1b, layer, alpha)
        act = _ffn_up(hb, w_up_b, cw, cb, layer, bsz, lp)
        h, hb = _ffn_down(act, w_down_b, h, l2g, l2b, layer, alpha, lp)
    y = h.reshape(bsz, lp, d)[:, LANE:]
    return (y[:n_prompt], y[n_prompt:])
```

```python
import functools
import math

import jax
import jax.numpy as jnp
from jax import lax
from jax.experimental import pallas as pl
from jax.experimental.pallas import tpu as pltpu

N_META = 16
ML_HEADS = 4
ML_HEAD_DIM = 256
ML_WIDTH = ML_HEADS * ML_HEAD_DIM
N_GATES = 4 * ML_HEADS
MLA_HEADS = 8
Q_LORA = 512
KV_LORA = 256
QK_NOPE = 128
QK_ROPE = 64
V_HEAD = 128
ROPE_THETA = 10000.0
EPS = 1e-5
NEG = -1e30

LANE = 128
FRONT = LANE - N_META
CHUNK = 128
ROPE_HALF = QK_ROPE // 2
SOFTMAX_SCALE_LOG2 = (QK_NOPE + QK_ROPE) ** -0.5 * math.log2(math.e)
MIB = 1024 * 1024

COL_Q, COL_V, COL_O = 0, ML_WIDTH, 2 * ML_WIDTH
COL_DQ = 3 * ML_WIDTH
COL_DKV = COL_DQ + Q_LORA
COL_KR = COL_DKV + KV_LORA
N_MAIN = COL_KR + 2 * QK_ROPE + LANE

f32 = jnp.float32
bf16 = jnp.bfloat16


def _pick(n, cands):
    for c in cands:
        if n % c == 0:
            return c
    raise ValueError(f"no tile for {n} in {cands}")


def _params(sem, vmem_mib):
    return pltpu.CompilerParams(dimension_semantics=sem, vmem_limit_bytes=vmem_mib * MIB)


def _dot(a, b):
    return jnp.dot(a, b, preferred_element_type=f32)


def _dot_nt(a, b):
    return lax.dot_general(a, b, (((1,), (1,)), ((), ())), preferred_element_type=f32)


def _inproj_kernel(a_ref, w_ref, o_ref, *, tiles_per_seq):
    first = pl.program_id(0) % tiles_per_seq == 0

    @pl.when(first)
    def _():
        o_ref[:FRONT, :] = jnp.zeros((FRONT, o_ref.shape[1]), o_ref.dtype)
        o_ref[FRONT:, :] = _dot(a_ref[FRONT:, :], w_ref[...]).astype(o_ref.dtype)

    @pl.when(jnp.logical_not(first))
    def _():
        o_ref[...] = _dot(a_ref[...], w_ref[...]).astype(o_ref.dtype)


def _keys_t_kernel(w_ref, h_ref, o_ref):
    o_ref[...] = (_dot_nt(w_ref[...], h_ref[...]) * ML_HEAD_DIM ** -0.5).astype(o_ref.dtype)


def _keys_t(hb, wk_t, layer):
    r, d = hb.shape
    tm = _pick(r, (1536, 384, 128))
    return pl.pallas_call(
        _keys_t_kernel,
        grid=(r // tm,),
        in_specs=[pl.BlockSpec((None, ML_WIDTH, d), lambda i: (layer, 0, 0)),
                  pl.BlockSpec((tm, d), lambda i: (i, 0))],
        out_specs=pl.BlockSpec((ML_WIDTH, tm), lambda i: (0, i)),
        out_shape=jax.ShapeDtypeStruct((ML_WIDTH, r), bf16),
        compiler_params=_params(("parallel",), 48),
        name="keys_t",
    )(wk_t, hb)


def _inproj(hb, w_main, layer, lp):
    r, d = hb.shape
    n = w_main.shape[-1]
    tm = _pick(lp, (1408, 384, 128))
    tn = _pick(n, (1024, 512, 256, 128))
    return pl.pallas_call(
        functools.partial(_inproj_kernel, tiles_per_seq=lp // tm),
        grid=(r // tm, n // tn),
        in_specs=[pl.BlockSpec((tm, d), lambda i, j: (i, 0)),
                  pl.BlockSpec((None, d, tn), lambda i, j: (layer, 0, j))],
        out_specs=pl.BlockSpec((tm, tn), lambda i, j: (i, j)),
        out_shape=jax.ShapeDtypeStruct((r, n), bf16),
        compiler_params=_params(("parallel", "arbitrary"), 48),
        name="inproj",
    )(hb, w_main)


def _gates_kernel(h_ref, wg_ref, bg_ref, o_ref):
    tm = h_ref.shape[0]
    half = N_GATES // 2
    g = _dot_nt(wg_ref[...], h_ref[...]) + bg_ref[...]
    pos = pl.program_id(1) * tm + lax.broadcasted_iota(jnp.int32, (half, tm), 1)
    unused = pos < FRONT
    gi, gf = g[:half], g[half:]
    li = jnp.where(unused, NEG, gi)
    lf = jnp.where(unused, 0.0, jnp.minimum(gf, 0.0) - jnp.log1p(jnp.exp(-jnp.abs(gf))))
    lane = lax.broadcasted_iota(jnp.int32, (half, CHUNK), 1)
    is_fwd = lax.broadcasted_iota(jnp.int32, (half, CHUNK), 0) < ML_HEADS
    for c in range(tm // CHUNK):
        sl = slice(c * CHUNK, (c + 1) * CHUNK)
        pre = suf = lf[:, sl]
        k = 1
        while k < CHUNK:
            pre = pre + jnp.where(lane >= k, pltpu.roll(pre, k, 1), 0.0)
            suf = suf + jnp.where(lane < CHUNK - k, pltpu.roll(suf, CHUNK - k, 1), 0.0)
            k *= 2
        b = jnp.where(is_fwd, pre, suf)
        o_ref[:half, sl] = li[:, sl] - b
        o_ref[half:, sl] = b


def _gates(hb, wg_t, bg, layer, bsz, lp):
    r, d = hb.shape
    tm = _pick(lp, (1408, 384, 128))
    nt = lp // tm
    return pl.pallas_call(
        _gates_kernel,
        grid=(bsz, nt),
        in_specs=[pl.BlockSpec((tm, d), lambda b, i: (b * nt + i, 0)),
                  pl.BlockSpec((None, N_GATES, d), lambda b, i: (layer, 0, 0)),
                  pl.BlockSpec((None, N_GATES, 1), lambda b, i: (layer, 0, 0))],
        out_specs=pl.BlockSpec((None, N_GATES, tm), lambda b, i: (b, 0, i)),
        out_shape=jax.ShapeDtypeStruct((bsz, N_GATES, lp), f32),
        compiler_params=_params(("parallel", "arbitrary"), 32),
        name="gates",
    )(hb, wg_t, bg)


MLSTM_UNROLL = 16


def _mlstm_kernel(q_ref, kt_ref, v_ref, o_ref, gq_ref, ng_ref, out_ref, hf_scr, hb_scr, c_scr, m_scr, s_scr):
    head = pl.program_id(1)
    lp, dh = q_ref.shape
    t = CHUNK
    nc = lp // t
    c_scr[...] = jnp.zeros_like(c_scr)
    m_scr[...] = jnp.zeros_like(m_scr)
    ti = lax.broadcasted_iota(jnp.int32, (t, t), 0)
    si = lax.broadcasted_iota(jnp.int32, (t, t), 1)
    masks = (si <= ti, si >= ti)
    eye = si == ti
    half = N_GATES // 2
    gate_row = lax.broadcasted_iota(jnp.int32, (half, t), 0)
    ones_col = jnp.where(lax.broadcasted_iota(jnp.int32, (t, LANE), 1) == 0, 1.0, 0.0).astype(bf16)

    def qk_ahead(c, dirn):
        base = pl.multiple_of(jnp.clip(c, 0, nc - 1) * t, t)
        s_scr[dirn] = _dot(q_ref[pl.ds(base, t), :], kt_ref[:, pl.ds(base, t)])

    def chunk(c, dirn, h_scr):
        base = pl.multiple_of(c * t, t)
        q = q_ref[pl.ds(base, t), :]
        kt = kt_ref[:, pl.ds(base, t)]
        v = jnp.concatenate([v_ref[pl.ds(base, t), :], ones_col], axis=1)
        mine = gate_row == head + ML_HEADS * dirn
        r_row = jnp.sum(jnp.where(mine, gq_ref[:half, pl.ds(base, t)], 0.0), axis=0, keepdims=True)
        b_row = jnp.sum(jnp.where(mine, gq_ref[half:, pl.ds(base, t)], 0.0), axis=0, keepdims=True)
        m_state = m_scr[dirn, 0:1, 0:1]
        mask = masks[dirn]
        m_col = jnp.maximum(jnp.max(jnp.where(mask, r_row, -jnp.inf), axis=1, keepdims=True), m_state)
        b_col = jnp.sum(jnp.where(eye, b_row, 0.0), axis=1, keepdims=True)
        m_all = jnp.maximum(jnp.max(r_row, axis=1, keepdims=True), m_state)
        b_last = b_row[:, 0:1] if dirn else b_row[:, t - 1:t]
        s = s_scr[dirn] * jnp.exp(jnp.where(mask, r_row - m_col, -jnp.inf))
        both = jnp.exp(m_state - m_col) * _dot(q, c_scr[dirn].astype(bf16)) + _dot(s.astype(bf16), v)
        floor = jnp.exp(-(b_col + m_col))
        h_scr[pl.ds(base, t), :] = both[:, :dh] / jnp.maximum(jnp.abs(both[:, dh:dh + 1]), floor)
        ktw = (kt.astype(f32) * jnp.exp(r_row - m_all)).astype(bf16)
        c_scr[dirn] = jnp.exp(m_state - m_all) * c_scr[dirn] + _dot(ktw, v)
        m_scr[dirn] = jnp.broadcast_to(b_last + m_all, m_scr.shape[1:])

    def finish(c):
        base = pl.multiple_of(c * t, t)
        hs = hf_scr[pl.ds(base, t), :] + hb_scr[pl.ds(base, t), :]
        mu = jnp.mean(hs, axis=1, keepdims=True)
        var = jnp.mean(jnp.square(hs - mu), axis=1, keepdims=True)
        hn = (hs - mu) * lax.rsqrt(var + EPS) * ng_ref[...]
        gate = jax.nn.sigmoid(o_ref[pl.ds(base, t), :].astype(f32))
        out_ref[pl.ds(base, t), :] = (gate * hn).astype(out_ref.dtype)

    def scan_step(c):
        chunk(c, 0, hf_scr)
        chunk(nc - 1 - c, 1, hb_scr)
        qk_ahead(c + 1, 0)
        qk_ahead(nc - 2 - c, 1)

    def first_half(c, carry):
        scan_step(c)
        return carry

    def second_half(c, carry):
        finish(c - 1)
        finish(nc - c)
        scan_step(c)
        return carry

    qk_ahead(0, 0)
    qk_ahead(nc - 1, 1)
    meet = (nc + 1) // 2
    lax.fori_loop(0, meet, first_half, 0, unroll=MLSTM_UNROLL)
    lax.fori_loop(meet, nc, second_half, 0, unroll=MLSTM_UNROLL)
    finish(nc - 1)
    finish(0)


def _mlstm(u, kt, gq, ml_norm_g, layer, bsz, lp):
    r = u.shape[0]
    dh = ML_HEAD_DIM

    def col(first):
        return pl.BlockSpec((lp, dh), lambda b, h: (b, first // dh + h))

    return pl.pallas_call(
        _mlstm_kernel,
        grid=(bsz, ML_HEADS),
        in_specs=[col(COL_Q), pl.BlockSpec((dh, lp), lambda b, h: (h, b)), col(COL_V), col(COL_O),
                  pl.BlockSpec((None, N_GATES, lp), lambda b, h: (b, 0, 0)),
                  pl.BlockSpec((None, 1, dh), lambda b, h: (layer, 0, h))],
        out_specs=pl.BlockSpec((lp, dh), lambda b, h: (b, h)),
        out_shape=jax.ShapeDtypeStruct((r, ML_WIDTH), bf16),
        scratch_shapes=[pltpu.VMEM((lp, dh), f32), pltpu.VMEM((lp, dh), f32),
                        pltpu.VMEM((2, dh, dh + LANE), f32), pltpu.VMEM((2, 8, LANE), f32),
                        pltpu.VMEM((2, CHUNK, CHUNK), f32)],
        compiler_params=_params(("parallel", "arbitrary"), 48),
        name="mlstm",
    )(u, kt, u, u, gq, ml_norm_g)


def _rms(x, g):
    return x * lax.rsqrt(jnp.mean(jnp.square(x), axis=-1, keepdims=True) + EPS) * g


def _rope(x, cos, sin_a, sin_b):
    n = x.shape[1]
    reps = n // LANE
    if reps > 1:
        cos, sin_a, sin_b = (jnp.concatenate([a] * reps, axis=1) for a in (cos, sin_a, sin_b))
    return x * cos + pltpu.roll(x, n - ROPE_HALF, 1) * sin_a + pltpu.roll(x, ROPE_HALF, 1) * sin_b


def _mla_prep_kernel(dq_ref, dkv_ref, kr_ref, wuq_ref, wukv_ref, qg_ref, kvg_ref,
                     cos_ref, sa_ref, sb_ref, qc_ref, kc_ref, v_ref):
    cos, sa, sb = cos_ref[...], sa_ref[...], sb_ref[...]
    q = _dot(_rms(dq_ref[...].astype(f32), qg_ref[...]).astype(bf16), wuq_ref[...]) * SOFTMAX_SCALE_LOG2
    kv =_dot(_rms(dkv_ref[...].astype(f32), kvg_ref[...]).astype(bf16), wukv_ref[...])
    q_rope = _rope(q[:, MLA_HEADS * QK_NOPE:], cos, sa, sb)
    k_rope = _rope(kr_ref[...].astype(f32), cos, sa, sb)
    tm = q.shape[0]
    lane = lax.broadcasted_iota(jnp.int32, (tm, LANE), 1)
    pos = pl.program_id(1) * tm + lax.broadcasted_iota(jnp.int32, (tm, LANE), 0)
    low = lane < QK_ROPE
    marker = lane == QK_ROPE
    one_hot = jnp.where(marker, 1.0, 0.0)
    k_tail = jnp.where(low, k_rope, jnp.where(marker & (pos < FRONT), -jnp.inf, 0.0)).astype(bf16)
    ones_col = jnp.where(lane == 0, 1.0, 0.0).astype(bf16)
    for h in range(MLA_HEADS):
        pair = q_rope[:, (h // 2) * LANE:(h // 2 + 1) * LANE]
        if h % 2:
            pair = pltpu.roll(pair, QK_ROPE, 1)
        qc_ref[h, :, :QK_NOPE] = q[:, h * QK_NOPE:(h + 1) * QK_NOPE].astype(bf16)
        qc_ref[h, :, QK_NOPE:] = jnp.where(low, pair, one_hot).astype(bf16)
        kc_ref[h, :, :QK_NOPE] = kv[:, h * QK_NOPE:(h + 1) * QK_NOPE].astype(bf16)
        kc_ref[h, :, QK_NOPE:] = k_tail
        v_ref[h, :, :V_HEAD] = kv[:, (MLA_HEADS + h) * V_HEAD:(MLA_HEADS + h + 1) * V_HEAD].astype(bf16)
        v_ref[h, :, V_HEAD:] = ones_col


def _mla_prep(u, w_uq_p, w_ukv_p, q_norm_g, kv_norm_g, tables, layer, bsz, lp):
    r = u.shape[0]
    tm = _pick(lp, (384, 128))
    nt = lp // tm
    kd = QK_NOPE + LANE
    tab = pl.BlockSpec((tm, LANE), lambda b, i: (i, 0))
    row = lambda b, i: b * nt + i
    return pl.pallas_call(
        _mla_prep_kernel,
        grid=(bsz, nt),
        in_specs=[pl.BlockSpec((tm, Q_LORA), lambda b, i: (row(b, i), COL_DQ // Q_LORA)),
                  pl.BlockSpec((tm, KV_LORA), lambda b, i: (row(b, i), COL_DKV // KV_LORA)),
                  pl.BlockSpec((tm, LANE), lambda b, i: (row(b, i), COL_KR // LANE)),
                  pl.BlockSpec((None,) + w_uq_p.shape[1:], lambda b, i: (layer, 0, 0)),
                  pl.BlockSpec((None,) + w_ukv_p.shape[1:], lambda b, i: (layer, 0, 0)),
                  pl.BlockSpec((None, 1, Q_LORA), lambda b, i: (layer, 0, 0)),
                  pl.BlockSpec((None, 1, KV_LORA), lambda b, i: (layer, 0, 0)),
                  tab, tab, tab],
        out_specs=[pl.BlockSpec((MLA_HEADS, tm, kd), lambda b, i: (0, row(b, i), 0)),
                   pl.BlockSpec((MLA_HEADS, tm, kd), lambda b, i: (0, row(b, i), 0)),
                   pl.BlockSpec((MLA_HEADS, tm, 2 * V_HEAD), lambda b, i: (0, row(b, i), 0))],
        out_shape=[jax.ShapeDtypeStruct((MLA_HEADS, r, kd), bf16),
                   jax.ShapeDtypeStruct((MLA_HEADS, r, kd), bf16),
                   jax.ShapeDtypeStruct((MLA_HEADS, r, 2 * V_HEAD), bf16)],
        compiler_params=_params(("parallel", "arbitrary"), 48),
        name="mla_prep",
    )(u, u, u, w_uq_p, w_ukv_p, q_norm_g, kv_norm_g, *tables)


ATTN_HEADS_PER_STEP = 2
ATTN_TILES_PER_BODY = 12


def _attn_kernel(q_ref, k_ref, v_ref, o_ref, s0_ref, s1_ref, p0_ref, p1_ref, *, tq):
    nh, lp, _ = k_ref.shape
    nq = lp // tq
    n_tiles = nh * nq
    s_refs, p_refs = (s0_ref, s1_ref), (p0_ref, p1_ref)

    @pl.when((pl.program_id(0) == 0) & (pl.program_id(1) == 0))
    def _():
        for ref in s_refs + p_refs:
            ref[...] = jnp.zeros_like(ref)

    def tile(x):
        x = jnp.clip(x, 0, n_tiles - 1)
        hh = x // nq
        return hh, pl.ds(pl.multiple_of((x - hh * nq) * tq, tq), tq)

    def scores(x, slot):
        hh, rows = tile(x)
        s_refs[slot][...] = _dot_nt(q_ref[hh, rows, :], k_ref[hh])

    def softmax(slot):
        s = s_refs[slot][...]
        p_refs[slot][...] = jnp.exp2(s - jnp.max(s, axis=1, keepdims=True)).astype(bf16)

    def weighted(x, slot):
        hh, rows = tile(x)
        acc = _dot(p_refs[slot][...], v_ref[hh])
        o_ref[hh, rows, :] = (acc[:, :V_HEAD] * (1.0 / acc[:, V_HEAD:V_HEAD + 1])).astype(o_ref.dtype)

    n_bodies = -(-(n_tiles + 2) // ATTN_TILES_PER_BODY)
    start = n_tiles + 2 - ATTN_TILES_PER_BODY * n_bodies

    def body(u, carry):
        for e in range(ATTN_TILES_PER_BODY):
            x = ATTN_TILES_PER_BODY * u + e + start
            weighted(x - 2, e % 2)
            softmax((e + 1) % 2)
            scores(x, e % 2)
        return carry

    lax.fori_loop(0, n_bodies, body, 0)


def _attention(qc, kc, v, bsz, lp):
    r = v.shape[1]
    tq = _pick(lp, (384, 128))
    kd = qc.shape[-1]
    nh = ATTN_HEADS_PER_STEP
    blk = lambda width: pl.BlockSpec((nh, lp, width), lambda b, h: (h, b, 0))
    return pl.pallas_call(
        functools.partial(_attn_kernel, tq=tq),
        grid=(bsz, MLA_HEADS // nh),
        in_specs=[blk(kd), blk(kd), blk(2 * V_HEAD)],
        out_specs=blk(V_HEAD),
        out_shape=jax.ShapeDtypeStruct((MLA_HEADS, r, V_HEAD), bf16),
        scratch_shapes=[pltpu.VMEM((tq, lp), f32), pltpu.VMEM((tq, lp), f32),
                        pltpu.VMEM((tq, lp), bf16), pltpu.VMEM((tq, lp), bf16)],
        compiler_params=_params(("arbitrary", "arbitrary"), 60),
        name="attention",
    )(qc, kc, v)


def _layer_norm(y, g, b):
    mu = jnp.mean(y, axis=-1, keepdims=True)
    var = jnp.mean(jnp.square(y - mu), axis=-1, keepdims=True)
    return (y - mu) * lax.rsqrt(var + EPS) * g + b


OUTPROJ_SUBTILES = 2


def _outproj_kernel(ml_ref, at_ref, w_ref, h_ref, g_ref, b_ref, o_ref, ob_ref, *, alpha):
    ts = h_ref.shape[0] // OUTPROJ_SUBTILES
    for sub in range(OUTPROJ_SUBTILES):
        rows = slice(sub * ts, (sub + 1) * ts)
        at = jnp.concatenate([at_ref[h, rows, :] for h in range(MLA_HEADS)], axis=1)
        mix = _dot(ml_ref[rows, :], w_ref[:ML_WIDTH, :]) + _dot(at, w_ref[ML_WIDTH:, :])
        y = _layer_norm(alpha * h_ref[rows, :] + mix, g_ref[...], b_ref[...])
        o_ref[rows, :] = y
        ob_ref[rows, :] = y.astype(bf16)


def _outproj(ml, at, w_out, h, ln_g, ln_b, layer, alpha):
    r, d = h.shape
    tm = _pick(r, (768, 384, 128))
    vec = pl.BlockSpec((None, 1, d), lambda i: (layer, 0, 0))
    return pl.pallas_call(
        functools.partial(_outproj_kernel, alpha=alpha),
        grid=(r // tm,),
        in_specs=[pl.BlockSpec((tm, ML_WIDTH), lambda i: (i, 0)),
                  pl.BlockSpec((MLA_HEADS, tm, V_HEAD), lambda i: (0, i, 0)),
                  pl.BlockSpec((None,) + w_out.shape[1:], lambda i: (layer, 0, 0), pipeline_mode=pl.Buffered(1)),
                  pl.BlockSpec((tm, d), lambda i: (i, 0)), vec, vec],
        out_specs=[pl.BlockSpec((tm, d), lambda i: (i, 0)), pl.BlockSpec((tm, d), lambda i: (i, 0))],
        out_shape=[jax.ShapeDtypeStruct((r, d), f32), jax.ShapeDtypeStruct((r, d), bf16)],
        compiler_params=_params(("parallel",), 56),
        name="outproj_ln",
    )(ml, at, w_out, h, ln_g, ln_b)


HALO = 16


def _ffn_up_kernel(prev_ref, main_ref, next_ref, wg_ref, wv_ref, cw_ref, cb_ref, o_ref, ext_scr, gate_scr, *, lp):
    tm = main_ref.shape[0]

    @pl.when(pl.program_id(2) == 0)
    def _():
        ext_scr[:HALO, :] = prev_ref[...]
        ext_scr[HALO:HALO + tm, :] = main_ref[...]
        ext_scr[HALO + tm:, :] = next_ref[...]

    def compute(lo):
        pos = pl.program_id(1) * tm - HALO + lo + lax.broadcasted_iota(jnp.int32, (tm + 2 * HALO - lo, 1), 0)
        inside = (pos >= FRONT) & (pos < lp)
        gate_scr[lo:, :] = jnp.where(inside, _dot(ext_scr[lo:, :], wg_ref[...]), 0.0)
        taps = [gate_scr[pl.ds(lo + HALO - 1 + j, tm - lo), :] * cw_ref[j:j + 1, :] for j in range(3)]
        conv = taps[0] + taps[1] + taps[2] + cb_ref[...]
        val = _dot(main_ref[lo:, :], wv_ref[...])
        o_ref[lo:, :] = (conv * jax.nn.sigmoid(conv) * val).astype(o_ref.dtype)

    @pl.when(pl.program_id(1) == 0)
    def _():
        o_ref[:FRONT, :] = jnp.zeros((FRONT, o_ref.shape[1]), o_ref.dtype)
        compute(FRONT)

    @pl.when(pl.program_id(1) != 0)
    def _():
        compute(0)


def _ffn_up(hb, w_up, conv_w, conv_b, layer, bsz, lp):
    r, d = hb.shape
    dff = conv_b.shape[-1]
    tm = _pick(lp, (1408, 384, 128))
    tn = _pick(dff, (256, 128))
    nt, nj = lp // tm, dff // tn
    hblk = tm // HALO
    last = r // HALO - 1
    return pl.pallas_call(
        functools.partial(_ffn_up_kernel, lp=lp),
        grid=(bsz, nt, nj),
        in_specs=[pl.BlockSpec((HALO, d), lambda b, i, j: (jnp.maximum((b * nt + i) * hblk - 1, 0), 0)),
                  pl.BlockSpec((tm, d), lambda b, i, j: (b * nt + i, 0)),
                  pl.BlockSpec((HALO, d), lambda b, i, j: (jnp.minimum((b * nt + i + 1) * hblk, last), 0)),
                  pl.BlockSpec((None, d, tn), lambda b, i, j: (layer, 0, j)),
                  pl.BlockSpec((None, d, tn), lambda b, i, j: (layer, 0, nj + j)),
                  pl.BlockSpec((None, 3, tn), lambda b, i, j: (layer, 0, j)),
                  pl.BlockSpec((None, 1, tn), lambda b, i, j: (layer, 0, j))],
        out_specs=pl.BlockSpec((tm, tn), lambda b, i, j: (b * nt + i, j)),
        out_shape=jax.ShapeDtypeStruct((r, dff), bf16),
        scratch_shapes=[pltpu.VMEM((tm + 2 * HALO, d), bf16), pltpu.VMEM((tm + 2 * HALO, tn), f32)],
        compiler_params=_params(("parallel", "parallel", "arbitrary"), 56),
        name="ffn_up_conv",
    )(hb, hb, hb, w_up, w_up, conv_w, conv_b)


FFN_DOWN_SUBTILES = 2


def _ffn_down_kernel(a_ref, w_ref, h_ref, g_ref, b_ref, o_ref, ob_ref, *, alpha, tiles_per_seq):
    tm = h_ref.shape[0]

    def compute(rows):
        y = _layer_norm(alpha * h_ref[rows, :] + _dot(a_ref[rows, :], w_ref[...]), g_ref[...], b_ref[...])
        o_ref[rows, :] = y
        ob_ref[rows, :] = y.astype(bf16)

    first = pl.program_id(0) % tiles_per_seq == 0

    @pl.when(first)
    def _():
        o_ref[:FRONT, :] = jnp.zeros((FRONT, o_ref.shape[1]), o_ref.dtype)
        ob_ref[:FRONT, :] = jnp.zeros((FRONT, ob_ref.shape[1]), ob_ref.dtype)
        compute(slice(FRONT, tm))

    @pl.when(jnp.logical_not(first))
    def _():
        ts = tm // FFN_DOWN_SUBTILES
        for sub in range(FFN_DOWN_SUBTILES):
            compute(slice(sub * ts, (sub + 1) * ts))


def _ffn_down(act, w_down, h, ln_g, ln_b, layer, alpha, lp):
    r, d = h.shape
    dff = act.shape[1]
    tm = _pick(lp, (384, 128))
    vec = pl.BlockSpec((None, 1, d), lambda i: (layer, 0, 0))
    return pl.pallas_call(
        functools.partial(_ffn_down_kernel, alpha=alpha, tiles_per_seq=lp // tm),
        grid=(r // tm,),
        in_specs=[pl.BlockSpec((tm, dff), lambda i: (i, 0)),
                  pl.BlockSpec((None, dff, d), lambda i: (layer, 0, 0), pipeline_mode=pl.Buffered(1)),
                  pl.BlockSpec((tm, d), lambda i: (i, 0)), vec, vec],
        out_specs=[pl.BlockSpec((tm, d), lambda i: (i, 0)), pl.BlockSpec((tm, d), lambda i: (i, 0))],
        out_shape=[jax.ShapeDtypeStruct((r, d), f32), jax.ShapeDtypeStruct((r, d), bf16)],
        compiler_params=_params(("parallel",), 60),
        name="ffn_down_ln",
    )(act, w_down, h, ln_g, ln_b)


def _rope_tables(lp):
    inv_freq = ROPE_THETA ** (-jnp.arange(0, QK_ROPE, 2, dtype=f32) / QK_ROPE)
    pos = jnp.maximum(jnp.arange(lp, dtype=f32) - FRONT, 0.0)
    ang = pos[:, None] * inv_freq[None, :]
    cos, sin = jnp.cos(ang), jnp.sin(ang)
    zero = jnp.zeros_like(sin)
    reps = LANE // QK_ROPE
    cos_t = jnp.tile(jnp.concatenate([cos, cos], axis=1), (1, reps))
    sin_a = jnp.tile(jnp.concatenate([-sin, zero], axis=1), (1, reps))
    sin_b = jnp.tile(jnp.concatenate([zero, sin], axis=1), (1, reps))
    return cos_t, sin_a, sin_b


def kernel(x_prompt, x_sample, meta_tokens, w_in, b_gates, ml_norm_g, q_norm_g, kv_norm_g, w_uq, w_ukv,
           w_out, ln1_g, ln1_b, w_up, conv_w, conv_b, w_down, ln2_g, ln2_b):
    depth, d, _ = w_in.shape
    assert x_prompt.shape[1:] == x_sample.shape[1:]
    n_prompt = x_prompt.shape[0]
    x = jnp.concatenate([x_prompt, x_sample], axis=0)
    bsz, seq, _ = x.shape
    assert seq % LANE == 0
    lp = LANE + seq
    alpha = (2 * depth) ** 0.25

    gate_lo = 4 * ML_WIDTH
    w_main = jnp.concatenate(
        [w_in[..., :ML_WIDTH], w_in[..., 2 * ML_WIDTH:gate_lo], w_in[..., gate_lo + N_GATES:], w_in[..., -QK_ROPE:],
         jnp.zeros((depth, d, N_MAIN - COL_KR - 2 * QK_ROPE), w_in.dtype)], axis=-1).astype(bf16)
    wk_t = jnp.swapaxes(w_in[..., ML_WIDTH:2 * ML_WIDTH], 1, 2).astype(bf16)
    h4 = ML_HEADS
    perm = jnp.array(list(range(0, h4)) + list(range(2 * h4, 3 * h4))
                     + list(range(h4, 2 * h4)) + list(range(3 * h4, 4 * h4)))
    wg_t = jnp.swapaxes(w_in[..., gate_lo:gate_lo + N_GATES], 1, 2)[:, perm, :].astype(bf16)
    bg = b_gates[:, perm, None].astype(f32)
    uq = w_uq.reshape(depth, Q_LORA, MLA_HEADS, QK_NOPE + QK_ROPE)
    w_uq_p = jnp.concatenate([uq[..., :QK_NOPE].reshape(depth, Q_LORA, -1),
                              uq[..., QK_NOPE:].reshape(depth, Q_LORA, -1)], axis=-1).astype(bf16)
    ukv = w_ukv.reshape(depth, KV_LORA, MLA_HEADS, QK_NOPE + V_HEAD)
    w_ukv_p = jnp.concatenate([ukv[..., :QK_NOPE].reshape(depth, KV_LORA, -1),
                               ukv[..., QK_NOPE:].reshape(depth, KV_LORA, -1)], axis=-1).astype(bf16)
    w_out_b, w_up_b, w_down_b = w_out.astype(bf16), w_up.astype(bf16), w_down.astype(bf16)
    vec3 = lambda a: a[:, None, :].astype(f32)
    ml_g, q_g, kv_g = vec3(ml_norm_g), vec3(q_norm_g), vec3(kv_norm_g)
    l1g, l1b, l2g, l2b, cb = vec3(ln1_g), vec3(ln1_b), vec3(ln2_g), vec3(ln2_b), vec3(conv_b)
    cw = conv_w.astype(f32)
    tables = _rope_tables(lp)

    meta = jnp.broadcast_to(meta_tokens.astype(x.dtype)[None], (bsz, N_META, d))
    h = jnp.concatenate([jnp.zeros((bsz, FRONT, d), x.dtype), meta, x], axis=1).reshape(bsz * lp, d)
    hb = h.astype(bf16)
    for layer in range(depth):
        u = _inproj(hb, w_main, layer, lp)
        gq = _gates(hb, wg_t, bg, layer, bsz, lp)
        kt = _keys_t(hb, wk_t, layer)
        ml = _mlstm(u, kt, gq, ml_g, layer, bsz, lp)
        qc, kc, v = _mla_prep(u, w_uq_p, w_ukv_p, q_g, kv_g, tables, layer, bsz, lp)
        at = _attention(qc, kc, v, bsz, lp)
        h, hb = _outproj(ml, at, w_out_b, h, l1g, l1b, layer, alpha)
        act = _ffn_up(hb, w_up_b, cw, cb, layer, bsz, lp)
        h, hb = _ffn_down(act, w_down_b, h, l2g, l2b, layer, alpha, lp)
    y = h.reshape(bsz, lp, d)[:, LANE:]
    return (y[:n_prompt], y[n_prompt:])
```

```python
import functools
import math

import jax
import jax.numpy as jnp
from jax import lax
from jax.experimental import pallas as pl
from jax.experimental.pallas import tpu as pltpu

N_META = 16
ML_HEADS = 4
ML_HEAD_DIM = 256
ML_WIDTH = ML_HEADS * ML_HEAD_DIM
N_GATES = 4 * ML_HEADS
MLA_HEADS = 8
Q_LORA = 512
KV_LORA = 256
QK_NOPE = 128
QK_ROPE = 64
V_HEAD = 128
ROPE_THETA = 10000.0
EPS = 1e-5
NEG = -1e30

LANE = 128
FRONT = LANE - N_META
CHUNK = 128
ROPE_HALF = QK_ROPE // 2
SOFTMAX_SCALE_LOG2 = (QK_NOPE + QK_ROPE) ** -0.5 * math.log2(math.e)
MIB = 1024 * 1024

COL_Q, COL_V, COL_O = 0, ML_WIDTH, 2 * ML_WIDTH
COL_DQ = 3 * ML_WIDTH
COL_DKV = COL_DQ + Q_LORA
COL_KR = COL_DKV + KV_LORA
N_MAIN = COL_KR + 2 * QK_ROPE + LANE

f32 = jnp.float32
bf16 = jnp.bfloat16


def _pick(n, cands):
    for c in cands:
        if n % c == 0:
            return c
    raise ValueError(f"no tile for {n} in {cands}")


def _params(sem, vmem_mib):
    return pltpu.CompilerParams(dimension_semantics=sem, vmem_limit_bytes=vmem_mib * MIB)


def _dot(a, b):
    return jnp.dot(a, b, preferred_element_type=f32)


def _dot_nt(a, b):
    return lax.dot_general(a, b, (((1,), (1,)), ((), ())), preferred_element_type=f32)


def _inproj_kernel(a_ref, w_ref, o_ref, *, tiles_per_seq):
    first = pl.program_id(0) % tiles_per_seq == 0

    @pl.when(first)
    def _():
        o_ref[:FRONT, :] = jnp.zeros((FRONT, o_ref.shape[1]), o_ref.dtype)
        o_ref[FRONT:, :] = _dot(a_ref[FRONT:, :], w_ref[...]).astype(o_ref.dtype)

    @pl.when(jnp.logical_not(first))
    def _():
        o_ref[...] = _dot(a_ref[...], w_ref[...]).astype(o_ref.dtype)


def _keys_t_kernel(w_ref, h_ref, o_ref):
    o_ref[...] = (_dot_nt(w_ref[...], h_ref[...]) * ML_HEAD_DIM ** -0.5).astype(o_ref.dtype)


def _keys_t(hb, wk_t, layer):
    r, d = hb.shape
    tm = _pick(r, (1536, 384, 128))
    return pl.pallas_call(
        _keys_t_kernel,
        grid=(r // tm,),
        in_specs=[pl.BlockSpec((None, ML_WIDTH, d), lambda i: (layer, 0, 0)),
                  pl.BlockSpec((tm, d), lambda i: (i, 0))],
        out_specs=pl.BlockSpec((ML_WIDTH, tm), lambda i: (0, i)),
        out_shape=jax.ShapeDtypeStruct((ML_WIDTH, r), bf16),
        compiler_params=_params(("parallel",), 48),
        name="keys_t",
    )(wk_t, hb)


def _inproj(hb, w_main, layer, lp):
    r, d = hb.shape
    n = w_main.shape[-1]
    tm = _pick(lp, (1408, 384, 128))
    tn = _pick(n, (1024, 512, 256, 128))
    return pl.pallas_call(
        functools.partial(_inproj_kernel, tiles_per_seq=lp // tm),
        grid=(r // tm, n // tn),
        in_specs=[pl.BlockSpec((tm, d), lambda i, j: (i, 0)),
                  pl.BlockSpec((None, d, tn), lambda i, j: (layer, 0, j))],
        out_specs=pl.BlockSpec((tm, tn), lambda i, j: (i, j)),
        out_shape=jax.ShapeDtypeStruct((r, n), bf16),
        compiler_params=_params(("parallel", "arbitrary"), 48),
        name="inproj",
    )(hb, w_main)


def _gates_kernel(h_ref, wg_ref, bg_ref, o_ref):
    tm = h_ref.shape[0]
    half = N_GATES // 2
    g = _dot_nt(wg_ref[...], h_ref[...]) + bg_ref[...]
    pos = pl.program_id(1) * tm + lax.broadcasted_iota(jnp.int32, (half, tm), 1)
    unused = pos < FRONT
    gi, gf = g[:half], g[half:]
    li = jnp.where(unused, NEG, gi)
    lf = jnp.where(unused, 0.0, jnp.minimum(gf, 0.0) - jnp.log1p(jnp.exp(-jnp.abs(gf))))
    lane = lax.broadcasted_iota(jnp.int32, (half, CHUNK), 1)
    is_fwd = lax.broadcasted_iota(jnp.int32, (half, CHUNK), 0) < ML_HEADS
    for c in range(tm // CHUNK):
        sl = slice(c * CHUNK, (c + 1) * CHUNK)
        pre = suf = lf[:, sl]
        k = 1
        while k < CHUNK:
            pre = pre + jnp.where(lane >= k, pltpu.roll(pre, k, 1), 0.0)
            suf = suf + jnp.where(lane < CHUNK - k, pltpu.roll(suf, CHUNK - k, 1), 0.0)
            k *= 2
        b = jnp.where(is_fwd, pre, suf)
        o_ref[:half, sl] = li[:, sl] - b
        o_ref[half:, sl] = b


def _gates(hb, wg_t, bg, layer, bsz, lp):
    r, d = hb.shape
    tm = _pick(lp, (1408, 384, 128))
    nt = lp // tm
    return pl.pallas_call(
        _gates_kernel,
        grid=(bsz, nt),
        in_specs=[pl.BlockSpec((tm, d), lambda b, i: (b * nt + i, 0)),
                  pl.BlockSpec((None, N_GATES, d), lambda b, i: (layer, 0, 0)),
                  pl.BlockSpec((None, N_GATES, 1), lambda b, i: (layer, 0, 0))],
        out_specs=pl.BlockSpec((None, N_GATES, tm), lambda b, i: (b, 0, i)),
        out_shape=jax.ShapeDtypeStruct((bsz, N_GATES, lp), f32),
        compiler_params=_params(("parallel", "arbitrary"), 32),
        name="gates",
    )(hb, wg_t, bg)


MLSTM_UNROLL = 16


def _mlstm_kernel(q_ref, kt_ref, v_ref, o_ref, gq_ref, ng_ref, out_ref, hf_scr, hb_scr, c_scr, m_scr, s_scr):
    head = pl.program_id(1)
    lp, dh = q_ref.shape
    t = CHUNK
    nc = lp // t
    c_scr[...] = jnp.zeros_like(c_scr)
    m_scr[...] = jnp.zeros_like(m_scr)
    ti = lax.broadcasted_iota(jnp.int32, (t, t), 0)
    si = lax.broadcasted_iota(jnp.int32, (t, t), 1)
    masks = (si <= ti, si >= ti)
    eye = si == ti
    half = N_GATES // 2
    gate_row = lax.broadcasted_iota(jnp.int32, (half, t), 0)
    ones_col = jnp.where(lax.broadcasted_iota(jnp.int32, (t, LANE), 1) == 0, 1.0, 0.0).astype(bf16)

    def qk_ahead(c, dirn):
        base = pl.multiple_of(jnp.clip(c, 0, nc - 1) * t, t)
        s_scr[dirn] = _dot(q_ref[pl.ds(base, t), :], kt_ref[:, pl.ds(base, t)])

    def chunk(c, dirn, h_scr):
        base = pl.multiple_of(c * t, t)
        q = q_ref[pl.ds(base, t), :]
        kt = kt_ref[:, pl.ds(base, t)]
        v = jnp.concatenate([v_ref[pl.ds(base, t), :], ones_col], axis=1)
        mine = gate_row == head + ML_HEADS * dirn
        r_row = jnp.sum(jnp.where(mine, gq_ref[:half, pl.ds(base, t)], 0.0), axis=0, keepdims=True)
        b_row = jnp.sum(jnp.where(mine, gq_ref[half:, pl.ds(base, t)], 0.0), axis=0, keepdims=True)
        m_state = m_scr[dirn, 0:1, 0:1]
        mask = masks[dirn]
        m_col = jnp.maximum(jnp.max(jnp.where(mask, r_row, -jnp.inf), axis=1, keepdims=True), m_state)
        b_col = jnp.sum(jnp.where(eye, b_row, 0.0), axis=1, keepdims=True)
        m_all = jnp.maximum(jnp.max(r_row, axis=1, keepdims=True), m_state)
        b_last = b_row[:, 0:1] if dirn else b_row[:, t - 1:t]
        s = s_scr[dirn] * jnp.exp(jnp.where(mask, r_row - m_col, -jnp.inf))
        both = jnp.exp(m_state - m_col) * _dot(q, c_scr[dirn].astype(bf16)) + _dot(s.astype(bf16), v)
        floor = jnp.exp(-(b_col + m_col))
        h_scr[pl.ds(base, t), :] = both[:, :dh] / jnp.maximum(jnp.abs(both[:, dh:dh + 1]), floor)
        ktw = (kt.astype(f32) * jnp.exp(r_row - m_all)).astype(bf16)
        c_scr[dirn] = jnp.exp(m_state - m_all) * c_scr[dirn] + _dot(ktw, v)
        m_scr[dirn] = jnp.broadcast_to(b_last + m_all, m_scr.shape[1:])

    def finish(c):
        base = pl.multiple_of(c * t, t)
        hs = hf_scr[pl.ds(base, t), :] + hb_scr[pl.ds(base, t), :]
        mu = jnp.mean(hs, axis=1, keepdims=True)
        var = jnp.mean(jnp.square(hs - mu), axis=1, keepdims=True)
        hn = (hs - mu) * lax.rsqrt(var + EPS) * ng_ref[...]
        gate = jax.nn.sigmoid(o_ref[pl.ds(base, t), :].astype(f32))
        out_ref[pl.ds(base, t), :] = (gate * hn).astype(out_ref.dtype)

    def scan_step(c):
        chunk(c, 0, hf_scr)
        chunk(nc - 1 - c, 1, hb_scr)
        qk_ahead(c + 1, 0)
        qk_ahead(nc - 2 - c, 1)

    def first_half(c, carry):
        scan_step(c)
        return carry

    def second_half(c, carry):
        finish(c - 1)
        finish(nc - c)
        scan_step(c)
        return carry

    qk_ahead(0, 0)
    qk_ahead(nc - 1, 1)
    meet = (nc + 1) // 2
    lax.fori_loop(0, meet, first_half, 0, unroll=MLSTM_UNROLL)
    lax.fori_loop(meet, nc, second_half, 0, unroll=MLSTM_UNROLL)
    finish(nc - 1)
    finish(0)


def _mlstm(u, kt, gq, ml_norm_g, layer, bsz, lp):
    r = u.shape[0]
    dh = ML_HEAD_DIM

    def col(first):
        return pl.BlockSpec((lp, dh), lambda b, h: (b, first // dh + h))

    return pl.pallas_call(
        _mlstm_kernel,
        grid=(bsz, ML_HEADS),
        in_specs=[col(COL_Q), pl.BlockSpec((dh, lp), lambda b, h: (h, b)), col(COL_V), col(COL_O),
                  pl.BlockSpec((None, N_GATES, lp), lambda b, h: (b, 0, 0)),
                  pl.BlockSpec((None, 1, dh), lambda b, h: (layer, 0, h))],
        out_specs=pl.BlockSpec((lp, dh), lambda b, h: (b, h)),
        out_shape=jax.ShapeDtypeStruct((r, ML_WIDTH), bf16),
        scratch_shapes=[pltpu.VMEM((lp, dh), f32), pltpu.VMEM((lp, dh), f32),
                        pltpu.VMEM((2, dh, dh + LANE), f32), pltpu.VMEM((2, 8, LANE), f32),
                        pltpu.VMEM((2, CHUNK, CHUNK), f32)],
        compiler_params=_params(("parallel", "arbitrary"), 48),
        name="mlstm",
    )(u, kt, u, u, gq, ml_norm_g)


def _rms(x, g):
    return x * lax.rsqrt(jnp.mean(jnp.square(x), axis=-1, keepdims=True) + EPS) * g


def _rope(x, cos, sin_a, sin_b):
    n = x.shape[1]
    reps = n // LANE
    if reps > 1:
        cos, sin_a, sin_b = (jnp.concatenate([a] * reps, axis=1) for a in (cos, sin_a, sin_b))
    return x * cos + pltpu.roll(x, n - ROPE_HALF, 1) * sin_a + pltpu.roll(x, ROPE_HALF, 1) * sin_b


def _mla_prep_kernel(dq_ref, dkv_ref, kr_ref, wuq_ref, wukv_ref, qg_ref, kvg_ref,
                     cos_ref, sa_ref, sb_ref, qc_ref, kc_ref, v_ref):
    cos, sa, sb = cos_ref[...], sa_ref[...], sb_ref[...]
    q = _dot(_rms(dq_ref[...].astype(f32), qg_ref[...]).astype(bf16), wuq_ref[...]) * SOFTMAX_SCALE_LOG2
    kv =_dot(_rms(dkv_ref[...].astype(f32), kvg_ref[...]).astype(bf16), wukv_ref[...])
    q_rope = _rope(q[:, MLA_HEADS * QK_NOPE:], cos, sa, sb)
    k_rope = _rope(kr_ref[...].astype(f32), cos, sa, sb)
    tm = q.shape[0]
    lane = lax.broadcasted_iota(jnp.int32, (tm, LANE), 1)
    pos = pl.program_id(1) * tm + lax.broadcasted_iota(jnp.int32, (tm, LANE), 0)
    low = lane < QK_ROPE
    marker = lane == QK_ROPE
    one_hot = jnp.where(marker, 1.0, 0.0)
    k_tail = jnp.where(low, k_rope, jnp.where(marker & (pos < FRONT), -jnp.inf, 0.0)).astype(bf16)
    ones_col = jnp.where(lane == 0, 1.0, 0.0).astype(bf16)
    for h in range(MLA_HEADS):
        pair = q_rope[:, (h // 2) * LANE:(h // 2 + 1) * LANE]
        if h % 2:
            pair = pltpu.roll(pair, QK_ROPE, 1)
        qc_ref[h, :, :QK_NOPE] = q[:, h * QK_NOPE:(h + 1) * QK_NOPE].astype(bf16)
        qc_ref[h, :, QK_NOPE:] = jnp.where(low, pair, one_hot).astype(bf16)
        kc_ref[h, :, :QK_NOPE] = kv[:, h * QK_NOPE:(h + 1) * QK_NOPE].astype(bf16)
        kc_ref[h, :, QK_NOPE:] = k_tail
        v_ref[h, :, :V_HEAD] = kv[:, (MLA_HEADS + h) * V_HEAD:(MLA_HEADS + h + 1) * V_HEAD].astype(bf16)
        v_ref[h, :, V_HEAD:] = ones_col


def _mla_prep(u, w_uq_p, w_ukv_p, q_norm_g, kv_norm_g, tables, layer, bsz, lp):
    r = u.shape[0]
    tm = _pick(lp, (384, 128))
    nt = lp // tm
    kd = QK_NOPE + LANE
    tab = pl.BlockSpec((tm, LANE), lambda b, i: (i, 0))
    row = lambda b, i: b * nt + i
    return pl.pallas_call(
        _mla_prep_kernel,
        grid=(bsz, nt),
        in_specs=[pl.BlockSpec((tm, Q_LORA), lambda b, i: (row(b, i), COL_DQ // Q_LORA)),
                  pl.BlockSpec((tm, KV_LORA), lambda b, i: (row(b, i), COL_DKV // KV_LORA)),
                  pl.BlockSpec((tm, LANE), lambda b, i: (row(b, i), COL_KR // LANE)),
                  pl.BlockSpec((None,) + w_uq_p.shape[1:], lambda b, i: (layer, 0, 0)),
                  pl.BlockSpec((None,) + w_ukv_p.shape[1:], lambda b, i: (layer, 0, 0)),
                  pl.BlockSpec((None, 1, Q_LORA), lambda b, i: (layer, 0, 0)),
                  pl.BlockSpec((None, 1, KV_LORA), lambda b, i: (layer, 0, 0)),
                  tab, tab, tab],
        out_specs=[pl.BlockSpec((MLA_HEADS, tm, kd), lambda b, i: (0, row(b, i), 0)),
                   pl.BlockSpec((MLA_HEADS, tm, kd), lambda b, i: (0, row(b, i), 0)),
                   pl.BlockSpec((MLA_HEADS, tm, 2 * V_HEAD), lambda b, i: (0, row(b, i), 0))],
        out_shape=[jax.ShapeDtypeStruct((MLA_HEADS, r, kd), bf16),
                   jax.ShapeDtypeStruct((MLA_HEADS, r, kd), bf16),
                   jax.ShapeDtypeStruct((MLA_HEADS, r, 2 * V_HEAD), bf16)],
        compiler_params=_params(("parallel", "arbitrary"), 48),
        name="mla_prep",
    )(u, u, u, w_uq_p, w_ukv_p, q_norm_g, kv_norm_g, *tables)


ATTN_HEADS_PER_STEP = 2
ATTN_TILES_PER_BODY = 12


def _attn_kernel(q_ref, k_ref, v_ref, o_ref, s0_ref, s1_ref, p0_ref, p1_ref, *, tq):
    nh, lp, _ = k_ref.shape
    nq = lp // tq
    n_tiles = nh * nq
    s_refs, p_refs = (s0_ref, s1_ref), (p0_ref, p1_ref)

    @pl.when((pl.program_id(0) == 0) & (pl.program_id(1) == 0))
    def _():
        for ref in s_refs + p_refs:
            ref[...] = jnp.zeros_like(ref)

    def tile(x):
        x = jnp.clip(x, 0, n_tiles - 1)
        hh = x // nq
        return hh, pl.ds(pl.multiple_of((x - hh * nq) * tq, tq), tq)

    def scores(x, slot):
        hh, rows = tile(x)
        s_refs[slot][...] = _dot_nt(q_ref[hh, rows, :], k_ref[hh])

    def softmax(slot):
        s = s_refs[slot][...]
        p_refs[slot][...] = jnp.exp2(s - jnp.max(s, axis=1, keepdims=True)).astype(bf16)

    def weighted(x, slot):
        hh, rows = tile(x)
        acc = _dot(p_refs[slot][...], v_ref[hh])
        o_ref[hh, rows, :] = (acc[:, :V_HEAD] * (1.0 / acc[:, V_HEAD:V_HEAD + 1])).astype(o_ref.dtype)

    n_bodies = -(-(n_tiles + 2) // ATTN_TILES_PER_BODY)
    start = n_tiles + 2 - ATTN_TILES_PER_BODY * n_bodies

    def body(u, carry):
        for e in range(ATTN_TILES_PER_BODY):
            x = ATTN_TILES_PER_BODY * u + e + start
            weighted(x - 2, e % 2)
            softmax((e + 1) % 2)
            scores(x, e % 2)
        return carry

    lax.fori_loop(0, n_bodies, body, 0)


def _attention(qc, kc, v, bsz, lp):
    r = v.shape[1]
    tq = _pick(lp, (384, 128))
    kd = qc.shape[-1]
    nh = ATTN_HEADS_PER_STEP
    blk = lambda width: pl.BlockSpec((nh, lp, width), lambda b, h: (h, b, 0))
    return pl.pallas_call(
        functools.partial(_attn_kernel, tq=tq),
        grid=(bsz, MLA_HEADS // nh),
        in_specs=[blk(kd), blk(kd), blk(2 * V_HEAD)],
        out_specs=blk(V_HEAD),
        out_shape=jax.ShapeDtypeStruct((MLA_HEADS, r, V_HEAD), bf16),
        scratch_shapes=[pltpu.VMEM((tq, lp), f32), pltpu.VMEM((tq, lp), f32),
                        pltpu.VMEM((tq, lp), bf16), pltpu.VMEM((tq, lp), bf16)],
        compiler_params=_params(("arbitrary", "arbitrary"), 60),
        name="attention",
    )(qc, kc, v)


def _layer_norm(y, g, b):
    mu = jnp.mean(y, axis=-1, keepdims=True)
    var = jnp.mean(jnp.square(y - mu), axis=-1, keepdims=True)
    return (y - mu) * lax.rsqrt(var + EPS) * g + b


OUTPROJ_SUBTILES = 2


def _outproj_kernel(ml_ref, at_ref, w_ref, h_ref, g_ref, b_ref, o_ref, ob_ref, *, alpha):
    ts = h_ref.shape[0] // OUTPROJ_SUBTILES
    for sub in range(OUTPROJ_SUBTILES):
        rows = slice(sub * ts, (sub + 1) * ts)
        at = jnp.concatenate([at_ref[h, rows, :] for h in range(MLA_HEADS)], axis=1)
        mix = _dot(ml_ref[rows, :], w_ref[:ML_WIDTH, :]) + _dot(at, w_ref[ML_WIDTH:, :])
        y = _layer_norm(alpha * h_ref[rows, :] + mix, g_ref[...], b_ref[...])
        o_ref[rows, :] = y
        ob_ref[rows, :] = y.astype(bf16)


def _outproj(ml, at, w_out, h, ln_g, ln_b, layer, alpha):
    r, d = h.shape
    tm = _pick(r, (768, 384, 128))
    vec = pl.BlockSpec((None, 1, d), lambda i: (layer, 0, 0))
    return pl.pallas_call(
        functools.partial(_outproj_kernel, alpha=alpha),
        grid=(r // tm,),
        in_specs=[pl.BlockSpec((tm, ML_WIDTH), lambda i: (i, 0)),
                  pl.BlockSpec((MLA_HEADS, tm, V_HEAD), lambda i: (0, i, 0)),
                  pl.BlockSpec((None,) + w_out.shape[1:], lambda i: (layer, 0, 0), pipeline_mode=pl.Buffered(1)),
                  pl.BlockSpec((tm, d), lambda i: (i, 0)), vec, vec],
        out_specs=[pl.BlockSpec((tm, d), lambda i: (i, 0)), pl.BlockSpec((tm, d), lambda i: (i, 0))],
        out_shape=[jax.ShapeDtypeStruct((r, d), f32), jax.ShapeDtypeStruct((r, d), bf16)],
        compiler_params=_params(("parallel",), 56),
        name="outproj_ln",
    )(ml, at, w_out, h, ln_g, ln_b)


HALO = 16


def _ffn_up_kernel(prev_ref, main_ref, next_ref, wg_ref, wv_ref, cw_ref, cb_ref, o_ref, ext_scr, gate_scr, *, lp):
    tm = main_ref.shape[0]

    @pl.when(pl.program_id(2) == 0)
    def _():
        ext_scr[:HALO, :] = prev_ref[...]
        ext_scr[HALO:HALO + tm, :] = main_ref[...]
        ext_scr[HALO + tm:, :] = next_ref[...]

    def compute(lo):
        pos = pl.program_id(1) * tm - HALO + lo + lax.broadcasted_iota(jnp.int32, (tm + 2 * HALO - lo, 1), 0)
        inside = (pos >= FRONT) & (pos < lp)
        gate_scr[lo:, :] = jnp.where(inside, _dot(ext_scr[lo:, :], wg_ref[...]), 0.0)
        taps = [gate_scr[pl.ds(lo + HALO - 1 + j, tm - lo), :] * cw_ref[j:j + 1, :] for j in range(3)]
        conv = taps[0] + taps[1] + taps[2] + cb_ref[...]
        val = _dot(main_ref[lo:, :], wv_ref[...])
        o_ref[lo:, :] = (conv * jax.nn.sigmoid(conv) * val).astype(o_ref.dtype)

    @pl.when(pl.program_id(1) == 0)
    def _():
        o_ref[:FRONT, :] = jnp.zeros((FRONT, o_ref.shape[1]), o_ref.dtype)
        compute(FRONT)

    @pl.when(pl.program_id(1) != 0)
    def _():
        compute(0)


def _ffn_up(hb, w_up, conv_w, conv_b, layer, bsz, lp):
    r, d = hb.shape
    dff = conv_b.shape[-1]
    tm = _pick(lp, (2112, 1408, 384, 128))
    tn = _pick(dff, (512, 256, 128))
    nt, nj = lp // tm, dff // tn
    hblk = tm // HALO
    last = r // HALO - 1
    return pl.pallas_call(
        functools.partial(_ffn_up_kernel, lp=lp),
        grid=(bsz, nt, nj),
        in_specs=[pl.BlockSpec((HALO, d), lambda b, i, j: (jnp.maximum((b * nt + i) * hblk - 1, 0), 0)),
                  pl.BlockSpec((tm, d), lambda b, i, j: (b * nt + i, 0)),
                  pl.BlockSpec((HALO, d), lambda b, i, j: (jnp.minimum((b * nt + i + 1) * hblk, last), 0)),
                  pl.BlockSpec((None, d, tn), lambda b, i, j: (layer, 0, j)),
                  pl.BlockSpec((None, d, tn), lambda b, i, j: (layer, 0, nj + j)),
                  pl.BlockSpec((None, 3, tn), lambda b, i, j: (layer, 0, j)),
                  pl.BlockSpec((None, 1, tn), lambda b, i, j: (layer, 0, j))],
        out_specs=pl.BlockSpec((tm, tn), lambda b, i, j: (b * nt + i, j)),
        out_shape=jax.ShapeDtypeStruct((r, dff), bf16),
        scratch_shapes=[pltpu.VMEM((tm + 2 * HALO, d), bf16), pltpu.VMEM((tm + 2 * HALO, tn), f32)],
        compiler_params=_params(("parallel", "parallel", "arbitrary"), 56),
        name="ffn_up_conv",
    )(hb, hb, hb, w_up, w_up, conv_w, conv_b)


FFN_DOWN_SUBTILES = 2


def _ffn_down_kernel(a_ref, w_ref, h_ref, g_ref, b_ref, o_ref, ob_ref, *, alpha, tiles_per_seq):
    tm = h_ref.shape[0]

    def compute(rows):
        y = _layer_norm(alpha * h_ref[rows, :] + _dot(a_ref[rows, :], w_ref[...]), g_ref[...], b_ref[...])
        o_ref[rows, :] = y
        ob_ref[rows, :] = y.astype(bf16)

    first = pl.program_id(0) % tiles_per_seq == 0

    @pl.when(first)
    def _():
        o_ref[:FRONT, :] = jnp.zeros((FRONT, o_ref.shape[1]), o_ref.dtype)
        ob_ref[:FRONT, :] = jnp.zeros((FRONT, ob_ref.shape[1]), ob_ref.dtype)
        compute(slice(FRONT, tm))

    @pl.when(jnp.logical_not(first))
    def _():
        ts = tm // FFN_DOWN_SUBTILES
        for sub in range(FFN_DOWN_SUBTILES):
            compute(slice(sub * ts, (sub + 1) * ts))


def _ffn_down(act, w_down, h, ln_g, ln_b, layer, alpha, lp):
    r, d = h.shape
    dff = act.shape[1]
    tm = _pick(lp, (384, 128))
    vec = pl.BlockSpec((None, 1, d), lambda i: (layer, 0, 0))
    return pl.pallas_call(
        functools.partial(_ffn_down_kernel, alpha=alpha, tiles_per_seq=lp // tm),
        grid=(r // tm,),
        in_specs=[pl.BlockSpec((tm, dff), lambda i: (i, 0)),
                  pl.BlockSpec((None, dff, d), lambda i: (layer, 0, 0), pipeline_mode=pl.Buffered(1)),
                  pl.BlockSpec((tm, d), lambda i: (i, 0)), vec, vec],
        out_specs=[pl.BlockSpec((tm, d), lambda i: (i, 0)), pl.BlockSpec((tm, d), lambda i: (i, 0))],
        out_shape=[jax.ShapeDtypeStruct((r, d), f32), jax.ShapeDtypeStruct((r, d), bf16)],
        compiler_params=_params(("parallel",), 60),
        name="ffn_down_ln",
    )(act, w_down, h, ln_g, ln_b)


def _rope_tables(lp):
    inv_freq = ROPE_THETA ** (-jnp.arange(0, QK_ROPE, 2, dtype=f32) / QK_ROPE)
    pos = jnp.maximum(jnp.arange(lp, dtype=f32) - FRONT, 0.0)
    ang = pos[:, None] * inv_freq[None, :]
    cos, sin = jnp.cos(ang), jnp.sin(ang)
    zero = jnp.zeros_like(sin)
    reps = LANE // QK_ROPE
    cos_t = jnp.tile(jnp.concatenate([cos, cos], axis=1), (1, reps))
    sin_a = jnp.tile(jnp.concatenate([-sin, zero], axis=1), (1, reps))
    sin_b = jnp.tile(jnp.concatenate([zero, sin], axis=1), (1, reps))
    return cos_t, sin_a, sin_b


def kernel(x_prompt, x_sample, meta_tokens, w_in, b_gates, ml_norm_g, q_norm_g, kv_norm_g, w_uq, w_ukv,
           w_out, ln1_g, ln1_b, w_up, conv_w, conv_b, w_down, ln2_g, ln2_b):
    depth, d, _ = w_in.shape
    assert x_prompt.shape[1:] == x_sample.shape[1:]
    n_prompt = x_prompt.shape[0]
    x = jnp.concatenate([x_prompt, x_sample], axis=0)
    bsz, seq, _ = x.shape
    assert seq % LANE == 0
    lp = LANE + seq
    alpha = (2 * depth) ** 0.25

    gate_lo = 4 * ML_WIDTH
    w_main = jnp.concatenate(
        [w_in[..., :ML_WIDTH], w_in[..., 2 * ML_WIDTH:gate_lo], w_in[..., gate_lo + N_GATES:], w_in[..., -QK_ROPE:],
         jnp.zeros((depth, d, N_MAIN - COL_KR - 2 * QK_ROPE), w_in.dtype)], axis=-1).astype(bf16)
    wk_t = jnp.swapaxes(w_in[..., ML_WIDTH:2 * ML_WIDTH], 1, 2).astype(bf16)
    h4 = ML_HEADS
    perm = jnp.array(list(range(0, h4)) + list(range(2 * h4, 3 * h4))
                     + list(range(h4, 2 * h4)) + list(range(3 * h4, 4 * h4)))
    wg_t = jnp.swapaxes(w_in[..., gate_lo:gate_lo + N_GATES], 1, 2)[:, perm, :].astype(bf16)
    bg = b_gates[:, perm, None].astype(f32)
    uq = w_uq.reshape(depth, Q_LORA, MLA_HEADS, QK_NOPE + QK_ROPE)
    w_uq_p = jnp.concatenate([uq[..., :QK_NOPE].reshape(depth, Q_LORA, -1),
                              uq[..., QK_NOPE:].reshape(depth, Q_LORA, -1)], axis=-1).astype(bf16)
    ukv = w_ukv.reshape(depth, KV_LORA, MLA_HEADS, QK_NOPE + V_HEAD)
    w_ukv_p = jnp.concatenate([ukv[..., :QK_NOPE].reshape(depth, KV_LORA, -1),
                               ukv[..., QK_NOPE:].reshape(depth, KV_LORA, -1)], axis=-1).astype(bf16)
    w_out_b, w_up_b, w_down_b = w_out.astype(bf16), w_up.astype(bf16), w_down.astype(bf16)
    vec3 = lambda a: a[:, None, :].astype(f32)
    ml_g, q_g, kv_g = vec3(ml_norm_g), vec3(q_norm_g), vec3(kv_norm_g)
    l1g, l1b, l2g, l2b, cb = vec3(ln1_g), vec3(ln1_b), vec3(ln2_g), vec3(ln2_b), vec3(conv_b)
    cw = conv_w.astype(f32)
    tables = _rope_tables(lp)

    meta = jnp.broadcast_to(meta_tokens.astype(x.dtype)[None], (bsz, N_META, d))
    h = jnp.concatenate([jnp.zeros((bsz, FRONT, d), x.dtype), meta, x], axis=1).reshape(bsz * lp, d)
    hb = h.astype(bf16)
    for layer in range(depth):
        u = _inproj(hb, w_main, layer, lp)
        gq = _gates(hb, wg_t, bg, layer, bsz, lp)
        kt = _keys_t(hb, wk_t, layer)
        ml = _mlstm(u, kt, gq, ml_g, layer, bsz, lp)
        qc, kc, v = _mla_prep(u, w_uq_p, w_ukv_p, q_g, kv_g, tables, layer, bsz, lp)
        at = _attention(qc, kc, v, bsz, lp)
        h, hb = _outproj(ml, at, w_out_b, h, l1g, l1b, layer, alpha)
        act = _ffn_up(hb, w_up_b, cw, cb, layer, bsz, lp)
        h, hb = _ffn_down(act, w_down_b, h, l2g, l2b, layer, alpha, lp)
    y = h.reshape(bsz, lp, d)[:, LANE:]
    return (y[:n_prompt], y[n_prompt:])
```

```python
import functools
import math

import jax
import jax.numpy as jnp
from jax import lax
from jax.experimental import pallas as pl
from jax.experimental.pallas import tpu as pltpu

N_META = 16
ML_HEADS = 4
ML_HEAD_DIM = 256
ML_WIDTH = ML_HEADS * ML_HEAD_DIM
N_GATES = 4 * ML_HEADS
MLA_HEADS = 8
Q_LORA = 512
KV_LORA = 256
QK_NOPE = 128
QK_ROPE = 64
V_HEAD = 128
ROPE_THETA = 10000.0
EPS = 1e-5
NEG = -1e30

LANE = 128
FRONT = LANE - N_META
CHUNK = 128
ROPE_HALF = QK_ROPE // 2
SOFTMAX_SCALE_LOG2 = (QK_NOPE + QK_ROPE) ** -0.5 * math.log2(math.e)
MIB = 1024 * 1024

COL_Q, COL_V, COL_O = 0, ML_WIDTH, 2 * ML_WIDTH
COL_DQ = 3 * ML_WIDTH
COL_DKV = COL_DQ + Q_LORA
COL_KR = COL_DKV + KV_LORA
N_MAIN = COL_KR + 2 * QK_ROPE + LANE

f32 = jnp.float32
bf16 = jnp.bfloat16


def _pick(n, cands):
    for c in cands:
        if n % c == 0:
            return c
    raise ValueError(f"no tile for {n} in {cands}")


def _params(sem, vmem_mib):
    return pltpu.CompilerParams(dimension_semantics=sem, vmem_limit_bytes=vmem_mib * MIB)


def _dot(a, b):
    return jnp.dot(a, b, preferred_element_type=f32)


def _dot_nt(a, b):
    return lax.dot_general(a, b, (((1,), (1,)), ((), ())), preferred_element_type=f32)


def _inproj_kernel(a_ref, w_ref, o_ref, *, tiles_per_seq):
    first = pl.program_id(0) % tiles_per_seq == 0

    @pl.when(first)
    def _():
        o_ref[:FRONT, :] = jnp.zeros((FRONT, o_ref.shape[1]), o_ref.dtype)
        o_ref[FRONT:, :] = _dot(a_ref[FRONT:, :], w_ref[...]).astype(o_ref.dtype)

    @pl.when(jnp.logical_not(first))
    def _():
        o_ref[...] = _dot(a_ref[...], w_ref[...]).astype(o_ref.dtype)


def _keys_t_kernel(w_ref, h_ref, o_ref):
    o_ref[...] = (_dot_nt(w_ref[...], h_ref[...]) * ML_HEAD_DIM ** -0.5).astype(o_ref.dtype)


def _keys_t(hb, wk_t, layer):
    r, d = hb.shape
    tm = _pick(r, (1536, 384, 128))
    return pl.pallas_call(
        _keys_t_kernel,
        grid=(r // tm,),
        in_specs=[pl.BlockSpec((None, ML_WIDTH, d), lambda i: (layer, 0, 0)),
                  pl.BlockSpec((tm, d), lambda i: (i, 0))],
        out_specs=pl.BlockSpec((ML_WIDTH, tm), lambda i: (0, i)),
        out_shape=jax.ShapeDtypeStruct((ML_WIDTH, r), bf16),
        compiler_params=_params(("parallel",), 48),
        name="keys_t",
    )(wk_t, hb)


def _inproj(hb, w_main, layer, lp):
    r, d = hb.shape
    n = w_main.shape[-1]
    tm = _pick(lp, (2112, 1408, 384, 128))
    tn = _pick(n, (1024, 512, 256, 128))
    return pl.pallas_call(
        functools.partial(_inproj_kernel, tiles_per_seq=lp // tm),
        grid=(r // tm, n // tn),
        in_specs=[pl.BlockSpec((tm, d), lambda i, j: (i, 0)),
                  pl.BlockSpec((None, d, tn), lambda i, j: (layer, 0, j))],
        out_specs=pl.BlockSpec((tm, tn), lambda i, j: (i, j)),
        out_shape=jax.ShapeDtypeStruct((r, n), bf16),
        compiler_params=_params(("parallel", "arbitrary"), 56),
        name="inproj",
    )(hb, w_main)


def _gates_kernel(h_ref, wg_ref, bg_ref, o_ref):
    tm = h_ref.shape[0]
    half = N_GATES // 2
    g = _dot_nt(wg_ref[...], h_ref[...]) + bg_ref[...]
    pos = pl.program_id(1) * tm + lax.broadcasted_iota(jnp.int32, (half, tm), 1)
    unused = pos < FRONT
    gi, gf = g[:half], g[half:]
    li = jnp.where(unused, NEG, gi)
    lf = jnp.where(unused, 0.0, jnp.minimum(gf, 0.0) - jnp.log1p(jnp.exp(-jnp.abs(gf))))
    lane = lax.broadcasted_iota(jnp.int32, (half, CHUNK), 1)
    is_fwd = lax.broadcasted_iota(jnp.int32, (half, CHUNK), 0) < ML_HEADS
    for c in range(tm // CHUNK):
        sl = slice(c * CHUNK, (c + 1) * CHUNK)
        pre = suf = lf[:, sl]
        k = 1
        while k < CHUNK:
            pre = pre + jnp.where(lane >= k, pltpu.roll(pre, k, 1), 0.0)
            suf = suf + jnp.where(lane < CHUNK - k, pltpu.roll(suf, CHUNK - k, 1), 0.0)
            k *= 2
        b = jnp.where(is_fwd, pre, suf)
        o_ref[:half, sl] = li[:, sl] - b
        o_ref[half:, sl] = b


def _gates(hb, wg_t, bg, layer, bsz, lp):
    r, d = hb.shape
    tm = _pick(lp, (1408, 384, 128))
    nt = lp // tm
    return pl.pallas_call(
        _gates_kernel,
        grid=(bsz, nt),
        in_specs=[pl.BlockSpec((tm, d), lambda b, i: (b * nt + i, 0)),
                  pl.BlockSpec((None, N_GATES, d), lambda b, i: (layer, 0, 0)),
                  pl.BlockSpec((None, N_GATES, 1), lambda b, i: (layer, 0, 0))],
        out_specs=pl.BlockSpec((None, N_GATES, tm), lambda b, i: (b, 0, i)),
        out_shape=jax.ShapeDtypeStruct((bsz, N_GATES, lp), f32),
        compiler_params=_params(("parallel", "arbitrary"), 32),
        name="gates",
    )(hb, wg_t, bg)


MLSTM_UNROLL = 16


def _mlstm_kernel(q_ref, kt_ref, v_ref, o_ref, gq_ref, ng_ref, out_ref, hf_scr, hb_scr, c_scr, m_scr, s_scr):
    head = pl.program_id(1)
    lp, dh = q_ref.shape
    t = CHUNK
    nc = lp // t
    c_scr[...] = jnp.zeros_like(c_scr)
    m_scr[...] = jnp.zeros_like(m_scr)
    ti = lax.broadcasted_iota(jnp.int32, (t, t), 0)
    si = lax.broadcasted_iota(jnp.int32, (t, t), 1)
    masks = (si <= ti, si >= ti)
    eye = si == ti
    half = N_GATES // 2
    gate_row = lax.broadcasted_iota(jnp.int32, (half, t), 0)
    ones_col = jnp.where(lax.broadcasted_iota(jnp.int32, (t, LANE), 1) == 0, 1.0, 0.0).astype(bf16)

    def qk_ahead(c, dirn):
        base = pl.multiple_of(jnp.clip(c, 0, nc - 1) * t, t)
        s_scr[dirn] = _dot(q_ref[pl.ds(base, t), :], kt_ref[:, pl.ds(base, t)])

    def chunk(c, dirn, h_scr):
        base = pl.multiple_of(c * t, t)
        q = q_ref[pl.ds(base, t), :]
        kt = kt_ref[:, pl.ds(base, t)]
        v = jnp.concatenate([v_ref[pl.ds(base, t), :], ones_col], axis=1)
        mine = gate_row == head + ML_HEADS * dirn
        r_row = jnp.sum(jnp.where(mine, gq_ref[:half, pl.ds(base, t)], 0.0), axis=0, keepdims=True)
        b_row = jnp.sum(jnp.where(mine, gq_ref[half:, pl.ds(base, t)], 0.0), axis=0, keepdims=True)
        m_state = m_scr[dirn, 0:1, 0:1]
        mask = masks[dirn]
        m_col = jnp.maximum(jnp.max(jnp.where(mask, r_row, -jnp.inf), axis=1, keepdims=True), m_state)
        b_col = jnp.sum(jnp.where(eye, b_row, 0.0), axis=1, keepdims=True)
        m_all = jnp.maximum(jnp.max(r_row, axis=1, keepdims=True), m_state)
        b_last = b_row[:, 0:1] if dirn else b_row[:, t - 1:t]
        s = s_scr[dirn] * jnp.exp(jnp.where(mask, r_row - m_col, -jnp.inf))
        both = jnp.exp(m_state - m_col) * _dot(q, c_scr[dirn].astype(bf16)) + _dot(s.astype(bf16), v)
        floor = jnp.exp(-(b_col + m_col))
        h_scr[pl.ds(base, t), :] = both[:, :dh] / jnp.maximum(jnp.abs(both[:, dh:dh + 1]), floor)
        ktw = (kt.astype(f32) * jnp.exp(r_row - m_all)).astype(bf16)
        c_scr[dirn] = jnp.exp(m_state - m_all) * c_scr[dirn] + _dot(ktw, v)
        m_scr[dirn] = jnp.broadcast_to(b_last + m_all, m_scr.shape[1:])

    def finish(c):
        base = pl.multiple_of(c * t, t)
        hs = hf_scr[pl.ds(base, t), :] + hb_scr[pl.ds(base, t), :]
        mu = jnp.mean(hs, axis=1, keepdims=True)
        var = jnp.mean(jnp.square(hs - mu), axis=1, keepdims=True)
        hn = (hs - mu) * lax.rsqrt(var + EPS) * ng_ref[...]
        gate = jax.nn.sigmoid(o_ref[pl.ds(base, t), :].astype(f32))
        out_ref[pl.ds(base, t), :] = (gate * hn).astype(out_ref.dtype)

    def scan_step(c):
        chunk(c, 0, hf_scr)
        chunk(nc - 1 - c, 1, hb_scr)
        qk_ahead(c + 1, 0)
        qk_ahead(nc - 2 - c, 1)

    def first_half(c, carry):
        scan_step(c)
        return carry

    def second_half(c, carry):
        finish(c - 1)
        finish(nc - c)
        scan_step(c)
        return carry

    qk_ahead(0, 0)
    qk_ahead(nc - 1, 1)
    meet = (nc + 1) // 2
    lax.fori_loop(0, meet, first_half, 0, unroll=MLSTM_UNROLL)
    lax.fori_loop(meet, nc, second_half, 0, unroll=MLSTM_UNROLL)
    finish(nc - 1)
    finish(0)


def _mlstm(u, kt, gq, ml_norm_g, layer, bsz, lp):
    r = u.shape[0]
    dh = ML_HEAD_DIM

    def col(first):
        return pl.BlockSpec((lp, dh), lambda b, h: (b, first // dh + h))

    return pl.pallas_call(
        _mlstm_kernel,
        grid=(bsz, ML_HEADS),
        in_specs=[col(COL_Q), pl.BlockSpec((dh, lp), lambda b, h: (h, b)), col(COL_V), col(COL_O),
                  pl.BlockSpec((None, N_GATES, lp), lambda b, h: (b, 0, 0)),
                  pl.BlockSpec((None, 1, dh), lambda b, h: (layer, 0, h))],
        out_specs=pl.BlockSpec((lp, dh), lambda b, h: (b, h)),
        out_shape=jax.ShapeDtypeStruct((r, ML_WIDTH), bf16),
        scratch_shapes=[pltpu.VMEM((lp, dh), f32), pltpu.VMEM((lp, dh), f32),
                        pltpu.VMEM((2, dh, dh + LANE), f32), pltpu.VMEM((2, 8, LANE), f32),
                        pltpu.VMEM((2, CHUNK, CHUNK), f32)],
        compiler_params=_params(("parallel", "arbitrary"), 48),
        name="mlstm",
    )(u, kt, u, u, gq, ml_norm_g)


def _rms(x, g):
    return x * lax.rsqrt(jnp.mean(jnp.square(x), axis=-1, keepdims=True) + EPS) * g


def _rope(x, cos, sin_a, sin_b):
    n = x.shape[1]
    reps = n // LANE
    if reps > 1:
        cos, sin_a, sin_b = (jnp.concatenate([a] * reps, axis=1) for a in (cos, sin_a, sin_b))
    return x * cos + pltpu.roll(x, n - ROPE_HALF, 1) * sin_a + pltpu.roll(x, ROPE_HALF, 1) * sin_b


def _mla_prep_kernel(dq_ref, dkv_ref, kr_ref, wuq_ref, wukv_ref, qg_ref, kvg_ref,
                     cos_ref, sa_ref, sb_ref, qc_ref, kc_ref, v_ref):
    cos, sa, sb = cos_ref[...], sa_ref[...], sb_ref[...]
    q = _dot(_rms(dq_ref[...].astype(f32), qg_ref[...]).astype(bf16), wuq_ref[...]) * SOFTMAX_SCALE_LOG2
    kv =_dot(_rms(dkv_ref[...].astype(f32), kvg_ref[...]).astype(bf16), wukv_ref[...])
    q_rope = _rope(q[:, MLA_HEADS * QK_NOPE:], cos, sa, sb)
    k_rope = _rope(kr_ref[...].astype(f32), cos, sa, sb)
    tm = q.shape[0]
    lane = lax.broadcasted_iota(jnp.int32, (tm, LANE), 1)
    pos = pl.program_id(1) * tm + lax.broadcasted_iota(jnp.int32, (tm, LANE), 0)
    low = lane < QK_ROPE
    marker = lane == QK_ROPE
    one_hot = jnp.where(marker, 1.0, 0.0)
    k_tail = jnp.where(low, k_rope, jnp.where(marker & (pos < FRONT), -jnp.inf, 0.0)).astype(bf16)
    ones_col = jnp.where(lane == 0, 1.0, 0.0).astype(bf16)
    for h in range(MLA_HEADS):
        pair = q_rope[:, (h // 2) * LANE:(h // 2 + 1) * LANE]
        if h % 2:
            pair = pltpu.roll(pair, QK_ROPE, 1)
        qc_ref[h, :, :QK_NOPE] = q[:, h * QK_NOPE:(h + 1) * QK_NOPE].astype(bf16)
        qc_ref[h, :, QK_NOPE:] = jnp.where(low, pair, one_hot).astype(bf16)
        kc_ref[h, :, :QK_NOPE] = kv[:, h * QK_NOPE:(h + 1) * QK_NOPE].astype(bf16)
        kc_ref[h, :, QK_NOPE:] = k_tail
        v_ref[h, :, :V_HEAD] = kv[:, (MLA_HEADS + h) * V_HEAD:(MLA_HEADS + h + 1) * V_HEAD].astype(bf16)
        v_ref[h, :, V_HEAD:] = ones_col


def _mla_prep(u, w_uq_p, w_ukv_p, q_norm_g, kv_norm_g, tables, layer, bsz, lp):
    r = u.shape[0]
    tm = _pick(lp, (384, 128))
    nt = lp // tm
    kd = QK_NOPE + LANE
    tab = pl.BlockSpec((tm, LANE), lambda b, i: (i, 0))
    row = lambda b, i: b * nt + i
    return pl.pallas_call(
        _mla_prep_kernel,
        grid=(bsz, nt),
        in_specs=[pl.BlockSpec((tm, Q_LORA), lambda b, i: (row(b, i), COL_DQ // Q_LORA)),
                  pl.BlockSpec((tm, KV_LORA), lambda b, i: (row(b, i), COL_DKV // KV_LORA)),
                  pl.BlockSpec((tm, LANE), lambda b, i: (row(b, i), COL_KR // LANE)),
                  pl.BlockSpec((None,) + w_uq_p.shape[1:], lambda b, i: (layer, 0, 0)),
                  pl.BlockSpec((None,) + w_ukv_p.shape[1:], lambda b, i: (layer, 0, 0)),
                  pl.BlockSpec((None, 1, Q_LORA), lambda b, i: (layer, 0, 0)),
                  pl.BlockSpec((None, 1, KV_LORA), lambda b, i: (layer, 0, 0)),
                  tab, tab, tab],
        out_specs=[pl.BlockSpec((MLA_HEADS, tm, kd), lambda b, i: (0, row(b, i), 0)),
                   pl.BlockSpec((MLA_HEADS, tm, kd), lambda b, i: (0, row(b, i), 0)),
                   pl.BlockSpec((MLA_HEADS, tm, 2 * V_HEAD), lambda b, i: (0, row(b, i), 0))],
        out_shape=[jax.ShapeDtypeStruct((MLA_HEADS, r, kd), bf16),
                   jax.ShapeDtypeStruct((MLA_HEADS, r, kd), bf16),
                   jax.ShapeDtypeStruct((MLA_HEADS, r, 2 * V_HEAD), bf16)],
        compiler_params=_params(("parallel", "arbitrary"), 48),
        name="mla_prep",
    )(u, u, u, w_uq_p, w_ukv_p, q_norm_g, kv_norm_g, *tables)


ATTN_HEADS_PER_STEP = 2
ATTN_TILES_PER_BODY = 12


def _attn_kernel(q_ref, k_ref, v_ref, o_ref, s0_ref, s1_ref, p0_ref, p1_ref, *, tq):
    nh, lp, _ = k_ref.shape
    nq = lp // tq
    n_tiles = nh * nq
    s_refs, p_refs = (s0_ref, s1_ref), (p0_ref, p1_ref)

    @pl.when((pl.program_id(0) == 0) & (pl.program_id(1) == 0))
    def _():
        for ref in s_refs + p_refs:
            ref[...] = jnp.zeros_like(ref)

    def tile(x):
        x = jnp.clip(x, 0, n_tiles - 1)
        hh = x // nq
        return hh, pl.ds(pl.multiple_of((x - hh * nq) * tq, tq), tq)

    def scores(x, slot):
        hh, rows = tile(x)
        s_refs[slot][...] = _dot_nt(q_ref[hh, rows, :], k_ref[hh])

    def softmax(slot):
        s = s_refs[slot][...]
        p_refs[slot][...] = jnp.exp2(s - jnp.max(s, axis=1, keepdims=True)).astype(bf16)

    def weighted(x, slot):
        hh, rows = tile(x)
        acc = _dot(p_refs[slot][...], v_ref[hh])
        o_ref[hh, rows, :] = (acc[:, :V_HEAD] * (1.0 / acc[:, V_HEAD:V_HEAD + 1])).astype(o_ref.dtype)

    n_bodies = -(-(n_tiles + 2) // ATTN_TILES_PER_BODY)
    start = n_tiles + 2 - ATTN_TILES_PER_BODY * n_bodies

    def body(u, carry):
        for e in range(ATTN_TILES_PER_BODY):
            x = ATTN_TILES_PER_BODY * u + e + start
            weighted(x - 2, e % 2)
            softmax((e + 1) % 2)
            scores(x, e % 2)
        return carry

    lax.fori_loop(0, n_bodies, body, 0)


def _attention(qc, kc, v, bsz, lp):
    r = v.shape[1]
    tq = _pick(lp, (384, 128))
    kd = qc.shape[-1]
    nh = ATTN_HEADS_PER_STEP
    blk = lambda width: pl.BlockSpec((nh, lp, width), lambda b, h: (h, b, 0))
    return pl.pallas_call(
        functools.partial(_attn_kernel, tq=tq),
        grid=(bsz, MLA_HEADS // nh),
        in_specs=[blk(kd), blk(kd), blk(2 * V_HEAD)],
        out_specs=blk(V_HEAD),
        out_shape=jax.ShapeDtypeStruct((MLA_HEADS, r, V_HEAD), bf16),
        scratch_shapes=[pltpu.VMEM((tq, lp), f32), pltpu.VMEM((tq, lp), f32),
                        pltpu.VMEM((tq, lp), bf16), pltpu.VMEM((tq, lp), bf16)],
        compiler_params=_params(("arbitrary", "arbitrary"), 60),
        name="attention",
    )(qc, kc, v)


def _layer_norm(y, g, b):
    mu = jnp.mean(y, axis=-1, keepdims=True)
    var = jnp.mean(jnp.square(y - mu), axis=-1, keepdims=True)
    return (y - mu) * lax.rsqrt(var + EPS) * g + b


OUTPROJ_SUBTILES = 2


def _outproj_kernel(ml_ref, at_ref, w_ref, h_ref, g_ref, b_ref, o_ref, ob_ref, *, alpha):
    ts = h_ref.shape[0] // OUTPROJ_SUBTILES
    for sub in range(OUTPROJ_SUBTILES):
        rows = slice(sub * ts, (sub + 1) * ts)
        at = jnp.concatenate([at_ref[h, rows, :] for h in range(MLA_HEADS)], axis=1)
        mix = _dot(ml_ref[rows, :], w_ref[:ML_WIDTH, :]) + _dot(at, w_ref[ML_WIDTH:, :])
        y = _layer_norm(alpha * h_ref[rows, :] + mix, g_ref[...], b_ref[...])
        o_ref[rows, :] = y
        ob_ref[rows, :] = y.astype(bf16)


def _outproj(ml, at, w_out, h, ln_g, ln_b, layer, alpha):
    r, d = h.shape
    tm = _pick(r, (768, 384, 128))
    vec = pl.BlockSpec((None, 1, d), lambda i: (layer, 0, 0))
    return pl.pallas_call(
        functools.partial(_outproj_kernel, alpha=alpha),
        grid=(r // tm,),
        in_specs=[pl.BlockSpec((tm, ML_WIDTH), lambda i: (i, 0)),
                  pl.BlockSpec((MLA_HEADS, tm, V_HEAD), lambda i: (0, i, 0)),
                  pl.BlockSpec((None,) + w_out.shape[1:], lambda i: (layer, 0, 0), pipeline_mode=pl.Buffered(1)),
                  pl.BlockSpec((tm, d), lambda i: (i, 0)), vec, vec],
        out_specs=[pl.BlockSpec((tm, d), lambda i: (i, 0)), pl.BlockSpec((tm, d), lambda i: (i, 0))],
        out_shape=[jax.ShapeDtypeStruct((r, d), f32), jax.ShapeDtypeStruct((r, d), bf16)],
        compiler_params=_params(("parallel",), 56),
        name="outproj_ln",
    )(ml, at, w_out, h, ln_g, ln_b)


HALO = 16


def _ffn_up_kernel(prev_ref, main_ref, next_ref, wg_ref, wv_ref, cw_ref, cb_ref, o_ref, ext_scr, gate_scr, *, lp):
    tm = main_ref.shape[0]

    @pl.when(pl.program_id(2) == 0)
    def _():
        ext_scr[:HALO, :] = prev_ref[...]
        ext_scr[HALO:HALO + tm, :] = main_ref[...]
        ext_scr[HALO + tm:, :] = next_ref[...]

    def compute(lo):
        pos = pl.program_id(1) * tm - HALO + lo + lax.broadcasted_iota(jnp.int32, (tm + 2 * HALO - lo, 1), 0)
        inside = (pos >= FRONT) & (pos < lp)
        gate_scr[lo:, :] = jnp.where(inside, _dot(ext_scr[lo:, :], wg_ref[...]), 0.0)
        taps = [gate_scr[pl.ds(lo + HALO - 1 + j, tm - lo), :] * cw_ref[j:j + 1, :] for j in range(3)]
        conv = taps[0] + taps[1] + taps[2] + cb_ref[...]
        val = _dot(main_ref[lo:, :], wv_ref[...])
        o_ref[lo:, :] = (conv * jax.nn.sigmoid(conv) * val).astype(o_ref.dtype)

    @pl.when(pl.program_id(1) == 0)
    def _():
        o_ref[:FRONT, :] = jnp.zeros((FRONT, o_ref.shape[1]), o_ref.dtype)
        compute(FRONT)

    @pl.when(pl.program_id(1) != 0)
    def _():
        compute(0)


def _ffn_up(hb, w_up, conv_w, conv_b, layer, bsz, lp):
    r, d = hb.shape
    dff = conv_b.shape[-1]
    tm = _pick(lp, (2112, 1408, 384, 128))
    tn = _pick(dff, (512, 256, 128))
    nt, nj = lp // tm, dff // tn
    hblk = tm // HALO
    last = r // HALO - 1
    return pl.pallas_call(
        functools.partial(_ffn_up_kernel, lp=lp),
        grid=(bsz, nt, nj),
        in_specs=[pl.BlockSpec((HALO, d), lambda b, i, j: (jnp.maximum((b * nt + i) * hblk - 1, 0), 0)),
                  pl.BlockSpec((tm, d), lambda b, i, j: (b * nt + i, 0)),
                  pl.BlockSpec((HALO, d), lambda b, i, j: (jnp.minimum((b * nt + i + 1) * hblk, last), 0)),
                  pl.BlockSpec((None, d, tn), lambda b, i, j: (layer, 0, j)),
                  pl.BlockSpec((None, d, tn), lambda b, i, j: (layer, 0, nj + j)),
                  pl.BlockSpec((None, 3, tn), lambda b, i, j: (layer, 0, j)),
                  pl.BlockSpec((None, 1, tn), lambda b, i, j: (layer, 0, j))],
        out_specs=pl.BlockSpec((tm, tn), lambda b, i, j: (b * nt + i, j)),
        out_shape=jax.ShapeDtypeStruct((r, dff), bf16),
        scratch_shapes=[pltpu.VMEM((tm + 2 * HALO, d), bf16), pltpu.VMEM((tm + 2 * HALO, tn), f32)],
        compiler_params=_params(("parallel", "parallel", "arbitrary"), 56),
        name="ffn_up_conv",
    )(hb, hb, hb, w_up, w_up, conv_w, conv_b)


FFN_DOWN_SUBTILES = 2


def _ffn_down_kernel(a_ref, w_ref, h_ref, g_ref, b_ref, o_ref, ob_ref, *, alpha, tiles_per_seq):
    tm = h_ref.shape[0]

    def compute(rows):
        y = _layer_norm(alpha * h_ref[rows, :] + _dot(a_ref[rows, :], w_ref[...]), g_ref[...], b_ref[...])
        o_ref[rows, :] = y
        ob_ref[rows, :] = y.astype(bf16)

    first = pl.program_id(0) % tiles_per_seq == 0

    @pl.when(first)
    def _():
        o_ref[:FRONT, :] = jnp.zeros((FRONT, o_ref.shape[1]), o_ref.dtype)
        ob_ref[:FRONT, :] = jnp.zeros((FRONT, ob_ref.shape[1]), ob_ref.dtype)
        compute(slice(FRONT, tm))

    @pl.when(jnp.logical_not(first))
    def _():
        ts = tm // FFN_DOWN_SUBTILES
        for sub in range(FFN_DOWN_SUBTILES):
            compute(slice(sub * ts, (sub + 1) * ts))


def _ffn_down(act, w_down, h, ln_g, ln_b, layer, alpha, lp):
    r, d = h.shape
    dff = act.shape[1]
    tm = _pick(lp, (384, 128))
    vec = pl.BlockSpec((None, 1, d), lambda i: (layer, 0, 0))
    return pl.pallas_call(
        functools.partial(_ffn_down_kernel, alpha=alpha, tiles_per_seq=lp // tm),
        grid=(r // tm,),
        in_specs=[pl.BlockSpec((tm, dff), lambda i: (i, 0)),
                  pl.BlockSpec((None, dff, d), lambda i: (layer, 0, 0), pipeline_mode=pl.Buffered(1)),
                  pl.BlockSpec((tm, d), lambda i: (i, 0)), vec, vec],
        out_specs=[pl.BlockSpec((tm, d), lambda i: (i, 0)), pl.BlockSpec((tm, d), lambda i: (i, 0))],
        out_shape=[jax.ShapeDtypeStruct((r, d), f32), jax.ShapeDtypeStruct((r, d), bf16)],
        compiler_params=_params(("parallel",), 60),
        name="ffn_down_ln",
    )(act, w_down, h, ln_g, ln_b)


def _rope_tables(lp):
    inv_freq = ROPE_THETA ** (-jnp.arange(0, QK_ROPE, 2, dtype=f32) / QK_ROPE)
    pos = jnp.maximum(jnp.arange(lp, dtype=f32) - FRONT, 0.0)
    ang = pos[:, None] * inv_freq[None, :]
    cos, sin = jnp.cos(ang), jnp.sin(ang)
    zero = jnp.zeros_like(sin)
    reps = LANE // QK_ROPE
    cos_t = jnp.tile(jnp.concatenate([cos, cos], axis=1), (1, reps))
    sin_a = jnp.tile(jnp.concatenate([-sin, zero], axis=1), (1, reps))
    sin_b = jnp.tile(jnp.concatenate([zero, sin], axis=1), (1, reps))
    return cos_t, sin_a, sin_b


def kernel(x_prompt, x_sample, meta_tokens, w_in, b_gates, ml_norm_g, q_norm_g, kv_norm_g, w_uq, w_ukv,
           w_out, ln1_g, ln1_b, w_up, conv_w, conv_b, w_down, ln2_g, ln2_b):
    depth, d, _ = w_in.shape
    assert x_prompt.shape[1:] == x_sample.shape[1:]
    n_prompt = x_prompt.shape[0]
    x = jnp.concatenate([x_prompt, x_sample], axis=0)
    bsz, seq, _ = x.shape
    assert seq % LANE == 0
    lp = LANE + seq
    alpha = (2 * depth) ** 0.25

    gate_lo = 4 * ML_WIDTH
    w_main = jnp.concatenate(
        [w_in[..., :ML_WIDTH], w_in[..., 2 * ML_WIDTH:gate_lo], w_in[..., gate_lo + N_GATES:], w_in[..., -QK_ROPE:],
         jnp.zeros((depth, d, N_MAIN - COL_KR - 2 * QK_ROPE), w_in.dtype)], axis=-1).astype(bf16)
    wk_t = jnp.swapaxes(w_in[..., ML_WIDTH:2 * ML_WIDTH], 1, 2).astype(bf16)
    h4 = ML_HEADS
    perm = jnp.array(list(range(0, h4)) + list(range(2 * h4, 3 * h4))
                     + list(range(h4, 2 * h4)) + list(range(3 * h4, 4 * h4)))
    wg_t = jnp.swapaxes(w_in[..., gate_lo:gate_lo + N_GATES], 1, 2)[:, perm, :].astype(bf16)
    bg = b_gates[:, perm, None].astype(f32)
    uq = w_uq.reshape(depth, Q_LORA, MLA_HEADS, QK_NOPE + QK_ROPE)
    w_uq_p = jnp.concatenate([uq[..., :QK_NOPE].reshape(depth, Q_LORA, -1),
                              uq[..., QK_NOPE:].reshape(depth, Q_LORA, -1)], axis=-1).astype(bf16)
    ukv = w_ukv.reshape(depth, KV_LORA, MLA_HEADS, QK_NOPE + V_HEAD)
    w_ukv_p = jnp.concatenate([ukv[..., :QK_NOPE].reshape(depth, KV_LORA, -1),
                               ukv[..., QK_NOPE:].reshape(depth, KV_LORA, -1)], axis=-1).astype(bf16)
    w_out_b, w_up_b, w_down_b = w_out.astype(bf16), w_up.astype(bf16), w_down.astype(bf16)
    vec3 = lambda a: a[:, None, :].astype(f32)
    ml_g, q_g, kv_g = vec3(ml_norm_g), vec3(q_norm_g), vec3(kv_norm_g)
    l1g, l1b, l2g, l2b, cb = vec3(ln1_g), vec3(ln1_b), vec3(ln2_g), vec3(ln2_b), vec3(conv_b)
    cw = conv_w.astype(f32)
    tables = _rope_tables(lp)

    meta = jnp.broadcast_to(meta_tokens.astype(x.dtype)[None], (bsz, N_META, d))
    h = jnp.concatenate([jnp.zeros((bsz, FRONT, d), x.dtype), meta, x], axis=1).reshape(bsz * lp, d)
    hb = h.astype(bf16)
    for layer in range(depth):
        u = _inproj(hb, w_main, layer, lp)
        gq = _gates(hb, wg_t, bg, layer, bsz, lp)
        kt = _keys_t(hb, wk_t, layer)
        ml = _mlstm(u, kt, gq, ml_g, layer, bsz, lp)
        qc, kc, v = _mla_prep(u, w_uq_p, w_ukv_p, q_g, kv_g, tables, layer, bsz, lp)
        at = _attention(qc, kc, v, bsz, lp)
        h, hb = _outproj(ml, at, w_out_b, h, l1g, l1b, layer, alpha)
        act = _ffn_up(hb, w_up_b, cw, cb, layer, bsz, lp)
        h, hb = _ffn_down(act, w_down_b, h, l2g, l2b, layer, alpha, lp)
    y = h.reshape(bsz, lp, d)[:, LANE:]
    return (y[:n_prompt], y[n_prompt:])
```
